```python
import math
import jax
import jax.numpy as jnp
from jax import lax
import numpy as np

D_MODEL = 2048
BATCH = 4
SEQ = 4096
DEPTH = 4

MLA_HEADS = 8
MLA_Q_RANK = 512
MLA_KV_RANK = 256
MLA_NOPE = 64
MLA_ROPE = 32
MLA_V = 64
ROPE_BASE = 10000.0
Q_BLOCK = 128
SWA_HEADS = 8
SWA_KV_HEADS = 2
SWA_GROUP = SWA_HEADS // SWA_KV_HEADS
SWA_HEAD_DIM = 64
WINDOW = 128
GDN_HEADS = 8
GDN_DK = 128
GDN_DV = 128
CONV_WIDTH = 4
CHUNK = 64
D_FF = 7168
N_EXPERTS = 8
TOP_K = 2
N_DENSE = (DEPTH + 1) // 2
N_MOE = DEPTH // 2
DN_ALPHA = float((2 * DEPTH) ** 0.25)
DN_BETA = float((8 * DEPTH) ** -0.25)
LN_EPS = 1e-5
RMS_EPS = 1e-6

MLA_IN = MLA_Q_RANK + MLA_KV_RANK + MLA_ROPE
SWA_IN = (SWA_HEADS + 2 * SWA_KV_HEADS) * SWA_HEAD_DIM
GDN_QKV = GDN_HEADS * (2 * GDN_DK + GDN_DV)
GDN_IN = GDN_QKV + 2 * GDN_HEADS + GDN_HEADS * GDN_DV
D_IN = MLA_IN + SWA_IN + GDN_IN
MIX_WIDTH = MLA_HEADS * MLA_V + SWA_HEADS * SWA_HEAD_DIM + GDN_HEADS * GDN_DV

kernel_name = "hybrid_mla_swa_gdn_moe_deepnorm"


def layer_norm(x, g, b):
    xf = x.astype(jnp.float32)
    mu = jnp.mean(xf, axis=-1, keepdims=True)
    var = jnp.mean(jnp.square(xf - mu), axis=-1, keepdims=True)
    return ((xf - mu) * lax.rsqrt(var + LN_EPS) * g + b).astype(x.dtype)


def rms_norm(x, g):
    xf = x.astype(jnp.float32)
    return (xf * lax.rsqrt(jnp.mean(xf * xf, axis=-1, keepdims=True) + RMS_EPS) * g).astype(x.dtype)


def l2_normalize(x):
    return x * lax.rsqrt(jnp.sum(x * x, axis=-1, keepdims=True) + RMS_EPS)


def rope_tables(pos, dim):
    half = dim // 2
    inv_freq = ROPE_BASE ** (-jnp.arange(half, dtype=jnp.float32) / half)
    ang = pos.astype(jnp.float32)[:, None] * inv_freq[None, :]
    return jnp.cos(ang), jnp.sin(ang)


def apply_rope(x, cos, sin):
    x1, x2 = jnp.split(x.astype(jnp.float32), 2, axis=-1)
    return jnp.concatenate([x1 * cos - x2 * sin, x2 * cos + x1 * sin], axis=-1).astype(x.dtype)


def alibi_slopes(n):
    return 2.0 ** (-8.0 * jnp.arange(1, n + 1, dtype=jnp.float32) / n)


def mla_attention(p, q_norm, w_uq, kv_norm, w_ukv, pos, cos, sin):
    B, S, _ = p.shape
    c_q, c_kv, k_rope = jnp.split(p, [MLA_Q_RANK, MLA_Q_RANK + MLA_KV_RANK], axis=-1)
    q = (rms_norm(c_q, q_norm) @ w_uq).reshape(B, S, MLA_HEADS, MLA_NOPE + MLA_ROPE)
    kv = (rms_norm(c_kv, kv_norm) @ w_ukv).reshape(B, S, MLA_HEADS, MLA_NOPE + MLA_V)
    q_nope, q_rope = jnp.split(q, [MLA_NOPE], axis=-1)
    k_nope, v = jnp.split(kv, [MLA_NOPE], axis=-1)
    q_rope = apply_rope(q_rope, cos[:, None, :], sin[:, None, :])
    k_rope = apply_rope(k_rope, cos, sin)
    scale = (MLA_NOPE + MLA_ROPE) ** -0.5

    def query_block(i):
        start = i * Q_BLOCK
        qn = lax.dynamic_slice_in_dim(q_nope, start, Q_BLOCK, axis=1)
        qr = lax.dynamic_slice_in_dim(q_rope, start, Q_BLOCK, axis=1)
        q_pos = lax.dynamic_slice_in_dim(pos, start, Q_BLOCK)
        s = (jnp.einsum('bqhd,bkhd->bhqk', qn, k_nope)
             + jnp.einsum('bqhd,bkd->bhqk', qr, k_rope)).astype(jnp.float32) * scale
        s = jnp.where(pos[None, :] <= q_pos[:, None], s, -jnp.inf)
        prob = jax.nn.softmax(s, axis=-1).astype(v.dtype)
        return jnp.einsum('bhqk,bkhd->bqhd', prob, v)

    out = lax.map(query_block, jnp.arange(S // Q_BLOCK))
    return out.transpose(1, 0, 2, 3, 4).reshape(B, S, MLA_HEADS * MLA_V)


def swa_attention(p, sinks):
    B, S, _ = p.shape
    nb = S // WINDOW
    hd = SWA_HEAD_DIM
    q, k, v = jnp.split(p, [SWA_HEADS * hd, (SWA_HEADS + SWA_KV_HEADS) * hd], axis=-1)
    q = q.reshape(B, nb, WINDOW, SWA_KV_HEADS, SWA_GROUP, hd)
    k = k.reshape(B, nb, WINDOW, SWA_KV_HEADS, hd)
    v = v.reshape(B, nb, WINDOW, SWA_KV_HEADS, hd)

    def with_prev(t):
        prev = jnp.pad(t[:, :-1], ((0, 0), (1, 0), (0, 0), (0, 0), (0, 0)))
        return jnp.concatenate([prev, t], axis=2)

    kb, vb = with_prev(k), with_prev(v)
    s = jnp.einsum('bnqgrd,bnkgd->bngrqk', q, kb).astype(jnp.float32) * hd ** -0.5
    qi = jnp.arange(WINDOW)
    kj = jnp.arange(2 * WINDOW)
    delta = qi[:, None] + WINDOW - kj[None, :]
    key_abs = jnp.arange(nb)[:, None] * WINDOW - WINDOW + kj[None, :]
    valid = ((delta >= 0) & (delta < WINDOW))[None] & (key_abs >= 0)[:, None, :]
    slopes = alibi_slopes(SWA_HEADS).reshape(SWA_KV_HEADS, SWA_GROUP)
    s = s - slopes[:, :, None, None] * delta.astype(jnp.float32)
    s = jnp.where(valid[:, None, None], s, -jnp.inf)
    sink_col = jnp.broadcast_to(
        sinks.astype(jnp.float32).reshape(SWA_KV_HEADS, SWA_GROUP)[:, :, None, None],
        s.shape[:-1] + (1,))
    prob = jax.nn.softmax(jnp.concatenate([s, sink_col], axis=-1), axis=-1)[..., :-1].astype(v.dtype)
    o = jnp.einsum('bngrqk,bnkgd->bnqgrd', prob, vb)
    return o.reshape(B, S, SWA_HEADS * hd)


def causal_conv(x, w):
    K = w.shape[0]
    S = x.shape[1]
    xp = jnp.pad(x, ((0, 0), (K - 1, 0), (0, 0)))
    return sum(xp[:, k:k + S] * w[k] for k in range(K))


def gated_deltanet(p, conv_w, a_log, dt_bias, norm_g):
    B, S, _ = p.shape
    H, DK, DV = GDN_HEADS, GDN_DK, GDN_DV
    f32 = jnp.float32
    qkv, b, a, z = jnp.split(p, [GDN_QKV, GDN_QKV + H, GDN_QKV + 2 * H], axis=-1)
    qkv = jax.nn.silu(causal_conv(qkv, conv_w))
    q, k, v = jnp.split(qkv, [H * DK, 2 * H * DK], axis=-1)
    q = l2_normalize(q.reshape(B, S, H, DK).astype(f32)) * DK ** -0.5
    k = l2_normalize(k.reshape(B, S, H, DK).astype(f32))
    v = v.reshape(B, S, H, DV).astype(f32)
    beta = jax.nn.sigmoid(b.astype(f32))
    g = -jnp.exp(a_log.astype(f32)) * jax.nn.softplus(a.astype(f32) + dt_bias.astype(f32))
    N = S // CHUNK

    def chunks(t):
        return jnp.moveaxis(t.reshape((B, N, CHUNK, H) + t.shape[3:]), 3, 1)

    q, k, v, beta, g = chunks(q), chunks(k), chunks(v), chunks(beta), chunks(g)
    G = jnp.cumsum(g, axis=-1)
    i = jnp.arange(CHUNK)
    incl = i[:, None] >= i[None, :]
    strict = i[:, None] > i[None, :]
    decay = jnp.exp(jnp.where(incl, G[..., :, None] - G[..., None, :], -jnp.inf))
    kb = k * beta[..., None]
    L = jnp.einsum('bhnid,bhnjd->bhnij', kb, k) * jnp.where(strict, decay, 0.0)
    rhs = jnp.concatenate([v * beta[..., None], kb * jnp.exp(G)[..., None]], axis=-1)
    sol = lax.linalg.triangular_solve(L, rhs, left_side=True, lower=True, unit_diagonal=True)
    u, w = jnp.split(sol, [DV], axis=-1)
    a_qk = jnp.einsum('bhnid,bhnjd->bhnij', q, k) * decay
    q_dec = q * jnp.exp(G)[..., None]
    G_last = G[..., -1:]
    k_dec = k * jnp.exp(G_last - G)[..., None]
    chunk_decay = jnp.exp(G_last)[..., None]
    xs = tuple(jnp.moveaxis(t, 2, 0) for t in (u, w, a_qk, q_dec, k_dec, chunk_decay))

    def step(state, inp):
        u_n, w_n, a_n, q_n, k_n, d_n = inp
        v_new = u_n - jnp.einsum('bhck,bhkv->bhcv', w_n, state)
        o_n = jnp.einsum('bhck,bhkv->bhcv', q_n, state) + jnp.einsum('bhij,bhjv->bhiv', a_n, v_new)
        state = state * d_n + jnp.einsum('bhck,bhcv->bhkv', k_n, v_new)
        return state, o_n

    _, o = lax.scan(step, jnp.zeros((B, H, DK, DV), f32), xs)
    o = o.transpose(1, 0, 3, 2, 4).reshape(B, S, H, DV)
    o = rms_norm(o, norm_g) * jax.nn.silu(z.reshape(B, S, H, DV).astype(f32))
    return o.reshape(B, S, H * DV).astype(p.dtype)


def swiglu(h, w_gate, w_up, w_down):
    return (jax.nn.silu(h @ w_gate) * (h @ w_up)) @ w_down


def moe_swiglu(h, router, w_gate, w_up, w_down):
    B, S, D = h.shape
    t = h.reshape(B * S, D)
    logits = (t @ router).astype(jnp.float32)
    top_val, top_idx = lax.top_k(logits, TOP_K)
    gates = jax.nn.softmax(top_val, axis=-1)
    combine = jnp.sum(jax.nn.one_hot(top_idx, N_EXPERTS, dtype=jnp.float32) * gates[..., None], axis=1)
    y = jnp.zeros_like(t)
    for e in range(N_EXPERTS):
        y = y + combine[:, e:e + 1].astype(t.dtype) * swiglu(t, w_gate[e], w_up[e], w_down[e])
    return y.reshape(B, S, D)


def setup_inputs(seed: int = 0) -> dict:
    key = jax.random.key(seed)
    ks = jax.random.split(key, 24)
    f32 = jnp.float32
    L = DEPTH

    def nrm(k, shape, scale):
        return jax.random.normal(k, shape, f32) * scale

    def gain(k, shape):
        return 1.0 + 0.02 * jax.random.normal(k, shape, f32)

    dt = jnp.exp(jax.random.uniform(ks[9], (L, GDN_HEADS), f32, math.log(1e-3), math.log(1e-1)))
    return {
        'x': nrm(ks[0], (BATCH, SEQ, D_MODEL), 1.0),
        'w_in': nrm(ks[1], (L, D_MODEL, D_IN), D_MODEL ** -0.5),
        'mla_q_norm': gain(ks[2], (L, MLA_Q_RANK)),
        'mla_w_uq': nrm(ks[3], (L, MLA_Q_RANK, MLA_HEADS * (MLA_NOPE + MLA_ROPE)), MLA_Q_RANK ** -0.5),
        'mla_kv_norm': gain(ks[4], (L, MLA_KV_RANK)),
        'mla_w_ukv': nrm(ks[5], (L, MLA_KV_RANK, MLA_HEADS * (MLA_NOPE + MLA_V)), MLA_KV_RANK ** -0.5),
        'swa_sinks': nrm(ks[6], (L, SWA_HEADS), 0.5),
        'gdn_conv': nrm(ks[7], (L, CONV_WIDTH, GDN_QKV), CONV_WIDTH ** -0.5),
        'gdn_a_log': jnp.log(jax.random.uniform(ks[8], (L, GDN_HEADS), f32, 1.0, 16.0)),
        'gdn_dt_bias': dt + jnp.log(-jnp.expm1(-dt)),
        'gdn_norm': gain(ks[10], (L, GDN_DV)),
        'w_out': nrm(ks[11], (L, MIX_WIDTH, D_MODEL), DN_BETA * MIX_WIDTH ** -0.5),
        'ln1_g': gain(ks[12], (L, D_MODEL)),
        'ln1_b': nrm(ks[13], (L, D_MODEL), 0.02),
        'ln2_g': gain(ks[14], (L, D_MODEL)),
        'ln2_b': nrm(ks[15], (L, D_MODEL), 0.02),
        'ffn_w_gate': nrm(ks[16], (N_DENSE, D_MODEL, D_FF), D_MODEL ** -0.5),
        'ffn_w_up': nrm(ks[17], (N_DENSE, D_MODEL, D_FF), D_MODEL ** -0.5),
        'ffn_w_down': nrm(ks[18], (N_DENSE, D_FF, D_MODEL), DN_BETA * D_FF ** -0.5),
        'moe_router': nrm(ks[19], (N_MOE, D_MODEL, N_EXPERTS), D_MODEL ** -0.5),
        'moe_w_gate': nrm(ks[20], (N_MOE, N_EXPERTS, D_MODEL, D_FF), D_MODEL ** -0.5),
        'moe_w_up': nrm(ks[21], (N_MOE, N_EXPERTS, D_MODEL, D_FF), D_MODEL ** -0.5),
        'moe_w_down': nrm(ks[22], (N_MOE, N_EXPERTS, D_FF, D_MODEL), DN_BETA * D_FF ** -0.5),
    }


def reference(x, w_in, mla_q_norm, mla_w_uq, mla_kv_norm, mla_w_ukv, swa_sinks, gdn_conv,
              gdn_a_log, gdn_dt_bias, gdn_norm, w_out, ln1_g, ln1_b, ln2_g, ln2_b,
              ffn_w_gate, ffn_w_up, ffn_w_down, moe_router, moe_w_gate, moe_w_up, moe_w_down):
    pos = jnp.arange(x.shape[1], dtype=jnp.int32)
    cos, sin = rope_tables(pos, MLA_ROPE)
    for l in range(DEPTH):
        proj = x @ w_in[l]
        p_mla, p_swa, p_gdn = jnp.split(proj, [MLA_IN, MLA_IN + SWA_IN], axis=-1)
        heads = jnp.concatenate([
            mla_attention(p_mla, mla_q_norm[l], mla_w_uq[l], mla_kv_norm[l], mla_w_ukv[l], pos, cos, sin),
            swa_attention(p_swa, swa_sinks[l]),
            gated_deltanet(p_gdn, gdn_conv[l], gdn_a_log[l], gdn_dt_bias[l], gdn_norm[l]),
        ], axis=-1)
        x = layer_norm(DN_ALPHA * x + heads @ w_out[l], ln1_g[l], ln1_b[l])
        if l % 2 == 0:
            f = swiglu(x, ffn_w_gate[l // 2], ffn_w_up[l // 2], ffn_w_down[l // 2])
        else:
            f = moe_swiglu(x, moe_router[l // 2], moe_w_gate[l // 2], moe_w_up[l // 2], moe_w_down[l // 2])
        x = layer_norm(DN_ALPHA * x + f, ln2_g[l], ln2_b[l])
    return x
```

```python
import functools
import math

import jax
import jax.numpy as jnp
from jax import lax
from jax.experimental import pallas as pl
from jax.experimental.pallas import tpu as pltpu

F32 = jnp.float32
BF16 = jnp.bfloat16

D_MODEL = 2048
DEPTH = 4
MLA_HEADS = 8
MLA_Q_RANK = 512
MLA_KV_RANK = 256
MLA_NOPE = 64
MLA_ROPE = 32
MLA_V = 64
ROPE_BASE = 10000.0
SWA_HEADS = 8
SWA_KV_HEADS = 2
SWA_GROUP = SWA_HEADS // SWA_KV_HEADS
SWA_HEAD_DIM = 64
WINDOW = 128
GDN_HEADS = 8
GDN_DK = 128
GDN_DV = 128
CONV_WIDTH = 4
CHUNK = 64
D_FF = 7168
N_EXPERTS = 8
TOP_K = 2
DN_ALPHA = float((2 * DEPTH) ** 0.25)
LN_EPS = 1e-5
RMS_EPS = 1e-6

MLA_IN = MLA_Q_RANK + MLA_KV_RANK + MLA_ROPE
SWA_IN = (SWA_HEADS + 2 * SWA_KV_HEADS) * SWA_HEAD_DIM
GDN_QKV = GDN_HEADS * (2 * GDN_DK + GDN_DV)

LANES = 128
NEG_BIG = -1e30
VMEM_LIMIT = 56 * 1024 * 1024

COL_CQ = 0
COL_SWAQ = 512
COL_Z = 1024
COL_GQ = 2048
COL_GK = 3072
COL_GV = 4096
COL_CKV = 5120
COL_SWAK = 5376
COL_SWAV = 5504
COL_KR = 5632
COL_BA = 5760
D_PROJ = 6144

MLA_HP = 128


def _cparams(sem, vmem=VMEM_LIMIT):
    return pltpu.CompilerParams(dimension_semantics=sem, vmem_limit_bytes=vmem)


def _layer_norm_rows(y, g, b):
    mu = jnp.mean(y, axis=-1, keepdims=True)
    d = y - mu
    var = jnp.mean(d * d, axis=-1, keepdims=True)
    return d * lax.rsqrt(var + LN_EPS) * g + b


def _sigmoid(x):
    return 1.0 / (1.0 + jnp.exp(-x))


def _silu(x):
    return x * _sigmoid(x)


def _dot(a, b):
    return jnp.dot(a, b, preferred_element_type=F32)


def _dot_nt(a, b):
    return lax.dot_general(a, b, (((1,), (1,)), ((), ())), preferred_element_type=F32)


def _matmul_kernel(x_ref, w_ref, o_ref):
    o_ref[...] = _dot(x_ref[...], w_ref[...])


def _in_proj(xb, w):
    T = xb.shape[0]
    tm, tn = 1024, 512
    return pl.pallas_call(
        _matmul_kernel,
        out_shape=jax.ShapeDtypeStruct((T, D_PROJ), F32),
        grid=(T // tm, D_PROJ // tn),
        in_specs=[pl.BlockSpec((tm, D_MODEL), lambda i, j: (i, 0)),
                  pl.BlockSpec((D_MODEL, tn), lambda i, j: (0, j))],
        out_specs=pl.BlockSpec((tm, tn), lambda i, j: (i, j)),
        compiler_params=_cparams(("parallel", "arbitrary")),
        name="in_proj",
    )(xb, w)


def _mla_prep_kernel(cq_ref, ckv_ref, kr_ref, gq_ref, gkv_ref, wq_ref, wqs_ref, wk_ref, wv_ref,
                     e_ref, cqt_ref, sqt_ref, ckt_ref, q_ref, k_ref, v_ref):
    cq = cq_ref[...]
    hq = cq * lax.rsqrt(jnp.mean(cq * cq, axis=-1, keepdims=True) + RMS_EPS) * gq_ref[...]
    hq = hq.astype(BF16)
    q = _dot(hq, wq_ref[...]) * cqt_ref[...] + _dot(hq, wqs_ref[...]) * sqt_ref[...]
    q_ref[...] = q.astype(BF16)
    ckv = ckv_ref[...]
    hk = ckv * lax.rsqrt(jnp.mean(ckv * ckv, axis=-1, keepdims=True) + RMS_EPS) * gkv_ref[...]
    hk = hk.astype(BF16)
    krp = kr_ref[...] * ckt_ref[...]
    krp_hi = krp.astype(BF16)
    krp_lo = (krp - krp_hi.astype(F32)).astype(BF16)
    e = e_ref[...]
    k = _dot(hk, wk_ref[...]) + (_dot(krp_hi, e) + _dot(krp_lo, e))
    k_ref[...] = k.astype(BF16)
    v_ref[...] = _dot(hk, wv_ref[...]).astype(BF16)


def _mla_prep(proj, gq, gkv, wq, wqs, wk, wv, e, cqt, sqt, ckt, S):
    T = proj.shape[0]
    tm = 512
    sb = S // tm
    full = lambda shape: pl.BlockSpec(shape, lambda i: (0, 0))
    return pl.pallas_call(
        _mla_prep_kernel,
        out_shape=(jax.ShapeDtypeStruct((T, MLA_HEADS * MLA_HP), BF16),
                   jax.ShapeDtypeStruct((T, MLA_HEADS * MLA_HP), BF16),
                   jax.ShapeDtypeStruct((T, MLA_HEADS * MLA_V), BF16)),
        grid=(T // tm,),
        in_specs=[pl.BlockSpec((tm, MLA_Q_RANK), lambda i: (i, COL_CQ // MLA_Q_RANK)),
                  pl.BlockSpec((tm, MLA_KV_RANK), lambda i: (i, COL_CKV // MLA_KV_RANK)),
                  pl.BlockSpec((tm, LANES), lambda i: (i, COL_KR // LANES)),
                  full((1, MLA_Q_RANK)), full((1, MLA_KV_RANK)),
                  full((MLA_Q_RANK, MLA_HEADS * MLA_HP)), full((MLA_Q_RANK, MLA_HEADS * MLA_HP)),
                  full((MLA_KV_RANK, MLA_HEADS * MLA_HP)), full((MLA_KV_RANK, MLA_HEADS * MLA_V)),
                  full((LANES, MLA_HEADS * MLA_HP)),
                  pl.BlockSpec((tm, MLA_HEADS * MLA_HP), lambda i: (i % sb, 0)),
                  pl.BlockSpec((tm, MLA_HEADS * MLA_HP), lambda i: (i % sb, 0)),
                  pl.BlockSpec((tm, LANES), lambda i: (i % sb, 0))],
        out_specs=(pl.BlockSpec((tm, MLA_HEADS * MLA_HP), lambda i: (i, 0)),
                   pl.BlockSpec((tm, MLA_HEADS * MLA_HP), lambda i: (i, 0)),
                   pl.BlockSpec((tm, MLA_HEADS * MLA_V), lambda i: (i, 0))),
        compiler_params=_cparams(("parallel",)),
        name="mla_prep",
    )(proj, proj, proj, gq, gkv, wq, wqs, wk, wv, e, cqt, sqt, ckt)


def _mla_attn_kernel(q_ref, k_ref, v_ref, o_ref, acc_sc, m_sc, l_sc, *, tq):
    qi = pl.program_id(2)
    acc_sc[...] = jnp.zeros_like(acc_sc)
    m_sc[...] = jnp.full_like(m_sc, NEG_BIG)
    l_sc[...] = jnp.zeros_like(l_sc)

    def step(kj, masked):
        rows = pl.ds(pl.multiple_of(kj * tq, tq), tq)
        v = v_ref[rows, :]
        for hh in range(2):
            q = q_ref[:, hh * MLA_HP:(hh + 1) * MLA_HP]
            k = k_ref[rows, hh * MLA_HP:(hh + 1) * MLA_HP]
            s = _dot_nt(q, k)
            if masked:
                r = lax.broadcasted_iota(jnp.int32, s.shape, 0)
                c = lax.broadcasted_iota(jnp.int32, s.shape, 1)
                s = jnp.where(c <= r, s, NEG_BIG)
            m_prev = m_sc[hh]
            m_new = jnp.maximum(m_prev, jnp.max(s, axis=1, keepdims=True))
            alpha = jnp.exp(m_prev - m_new)
            p = jnp.exp(s - m_new[:, :1])
            l_sc[hh] = alpha * l_sc[hh] + jnp.sum(p, axis=1, keepdims=True)
            acc_sc[hh] = acc_sc[hh] * alpha + _dot(p.astype(BF16), v)
            m_sc[hh] = m_new

    def body(kj, carry):
        step(kj, False)
        return carry

    lax.fori_loop(0, qi, body, 0)
    step(qi, True)
    o0 = acc_sc[0] / l_sc[0]
    o1 = acc_sc[1] / l_sc[1]
    lane = lax.broadcasted_iota(jnp.int32, o0.shape, 1)
    o_ref[...] = jnp.where(lane < MLA_V, o0, o1).astype(o_ref.dtype)


def _mla_attn(q, k, v, B, S):
    T = q.shape[0]
    tq = 512
    nq = S // tq
    pairs = MLA_HEADS // 2
    return pl.pallas_call(
        functools.partial(_mla_attn_kernel, tq=tq),
        out_shape=jax.ShapeDtypeStruct((T, MLA_HEADS * MLA_V), BF16),
        grid=(B, pairs, nq),
        in_specs=[pl.BlockSpec((tq, 2 * MLA_HP), lambda b, h, i: (b * nq + i, h)),
                  pl.BlockSpec((S, 2 * MLA_HP), lambda b, h, i: (b, h)),
                  pl.BlockSpec((S, 2 * MLA_V), lambda b, h, i: (b, h))],
        out_specs=pl.BlockSpec((tq, 2 * MLA_V), lambda b, h, i: (b * nq + i, h)),
        scratch_shapes=[pltpu.VMEM((2, tq, LANES), F32),
                        pltpu.VMEM((2, tq, LANES), F32),
                        pltpu.VMEM((2, tq, LANES), F32)],
        compiler_params=_cparams(("parallel", "parallel", "arbitrary")),
        name="mla_attn",
    )(q, k, v)


def _swa_kernel(sink_ref, q_ref, kc_ref, kp_ref, vc_ref, vp_ref, o_ref):
    n = pl.program_id(1)
    hd = SWA_HEAD_DIM
    qi = lax.broadcasted_iota(jnp.int32, (WINDOW, 2 * WINDOW), 0)
    kj = lax.broadcasted_iota(jnp.int32, (WINDOW, 2 * WINDOW), 1)
    delta = qi + WINDOW - kj
    first_key = jnp.where(n > 0, 0, WINDOW)
    valid = (delta >= 0) & (delta < WINDOW) & (kj >= first_key)
    deltaf = delta.astype(F32)
    for g in range(SWA_KV_HEADS):
        kcat = jnp.concatenate([kp_ref[:, g * hd:(g + 1) * hd], kc_ref[:, g * hd:(g + 1) * hd]], axis=0)
        vcat = jnp.concatenate([vp_ref[:, g * hd:(g + 1) * hd], vc_ref[:, g * hd:(g + 1) * hd]], axis=0)
        kcat = kcat.astype(BF16)
        vcat = vcat.astype(BF16)
        for r in range(SWA_GROUP):
            h = g * SWA_GROUP + r
            slope = 2.0 ** (-8.0 * (h + 1) / SWA_HEADS)
            q = q_ref[:, h * hd:(h + 1) * hd].astype(BF16)
            s = _dot_nt(q, kcat) * (hd ** -0.5) - slope * deltaf
            s = jnp.where(valid, s, NEG_BIG)
            sink = sink_ref[h]
            m = jnp.maximum(jnp.max(s, axis=1, keepdims=True), sink)
            p = jnp.exp(s - m)
            denom = jnp.sum(p, axis=1, keepdims=True) + jnp.exp(sink - m)
            o = _dot(p.astype(BF16), vcat) / denom
            o_ref[:, h * hd:(h + 1) * hd] = o.astype(o_ref.dtype)


def _swa(proj, sinks, B, S):
    T = proj.shape[0]
    nb = S // WINDOW
    cq = COL_SWAQ // (SWA_HEADS * SWA_HEAD_DIM)
    ck = COL_SWAK // LANES
    cv = COL_SWAV // LANES
    cur = lambda c: (lambda b, n: (b * nb + n, c))
    prev = lambda c: (lambda b, n: (b * nb + jnp.maximum(n - 1, 0), c))
    return pl.pallas_call(
        _swa_kernel,
        out_shape=jax.ShapeDtypeStruct((T, SWA_HEADS * SWA_HEAD_DIM), BF16),
        grid=(B, nb),
        in_specs=[pl.BlockSpec(memory_space=pltpu.SMEM),
                  pl.BlockSpec((WINDOW, SWA_HEADS * SWA_HEAD_DIM), cur(cq)),
                  pl.BlockSpec((WINDOW, LANES), cur(ck)),
                  pl.BlockSpec((WINDOW, LANES), prev(ck)),
                  pl.BlockSpec((WINDOW, LANES), cur(cv)),
                  pl.BlockSpec((WINDOW, LANES), prev(cv))],
        out_specs=pl.BlockSpec((WINDOW, SWA_HEADS * SWA_HEAD_DIM), lambda b, n: (b * nb + n, 0)),
        compiler_params=_cparams(("parallel", "arbitrary")),
        name="swa_attn",
    )(sinks, proj, proj, proj, proj, proj)


def _gdn_conv_kernel(x_ref, w_ref, o_ref, *, S, rows):
    c = pl.program_id(1)
    w = w_ref[...]
    is_qk = c < 2 * GDN_HEADS
    post = jnp.where(c < GDN_HEADS, GDN_DK ** -0.5, 1.0).astype(F32)
    sub = lax.broadcasted_iota(jnp.int32, (8, LANES), 0)

    def body(i, carry):
        r0 = pl.multiple_of(i * rows, rows)
        cur = x_ref[pl.ds(r0, rows), :]
        p0 = pl.multiple_of(jnp.maximum(r0 - 8, 0), 8)
        prev8 = x_ref[pl.ds(p0, 8), :]
        prev8 = jnp.where(i > 0, prev8, 0.0)
        acc = cur * w[CONV_WIDTH - 1:CONV_WIDTH, :]
        for d in range(1, CONV_WIDTH):
            rolled = pltpu.roll(cur, d, axis=0)
            head = jnp.where(sub < d, pltpu.roll(prev8, d, axis=0), rolled[:8])
            shifted = jnp.concatenate([head, rolled[8:]], axis=0)
            acc = acc + shifted * w[CONV_WIDTH - 1 - d:CONV_WIDTH - d, :]
        y = _silu(acc)
        nrm = lax.rsqrt(jnp.sum(y * y, axis=-1, keepdims=True) + RMS_EPS) * post
        y = y * jnp.where(is_qk, nrm, 1.0)
        o_ref[pl.ds(r0, rows), :] = y
        return carry

    lax.fori_loop(0, S // rows, body, 0)


def _gdn_conv(proj, conv_w, B, S):
    T = proj.shape[0]
    nc = GDN_QKV // LANES
    c0 = COL_GQ // LANES
    return pl.pallas_call(
        functools.partial(_gdn_conv_kernel, S=S, rows=256),
        out_shape=jax.ShapeDtypeStruct((T, GDN_QKV), F32),
        grid=(B, nc),
        in_specs=[pl.BlockSpec((S, LANES), lambda b, c: (b, c0 + c)),
                  pl.BlockSpec((CONV_WIDTH, LANES), lambda b, c: (0, c))],
        out_specs=pl.BlockSpec((S, LANES), lambda b, c: (b, c)),
        compiler_params=_cparams(("parallel", "arbitrary")),
        name="gdn_conv",
    )(proj, conv_w)


def _gdn_gate_kernel(x_ref, alog_ref, dtb_ref, o_ref, *, S, rows):
    lane = lax.broadcasted_iota(jnp.int32, (rows, LANES), 1)
    row = lax.broadcasted_iota(jnp.int32, (rows, LANES), 0)
    pos = row % CHUNK
    neg_rate = -jnp.exp(alog_ref[...])
    dtb = dtb_ref[...]

    def body(i, carry):
        r0 = pl.multiple_of(i * rows, rows)
        x = x_ref[pl.ds(r0, rows), :]
        t = x + dtb
        softplus = jnp.maximum(t, 0.0) + jnp.log1p(jnp.exp(-jnp.abs(t)))
        g = neg_rate * softplus
        d = 1
        while d < CHUNK:
            g = g + jnp.where(pos >= d, pltpu.roll(g, d, axis=0), 0.0)
            d *= 2
        o_ref[pl.ds(r0, rows), :] = jnp.where(lane < GDN_HEADS, _sigmoid(x), g)
        return carry

    lax.fori_loop(0, S // rows, body, 0)


def _gdn_gates(proj, alog128, dtb128, B, S):
    T = proj.shape[0]
    return pl.pallas_call(
        functools.partial(_gdn_gate_kernel, S=S, rows=256),
        out_shape=jax.ShapeDtypeStruct((T, LANES), F32),
        grid=(B,),
        in_specs=[pl.BlockSpec((S, LANES), lambda b: (b, COL_BA // LANES)),
                  pl.BlockSpec((1, LANES), lambda b: (0, 0)),
                  pl.BlockSpec((1, LANES), lambda b: (0, 0))],
        out_specs=pl.BlockSpec((S, LANES), lambda b: (b, 0)),
        compiler_params=_cparams(("parallel",)),
        name="gdn_gates",
    )(proj, alog128, dtb128)


def _gdn_chunk_kernel(q_ref, k_ref, v_ref, gt_ref, z_ref, ng_ref, o_ref, state_sc, *, chunks):
    @pl.when(pl.program_id(1) == 0)
    def _():
        state_sc[...] = jnp.zeros_like(state_sc)

    C = CHUNK
    ri = lax.broadcasted_iota(jnp.int32, (C, C), 0)
    ci = lax.broadcasted_iota(jnp.int32, (C, C), 1)
    incl = ri >= ci
    strict = ri > ci
    ng = ng_ref[...]

    def body(c, carry):
        rows = pl.ds(pl.multiple_of(c * C, C), C)
        gt = gt_ref[rows, :]
        gtt = jnp.concatenate([gt, jnp.zeros_like(gt)], axis=0).T
        for h in range(GDN_HEADS):
            cols = slice(h * GDN_DK, (h + 1) * GDN_DK)
            q = q_ref[rows, cols]
            k = k_ref[rows, cols]
            v = v_ref[rows, cols]
            beta = gt[:, h:h + 1]
            gc = gt[:, GDN_HEADS + h:GDN_HEADS + h + 1]
            gr = gtt[GDN_HEADS + h:GDN_HEADS + h + 1, :C]
            glast = gc[C - 1:C, :]
            decay = jnp.exp(jnp.where(incl, gc - gr, NEG_BIG))
            eg = jnp.exp(gc)
            kb = k * beta
            kbf = k.astype(BF16)
            a1 = _dot_nt(jnp.concatenate([kb, q], axis=0).astype(BF16), kbf)
            lmat = a1[:C] * jnp.where(strict, decay, 0.0)
            a_qk = a1[C:] * decay
            rinv = -lmat
            mpow = lmat
            for _ in range(5):
                mb = mpow.astype(BF16)
                mpow = _dot(mb, mb)
                rinv = rinv + mpow + _dot(rinv.astype(BF16), mpow.astype(BF16))
            rhs = jnp.concatenate([v * beta, kb * eg], axis=1)
            sol = rhs + _dot(rinv.astype(BF16), rhs.astype(BF16))
            u = sol[:, :GDN_DV]
            w = sol[:, GDN_DV:]
            st = state_sc[h]
            a2 = _dot(jnp.concatenate([w, q * eg], axis=0).astype(BF16), st.astype(BF16))
            v_new = u - a2[:C]
            v_new_b = v_new.astype(BF16)
            o = a2[C:] + _dot(a_qk.astype(BF16), v_new_b)
            k_dec = k * jnp.exp(glast - gc)
            state_sc[h] = st * jnp.exp(glast) + _dot(k_dec.T.astype(BF16), v_new_b)
            on = o * lax.rsqrt(jnp.mean(o * o, axis=-1, keepdims=True) + RMS_EPS) * ng
            o_ref[rows, cols] = (on * _silu(z_ref[rows, cols])).astype(o_ref.dtype)
        return carry

    lax.fori_loop(0, chunks, body, 0)


def _gdn_chunks(qkv, gates, proj, norm_g, B, S):
    T = qkv.shape[0]
    chunks = 8
    R = chunks * CHUNK
    nr = S // R
    W = GDN_HEADS * GDN_DK
    row = lambda c: (lambda b, r: (b * nr + r, c))
    return pl.pallas_call(
        functools.partial(_gdn_chunk_kernel, chunks=chunks),
        out_shape=jax.ShapeDtypeStruct((T, GDN_HEADS * GDN_DV), BF16),
        grid=(B, nr),
        in_specs=[pl.BlockSpec((R, W), row(0)),
                  pl.BlockSpec((R, W), row(1)),
                  pl.BlockSpec((R, W), row(2)),
                  pl.BlockSpec((R, LANES), row(0)),
                  pl.BlockSpec((R, W), row(COL_Z // W)),
                  pl.BlockSpec((1, GDN_DV), lambda b, r: (0, 0))],
        out_specs=pl.BlockSpec((R, W), row(0)),
        scratch_shapes=[pltpu.VMEM((GDN_HEADS, GDN_DK, GDN_DV), F32)],
        compiler_params=_cparams(("parallel", "arbitrary")),
        name="gdn_chunks",
    )(qkv, qkv, qkv, gates, proj, norm_g)


def _out_proj_kernel(a_ref, s_ref, g_ref, x_ref, wa_ref, ws_ref, wg_ref, lg_ref, lb_ref, of_ref, ob_ref):
    y = _dot(a_ref[...], wa_ref[...]) + _dot(s_ref[...], ws_ref[...]) + _dot(g_ref[...], wg_ref[...])
    y = _layer_norm_rows(DN_ALPHA * x_ref[...] + y, lg_ref[...], lb_ref[...])
    of_ref[...] = y
    ob_ref[...] = y.astype(BF16)


def _out_proj_ln(mla, swa, gdn, x, wa, ws, wg, ln_g, ln_b):
    T = x.shape[0]
    tm = 256
    row = lambda w: pl.BlockSpec((tm, w), lambda i: (i, 0))
    full = lambda shape: pl.BlockSpec(shape, lambda i: (0, 0))
    return pl.pallas_call(
        _out_proj_kernel,
        out_shape=(jax.ShapeDtypeStruct((T, D_MODEL), F32), jax.ShapeDtypeStruct((T, D_MODEL), BF16)),
        grid=(T // tm,),
        in_specs=[row(mla.shape[1]), row(swa.shape[1]), row(gdn.shape[1]), row(D_MODEL),
                  full(wa.shape), full(ws.shape), full(wg.shape), full((1, D_MODEL)), full((1, D_MODEL))],
        out_specs=(row(D_MODEL), row(D_MODEL)),
        compiler_params=_cparams(("parallel",)),
        name="out_proj_ln",
    )(mla, swa, gdn, x, wa, ws, wg, ln_g, ln_b)


def _ffn_kernel(xb_ref, xf_ref, wg_ref, wu_ref, wd_ref, lg_ref, lb_ref, of_ref, ob_ref, acc_sc):
    j = pl.program_id(1)
    xb = xb_ref[...]
    h = _silu(_dot(xb, wg_ref[...])) * _dot(xb, wu_ref[...])
    part = _dot(h.astype(BF16), wd_ref[...])

    @pl.when(j == 0)
    def _():
        acc_sc[...] = part

    @pl.when(j > 0)
    def _():
        acc_sc[...] += part

    @pl.when(j == pl.num_programs(1) - 1)
    def _():
        y = _layer_norm_rows(DN_ALPHA * xf_ref[...] + acc_sc[...], lg_ref[...], lb_ref[...])
        of_ref[...] = y
        ob_ref[...] = y.astype(BF16)


def _ffn_ln(xb, xf, wg, wu, wd, ln_g, ln_b):
    T = xb.shape[0]
    tm, tf = 512, 512
    return pl.pallas_call(
        _ffn_kernel,
        out_shape=(jax.ShapeDtypeStruct((T, D_MODEL), F32), jax.ShapeDtypeStruct((T, D_MODEL), BF16)),
        grid=(T // tm, D_FF // tf),
        in_specs=[pl.BlockSpec((tm, D_MODEL), lambda i, j: (i, 0)),
                  pl.BlockSpec((tm, D_MODEL), lambda i, j: (i, 0)),
                  pl.BlockSpec((D_MODEL, tf), lambda i, j: (0, j)),
                  pl.BlockSpec((D_MODEL, tf), lambda i, j: (0, j)),
                  pl.BlockSpec((tf, D_MODEL), lambda i, j: (j, 0)),
                  pl.BlockSpec((1, D_MODEL), lambda i, j: (0, 0)),
                  pl.BlockSpec((1, D_MODEL), lambda i, j: (0, 0))],
        out_specs=(pl.BlockSpec((tm, D_MODEL), lambda i, j: (i, 0)),
                   pl.BlockSpec((tm, D_MODEL), lambda i, j: (i, 0))),
        scratch_shapes=[pltpu.VMEM((tm, D_MODEL), F32)],
        compiler_params=_cparams(("parallel", "arbitrary")),
        name="ffn_ln",
    )(xb, xf, wg, wu, wd, ln_g, ln_b)


def _router_kernel(x_ref, w_ref, o_ref):
    logits = jnp.dot(x_ref[...], w_ref[...], preferred_element_type=F32, precision=lax.Precision.HIGHEST)
    lane = lax.broadcasted_iota(jnp.int32, logits.shape, 1).astype(F32)
    logits = jnp.where(lane < N_EXPERTS, logits, -jnp.inf)
    m1 = jnp.max(logits, axis=1, keepdims=True)
    i1 = jnp.min(jnp.where(logits == m1, lane, float(LANES)), axis=1, keepdims=True)
    rest = jnp.where(lane == i1, -jnp.inf, logits)
    m2 = jnp.max(rest, axis=1, keepdims=True)
    i2 = jnp.min(jnp.where(rest == m2, lane, float(LANES)), axis=1, keepdims=True)
    e = jnp.exp(m2 - m1)
    g1 = 1.0 / (1.0 + e)
    g2 = e / (1.0 + e)
    out = jnp.where(lane == 0, i1,
                    jnp.where(lane == 1, i2,
                              jnp.where(lane == 2, g1, jnp.where(lane == 3, g2, 0.0))))
    o_ref[...] = out


def _router(xf, w128):
    T = xf.shape[0]
    tm = 512
    return pl.pallas_call(
        _router_kernel,
        out_shape=jax.ShapeDtypeStruct((T, LANES), F32),
        grid=(T // tm,),
        in_specs=[pl.BlockSpec((tm, D_MODEL), lambda i: (i, 0)),
                  pl.BlockSpec((D_MODEL, LANES), lambda i: (0, 0))],
        out_specs=pl.BlockSpec((tm, LANES), lambda i: (i, 0)),
        compiler_params=_cparams(("parallel",)),
        name="moe_router",
    )(xf, w128)


MOE_TM = 512
MOE_TF = 512


def _moe_kernel(te_ref, nu_ref, nv_ref, tok_ref, dst_ref, x_hbm, wg_ref, wu_ref, wd_ref, o_hbm,
                xg_sc, xb_sc, acc_sc, gsem, ssem):
    i = pl.program_id(0)
    j = pl.program_id(1)
    nj = pl.num_programs(1)
    base = i * MOE_TM
    active = i < nu_ref[0]
    n_valid = nv_ref[i]

    @pl.when(active & (j == 0))
    def _():
        def issue(r, carry):
            pltpu.make_async_copy(x_hbm.at[tok_ref[base + r]], xg_sc.at[r], gsem).start()
            return carry
        lax.fori_loop(0, MOE_TM, issue, 0)

        def drain(r, carry):
            pltpu.make_async_copy(x_hbm.at[0], xg_sc.at[r], gsem).wait()
            return carry
        lax.fori_loop(0, MOE_TM, drain, 0)
        xb_sc[...] = xg_sc[...].astype(BF16)

    @pl.when(active)
    def _():
        xb = xb_sc[...]
        h = _silu(_dot(xb, wg_ref[...])) * _dot(xb, wu_ref[...])
        part = _dot(h.astype(BF16), wd_ref[...])

        @pl.when(j == 0)
        def _():
            acc_sc[...] = part

        @pl.when(j > 0)
        def _():
            acc_sc[...] += part

    @pl.when(active & (j == nj - 1))
    def _():
        def issue(r, carry):
            pltpu.make_async_copy(acc_sc.at[r], o_hbm.at[dst_ref[base + r]], ssem).start()
            return carry
        lax.fori_loop(0, n_valid, issue, 0)

        def drain(r, carry):
            pltpu.make_async_copy(acc_sc.at[r], o_hbm.at[0], ssem).wait()
            return carry
        lax.fori_loop(0, n_valid, drain, 0)


def _moe_grouped(tile_expert, n_used, tile_valid, row_token, row_dest, xf, wg, wu, wd, out_rows):
    n_tiles = tile_expert.shape[0]
    nj = D_FF // MOE_TF

    def wcol(i, j, te, nu, nv, tok, dst):
        jj = jnp.where(i < nu[0], j, nj - 1)
        return (te[i], 0, jj)

    def wrow(i, j, te, nu, nv, tok, dst):
        jj = jnp.where(i < nu[0], j, nj - 1)
        return (te[i], jj, 0)

    grid_spec = pltpu.PrefetchScalarGridSpec(
        num_scalar_prefetch=5,
        grid=(n_tiles, nj),
        in_specs=[pl.BlockSpec(memory_space=pl.ANY),
                  pl.BlockSpec((None, D_MODEL, MOE_TF), wcol),
                  pl.BlockSpec((None, D_MODEL, MOE_TF), wcol),
                  pl.BlockSpec((None, MOE_TF, D_MODEL), wrow)],
        out_specs=pl.BlockSpec(memory_space=pl.ANY),
        scratch_shapes=[pltpu.VMEM((MOE_TM, D_MODEL), F32),
                        pltpu.VMEM((MOE_TM, D_MODEL), BF16),
                        pltpu.VMEM((MOE_TM, D_MODEL), F32),
                        pltpu.SemaphoreType.DMA,
                        pltpu.SemaphoreType.DMA],
    )
    return pl.pallas_call(
        _moe_kernel,
        out_shape=jax.ShapeDtypeStruct((out_rows, D_MODEL), F32),
        grid_spec=grid_spec,
        compiler_params=_cparams(("arbitrary", "arbitrary")),
        name="moe_grouped",
    )(tile_expert, n_used, tile_valid, row_token, row_dest, xf, wg, wu, wd)


def _moe_combine_kernel(o0_ref, o1_ref, rt_ref, x_ref, lg_ref, lb_ref, of_ref, ob_ref):
    rt = rt_ref[...]
    f = rt[:, 2:3] * o0_ref[...] + rt[:, 3:4] * o1_ref[...]
    y = _layer_norm_rows(DN_ALPHA * x_ref[...] + f, lg_ref[...], lb_ref[...])
    of_ref[...] = y
    ob_ref[...] = y.astype(BF16)


def _moe_combine_ln(o_rows, route, xf, ln_g, ln_b):
    T = xf.shape[0]
    tm = 512
    nt = T // tm
    return pl.pallas_call(
        _moe_combine_kernel,
        out_shape=(jax.ShapeDtypeStruct((T, D_MODEL), F32), jax.ShapeDtypeStruct((T, D_MODEL), BF16)),
        grid=(nt,),
        in_specs=[pl.BlockSpec((tm, D_MODEL), lambda i: (i, 0)),
                  pl.BlockSpec((tm, D_MODEL), lambda i: (i + nt, 0)),
                  pl.BlockSpec((tm, LANES), lambda i: (i, 0)),
                  pl.BlockSpec((tm, D_MODEL), lambda i: (i, 0)),
                  pl.BlockSpec((1, D_MODEL), lambda i: (0, 0)),
                  pl.BlockSpec((1, D_MODEL), lambda i: (0, 0))],
        out_specs=(pl.BlockSpec((tm, D_MODEL), lambda i: (i, 0)),
                   pl.BlockSpec((tm, D_MODEL), lambda i: (i, 0))),
        compiler_params=_cparams(("parallel",)),
        name="moe_combine_ln",
    )(o_rows, o_rows, route, xf, ln_g, ln_b)


def _moe_plan(route, T):
    ids = route[:, :TOP_K].astype(jnp.int32)
    e_flat = ids.reshape(-1)
    onehot = (e_flat[:, None] == jnp.arange(N_EXPERTS, dtype=jnp.int32)[None, :]).astype(jnp.int32)
    csum = jnp.cumsum(onehot, axis=0)
    rank = jnp.sum(csum * onehot, axis=1) - 1
    counts = csum[-1]
    tiles_per = (counts + MOE_TM - 1) // MOE_TM
    tile_end = jnp.cumsum(tiles_per)
    tile_start = tile_end - tiles_per
    n_used = tile_end[-1]
    n_tiles = (TOP_K * T) // MOE_TM + N_EXPERTS
    pos = tile_start[e_flat] * MOE_TM + rank
    pair = jnp.arange(TOP_K * T, dtype=jnp.int32)
    token = pair // TOP_K
    slot = pair % TOP_K
    n_rows = n_tiles * MOE_TM
    row_token = jnp.zeros((n_rows,), jnp.int32).at[pos].set(token)
    row_dest = jnp.zeros((n_rows,), jnp.int32).at[pos].set(slot * T + token)
    tiles = jnp.arange(n_tiles, dtype=jnp.int32)
    tidx = jnp.minimum(tiles, n_used - 1)
    tile_expert = jnp.sum((tidx[:, None] >= tile_end[None, :]).astype(jnp.int32), axis=1)
    tile_expert = jnp.minimum(tile_expert, N_EXPERTS - 1).astype(jnp.int32)
    tile_valid = jnp.clip(counts[tile_expert] - (tiles - tile_start[tile_expert]) * MOE_TM, 0, MOE_TM)
    tile_valid = jnp.where(tiles < n_used, tile_valid, 0).astype(jnp.int32)
    return tile_expert, n_used.reshape(1).astype(jnp.int32), tile_valid, row_token, row_dest


def _prep_w_in(w_in):
    L = w_in.shape[0]
    z = lambda n: jnp.zeros((L, D_MODEL, n), w_in.dtype)
    o_swa = MLA_IN
    o_gdn = MLA_IN + SWA_IN
    hd = SWA_HEAD_DIM
    kr = w_in[:, :, MLA_Q_RANK + MLA_KV_RANK:MLA_IN]
    half = MLA_ROPE // 2
    kr_sw = jnp.concatenate([kr[:, :, half:], kr[:, :, :half]], axis=-1)
    ba = w_in[:, :, o_gdn + GDN_QKV:o_gdn + GDN_QKV + 2 * GDN_HEADS]
    segs = [
        w_in[:, :, :MLA_Q_RANK],
        w_in[:, :, o_swa:o_swa + SWA_HEADS * hd],
        w_in[:, :, o_gdn + GDN_QKV + 2 * GDN_HEADS:],
        w_in[:, :, o_gdn:o_gdn + GDN_QKV],
        w_in[:, :, MLA_Q_RANK:MLA_Q_RANK + MLA_KV_RANK],
        w_in[:, :, o_swa + SWA_HEADS * hd:o_swa + (SWA_HEADS + SWA_KV_HEADS) * hd],
        w_in[:, :, o_swa + (SWA_HEADS + SWA_KV_HEADS) * hd:o_swa + SWA_IN],
        kr, kr_sw, z(LANES - 2 * MLA_ROPE),
        ba, z(LANES - 2 * GDN_HEADS),
        z(D_PROJ - COL_BA - LANES),
    ]
    return jnp.concatenate(segs, axis=-1).astype(BF16)


def _prep_mla_weights(w_uq, w_ukv):
    L = w_uq.shape[0]
    half = MLA_ROPE // 2
    wq = w_uq.reshape(L, MLA_Q_RANK, MLA_HEADS, MLA_NOPE + MLA_ROPE)
    nope, rope = wq[..., :MLA_NOPE], wq[..., MLA_NOPE:]
    rope_sw = jnp.concatenate([rope[..., half:], rope[..., :half]], axis=-1)
    zpad = jnp.zeros(wq.shape[:3] + (MLA_HP - MLA_NOPE - MLA_ROPE,), wq.dtype)
    wq_pad = jnp.concatenate([nope, rope, zpad], axis=-1).reshape(L, MLA_Q_RANK, MLA_HEADS * MLA_HP)
    wq_sw = jnp.concatenate([jnp.zeros_like(nope), rope_sw, zpad], axis=-1).reshape(L, MLA_Q_RANK, MLA_HEADS * MLA_HP)
    wkv = w_ukv.reshape(L, MLA_KV_RANK, MLA_HEADS, MLA_NOPE + MLA_V)
    knope, vv = wkv[..., :MLA_NOPE], wkv[..., MLA_NOPE:]
    wk_pad = jnp.concatenate([knope, jnp.zeros(knope.shape[:3] + (MLA_HP - MLA_NOPE,), wkv.dtype)], axis=-1)
    wk_pad = wk_pad.reshape(L, MLA_KV_RANK, MLA_HEADS * MLA_HP)
    wv = vv.reshape(L, MLA_KV_RANK, MLA_HEADS * MLA_V)
    return wq_pad.astype(BF16), wq_sw.astype(BF16), wk_pad.astype(BF16), wv.astype(BF16)


def _rope_tables(S):
    half = MLA_ROPE // 2
    pos = jnp.arange(S, dtype=jnp.int32)
    inv_freq = ROPE_BASE ** (-jnp.arange(half, dtype=F32) / half)
    ang = pos.astype(F32)[:, None] * inv_freq[None, :]
    cos, sin = jnp.cos(ang), jnp.sin(ang)
    cos2 = jnp.concatenate([cos, cos], axis=-1)
    sin2 = jnp.concatenate([-sin, sin], axis=-1)
    scale = (MLA_NOPE + MLA_ROPE) ** -0.5
    tail = jnp.zeros((S, MLA_HP - MLA_NOPE - MLA_ROPE), F32)
    cq_head = jnp.concatenate([jnp.ones((S, MLA_NOPE), F32), cos2, tail], axis=-1) * scale
    sq_head = jnp.concatenate([jnp.zeros((S, MLA_NOPE), F32), sin2, tail], axis=-1) * scale
    cqt = jnp.tile(cq_head, (1, MLA_HEADS))
    sqt = jnp.tile(sq_head, (1, MLA_HEADS))
    ckt = jnp.concatenate([cos2, sin2, jnp.zeros((S, LANES - 2 * MLA_ROPE), F32)], axis=-1)
    j = jnp.arange(LANES)[:, None]
    col = jnp.arange(MLA_HEADS * MLA_HP)[None, :]
    place = ((col % MLA_HP) == (MLA_NOPE + (j % MLA_ROPE))) & (j < 2 * MLA_ROPE)
    return cqt, sqt, ckt, place.astype(BF16)


def _pad_lanes(v, offset):
    out = jnp.zeros((1, LANES), F32)
    return lax.dynamic_update_slice(out, v.astype(F32).reshape(1, -1), (0, offset))


def kernel(x, w_in, mla_q_norm, mla_w_uq, mla_kv_norm, mla_w_ukv, swa_sinks, gdn_conv,
           gdn_a_log, gdn_dt_bias, gdn_norm, w_out, ln1_g, ln1_b, ln2_g, ln2_b,
           ffn_w_gate, ffn_w_up, ffn_w_down, moe_router, moe_w_gate, moe_w_up, moe_w_down):
    B, S, D = x.shape
    T = B * S
    xf = x.reshape(T, D)
    xb = xf.astype(BF16)

    w_in_b = _prep_w_in(w_in)
    wq_pad, wq_sw, wk_pad, wv = _prep_mla_weights(mla_w_uq, mla_w_ukv)
    cqt, sqt, ckt, place = _rope_tables(S)
    w_out_b = w_out.astype(BF16)
    n_mla = MLA_HEADS * MLA_V
    n_swa = SWA_HEADS * SWA_HEAD_DIM
    ffn_g, ffn_u, ffn_d = ffn_w_gate.astype(BF16), ffn_w_up.astype(BF16), ffn_w_down.astype(BF16)
    moe_g, moe_u, moe_d = moe_w_gate.astype(BF16), moe_w_up.astype(BF16), moe_w_down.astype(BF16)
    router128 = jnp.pad(moe_router.astype(F32), ((0, 0), (0, 0), (0, LANES - N_EXPERTS)))

    for l in range(DEPTH):
        proj = _in_proj(xb, w_in_b[l])
        q, k, v = _mla_prep(proj, mla_q_norm[l].reshape(1, -1), mla_kv_norm[l].reshape(1, -1),
                            wq_pad[l], wq_sw[l], wk_pad[l], wv[l], place, cqt, sqt, ckt, S)
        h_mla = _mla_attn(q, k, v, B, S)
        h_swa = _swa(proj, swa_sinks[l].astype(F32), B, S)
        qkv = _gdn_conv(proj, gdn_conv[l], B, S)
        gates = _gdn_gates(proj, _pad_lanes(gdn_a_log[l], GDN_HEADS), _pad_lanes(gdn_dt_bias[l], GDN_HEADS), B, S)
        h_gdn = _gdn_chunks(qkv, gates, proj, gdn_norm[l].reshape(1, -1), B, S)
        wo = w_out_b[l]
        xf, xb = _out_proj_ln(h_mla, h_swa, h_gdn, xf,
                              wo[:n_mla], wo[n_mla:n_mla + n_swa], wo[n_mla + n_swa:],
                              ln1_g[l].reshape(1, -1), ln1_b[l].reshape(1, -1))
        lg, lb = ln2_g[l].reshape(1, -1), ln2_b[l].reshape(1, -1)
        if l % 2 == 0:
            xf, xb = _ffn_ln(xb, xf, ffn_g[l // 2], ffn_u[l // 2], ffn_d[l // 2], lg, lb)
        else:
            m = l // 2
            route = _router(xf, router128[m])
            tile_expert, n_used, tile_valid, row_token, row_dest = _moe_plan(route, T)
            o_rows = _moe_grouped(tile_expert, n_used, tile_valid, row_token, row_dest, xf,
                                  moe_g[m], moe_u[m], moe_d[m], TOP_K * T)
            xf, xb = _moe_combine_ln(o_rows, route, xf, lg, lb)
    return xf.reshape(B, S, D)
```

```python
import functools
import math

import jax
import jax.numpy as jnp
from jax import lax
from jax.experimental import pallas as pl
from jax.experimental.pallas import tpu as pltpu

F32 = jnp.float32
BF16 = jnp.bfloat16

D_MODEL = 2048
DEPTH = 4
MLA_HEADS = 8
MLA_Q_RANK = 512
MLA_KV_RANK = 256
MLA_NOPE = 64
MLA_ROPE = 32
MLA_V = 64
ROPE_BASE = 10000.0
SWA_HEADS = 8
SWA_KV_HEADS = 2
SWA_GROUP = SWA_HEADS // SWA_KV_HEADS
SWA_HEAD_DIM = 64
WINDOW = 128
GDN_HEADS = 8
GDN_DK = 128
GDN_DV = 128
CONV_WIDTH = 4
GDN_C = 256
D_FF = 7168
N_EXPERTS = 8
TOP_K = 2
DN_ALPHA = float((2 * DEPTH) ** 0.25)
LN_EPS = 1e-5
RMS_EPS = 1e-6

MLA_IN = MLA_Q_RANK + MLA_KV_RANK + MLA_ROPE
SWA_IN = (SWA_HEADS + 2 * SWA_KV_HEADS) * SWA_HEAD_DIM
GDN_QKV = GDN_HEADS * (2 * GDN_DK + GDN_DV)

LANES = 128
NEG_BIG = -1e30
VMEM_LIMIT = 56 * 1024 * 1024

COL_CQ = 0
COL_SWAQ = 512
COL_Z = 1024
COL_GQ = 2048
COL_GK = 3072
COL_GV = 4096
COL_CKV = 5120
COL_SWAK = 5376
COL_SWAV = 5504
COL_KR = 5632
COL_BA = 5760
D_PROJ = 6144

MLA_HP = 128


def _cparams(sem, vmem=VMEM_LIMIT):
    return pltpu.CompilerParams(dimension_semantics=sem, vmem_limit_bytes=vmem)


def _layer_norm_rows(y, g, b):
    mu = jnp.mean(y, axis=-1, keepdims=True)
    d = y - mu
    var = jnp.mean(d * d, axis=-1, keepdims=True)
    return d * lax.rsqrt(var + LN_EPS) * g + b


def _sigmoid(x):
    return 1.0 / (1.0 + jnp.exp(-x))


def _silu(x):
    return x * _sigmoid(x)


def _dot(a, b):
    return jnp.dot(a, b, preferred_element_type=F32)


def _dot_nt(a, b):
    return lax.dot_general(a, b, (((1,), (1,)), ((), ())), preferred_element_type=F32)


def _matmul_kernel(x_ref, w_ref, o_ref):
    o_ref[...] = _dot(x_ref[...], w_ref[...])


def _in_proj(xb, w, l):
    T = xb.shape[0]
    tm, tn = 1024, 512
    return pl.pallas_call(
        _matmul_kernel,
        out_shape=jax.ShapeDtypeStruct((T, D_PROJ), F32),
        grid=(T // tm, D_PROJ // tn),
        in_specs=[pl.BlockSpec((tm, D_MODEL), lambda i, j: (i, 0)),
                  pl.BlockSpec((None, D_MODEL, tn), lambda i, j: (l, 0, j))],
        out_specs=pl.BlockSpec((tm, tn), lambda i, j: (i, j)),
        compiler_params=_cparams(("parallel", "arbitrary")),
        name="in_proj",
    )(xb, w)


def _mla_prep_kernel(cq_ref, ckv_ref, kr_ref, gq_ref, gkv_ref, wq_ref, wqs_ref, wk_ref, wv_ref,
                     e_ref, cqt_ref, sqt_ref, ckt_ref, q_ref, k_ref, v_ref):
    cq = cq_ref[...]
    hq = cq * lax.rsqrt(jnp.mean(cq * cq, axis=-1, keepdims=True) + RMS_EPS) * gq_ref[...]
    hq = hq.astype(BF16)
    q = _dot(hq, wq_ref[...]) * cqt_ref[...] + _dot(hq, wqs_ref[...]) * sqt_ref[...]
    q_ref[...] = q.astype(BF16)
    ckv = ckv_ref[...]
    hk = ckv * lax.rsqrt(jnp.mean(ckv * ckv, axis=-1, keepdims=True) + RMS_EPS) * gkv_ref[...]
    hk = hk.astype(BF16)
    krp = kr_ref[...] * ckt_ref[...]
    krp_hi = krp.astype(BF16)
    krp_lo = (krp - krp_hi.astype(F32)).astype(BF16)
    e = e_ref[...]
    k = _dot(hk, wk_ref[...]) + (_dot(krp_hi, e) + _dot(krp_lo, e))
    k_ref[...] = k.astype(BF16)
    v_ref[...] = _dot(hk, wv_ref[...]).astype(BF16)


def _mla_prep(proj, gq, gkv, wq, wqs, wk, wv, l, e, cqt, sqt, ckt, S):
    T = proj.shape[0]
    tm = 512
    sb = S // tm
    full = lambda shape: pl.BlockSpec(shape, lambda i: (0, 0))
    layer = lambda shape: pl.BlockSpec((None,) + shape, lambda i: (l, 0, 0))
    return pl.pallas_call(
        _mla_prep_kernel,
        out_shape=(jax.ShapeDtypeStruct((T, MLA_HEADS * MLA_HP), BF16),
                   jax.ShapeDtypeStruct((T, MLA_HEADS * MLA_HP), BF16),
                   jax.ShapeDtypeStruct((T, MLA_HEADS * MLA_V), BF16)),
        grid=(T // tm,),
        in_specs=[pl.BlockSpec((tm, MLA_Q_RANK), lambda i: (i, COL_CQ // MLA_Q_RANK)),
                  pl.BlockSpec((tm, MLA_KV_RANK), lambda i: (i, COL_CKV // MLA_KV_RANK)),
                  pl.BlockSpec((tm, LANES), lambda i: (i, COL_KR // LANES)),
                  full((1, MLA_Q_RANK)), full((1, MLA_KV_RANK)),
                  layer((MLA_Q_RANK, MLA_HEADS * MLA_HP)), layer((MLA_Q_RANK, MLA_HEADS * MLA_HP)),
                  layer((MLA_KV_RANK, MLA_HEADS * MLA_HP)), layer((MLA_KV_RANK, MLA_HEADS * MLA_V)),
                  full((LANES, MLA_HEADS * MLA_HP)),
                  pl.BlockSpec((tm, MLA_HEADS * MLA_HP), lambda i: (i % sb, 0)),
                  pl.BlockSpec((tm, MLA_HEADS * MLA_HP), lambda i: (i % sb, 0)),
                  pl.BlockSpec((tm, LANES), lambda i: (i % sb, 0))],
        out_specs=(pl.BlockSpec((tm, MLA_HEADS * MLA_HP), lambda i: (i, 0)),
                   pl.BlockSpec((tm, MLA_HEADS * MLA_HP), lambda i: (i, 0)),
                   pl.BlockSpec((tm, MLA_HEADS * MLA_V), lambda i: (i, 0))),
        compiler_params=_cparams(("parallel",)),
        name="mla_prep",
    )(proj, proj, proj, gq, gkv, wq, wqs, wk, wv, e, cqt, sqt, ckt)


MLA_G = 2


def _mla_attn_kernel(q_ref, k_ref, v_ref, o_ref, acc_sc, m_sc, l_sc, *, tq):
    qi = pl.program_id(2)
    acc_sc[...] = jnp.zeros_like(acc_sc)
    m_sc[...] = jnp.full_like(m_sc, NEG_BIG)
    l_sc[...] = jnp.zeros_like(l_sc)

    def step(kj, masked):
        rows = pl.ds(pl.multiple_of(kj * tq, tq), tq)
        for g in range(MLA_G):
            s = _dot_nt(q_ref[:, g * MLA_HP:(g + 1) * MLA_HP], k_ref[rows, g * MLA_HP:(g + 1) * MLA_HP])
            if masked:
                r = lax.broadcasted_iota(jnp.int32, s.shape, 0)
                c = lax.broadcasted_iota(jnp.int32, s.shape, 1)
                s = jnp.where(c <= r, s, NEG_BIG)
            m_prev = m_sc[g]
            m_new = jnp.maximum(m_prev, jnp.max(s, axis=1, keepdims=True))
            alpha = jnp.exp(m_prev - m_new)
            p = jnp.exp(s - m_new[:, :1])
            l_sc[g] = alpha * l_sc[g] + jnp.sum(p, axis=1, keepdims=True)
            pv = _dot(p.astype(BF16), v_ref[rows, (g // 2) * LANES:(g // 2 + 1) * LANES])
            acc_sc[g] = acc_sc[g] * alpha + pv
            m_sc[g] = m_new

    def body(kj, carry):
        step(kj, False)
        return carry

    lax.fori_loop(0, qi, body, 0)
    step(qi, True)
    lane = lax.broadcasted_iota(jnp.int32, (tq, LANES), 1)
    for pr in range(MLA_G // 2):
        o0 = acc_sc[2 * pr] / l_sc[2 * pr]
        o1 = acc_sc[2 * pr + 1] / l_sc[2 * pr + 1]
        o_ref[:, pr * LANES:(pr + 1) * LANES] = jnp.where(lane < MLA_V, o0, o1).astype(o_ref.dtype)


def _mla_attn(q, k, v, B, S):
    T = q.shape[0]
    tq = 512
    nq = S // tq
    groups = MLA_HEADS // MLA_G
    return pl.pallas_call(
        functools.partial(_mla_attn_kernel, tq=tq),
        out_shape=jax.ShapeDtypeStruct((T, MLA_HEADS * MLA_V), BF16),
        grid=(B, groups, nq),
        in_specs=[pl.BlockSpec((tq, MLA_G * MLA_HP), lambda b, h, i: (b * nq + i, h)),
                  pl.BlockSpec((S, MLA_G * MLA_HP), lambda b, h, i: (b, h)),
                  pl.BlockSpec((S, MLA_G * MLA_V), lambda b, h, i: (b, h))],
        out_specs=pl.BlockSpec((tq, MLA_G * MLA_V), lambda b, h, i: (b * nq + i, h)),
        scratch_shapes=[pltpu.VMEM((MLA_G, tq, LANES), F32),
                        pltpu.VMEM((MLA_G, tq, LANES), F32),
                        pltpu.VMEM((MLA_G, tq, LANES), F32)],
        compiler_params=_cparams(("parallel", "parallel", "arbitrary")),
        name="mla_attn",
    )(q, k, v)


def _swa_kernel(sink_ref, q_ref, kc_ref, kp_ref, vc_ref, vp_ref, o_ref):
    n = pl.program_id(1)
    hd = SWA_HEAD_DIM
    qi = lax.broadcasted_iota(jnp.int32, (WINDOW, 2 * WINDOW), 0)
    kj = lax.broadcasted_iota(jnp.int32, (WINDOW, 2 * WINDOW), 1)
    delta = qi + WINDOW - kj
    first_key = jnp.where(n > 0, 0, WINDOW)
    valid = (delta >= 0) & (delta < WINDOW) & (kj >= first_key)
    deltaf = delta.astype(F32)
    for g in range(SWA_KV_HEADS):
        kcat = jnp.concatenate([kp_ref[:, g * hd:(g + 1) * hd], kc_ref[:, g * hd:(g + 1) * hd]], axis=0)
        vcat = jnp.concatenate([vp_ref[:, g * hd:(g + 1) * hd], vc_ref[:, g * hd:(g + 1) * hd]], axis=0)
        kcat = kcat.astype(BF16)
        vcat = vcat.astype(BF16)
        for r in range(SWA_GROUP):
            h = g * SWA_GROUP + r
            slope = 2.0 ** (-8.0 * (h + 1) / SWA_HEADS)
            q = q_ref[:, h * hd:(h + 1) * hd].astype(BF16)
            s = _dot_nt(q, kcat) * (hd ** -0.5) - slope * deltaf
            s = jnp.where(valid, s, NEG_BIG)
            sink = sink_ref[h]
            m = jnp.maximum(jnp.max(s, axis=1, keepdims=True), sink)
            p = jnp.exp(s - m)
            denom = jnp.sum(p, axis=1, keepdims=True) + jnp.exp(sink - m)
            o = _dot(p.astype(BF16), vcat) / denom
            o_ref[:, h * hd:(h + 1) * hd] = o.astype(o_ref.dtype)


def _swa(proj, sinks, B, S):
    T = proj.shape[0]
    nb = S // WINDOW
    cq = COL_SWAQ // (SWA_HEADS * SWA_HEAD_DIM)
    ck = COL_SWAK // LANES
    cv = COL_SWAV // LANES
    cur = lambda c: (lambda b, n: (b * nb + n, c))
    prev = lambda c: (lambda b, n: (b * nb + jnp.maximum(n - 1, 0), c))
    return pl.pallas_call(
        _swa_kernel,
        out_shape=jax.ShapeDtypeStruct((T, SWA_HEADS * SWA_HEAD_DIM), BF16),
        grid=(B, nb),
        in_specs=[pl.BlockSpec(memory_space=pltpu.SMEM),
                  pl.BlockSpec((WINDOW, SWA_HEADS * SWA_HEAD_DIM), cur(cq)),
                  pl.BlockSpec((WINDOW, LANES), cur(ck)),
                  pl.BlockSpec((WINDOW, LANES), prev(ck)),
                  pl.BlockSpec((WINDOW, LANES), cur(cv)),
                  pl.BlockSpec((WINDOW, LANES), prev(cv))],
        out_specs=pl.BlockSpec((WINDOW, SWA_HEADS * SWA_HEAD_DIM), lambda b, n: (b * nb + n, 0)),
        compiler_params=_cparams(("parallel", "arbitrary")),
        name="swa_attn",
    )(sinks, proj, proj, proj, proj, proj)


def _gdn_conv_kernel(x_ref, w_ref, o_ref, *, S, rows):
    c = pl.program_id(1)
    w = w_ref[...]
    is_qk = c < 2 * GDN_HEADS
    post = jnp.where(c < GDN_HEADS, GDN_DK ** -0.5, 1.0).astype(F32)
    sub = lax.broadcasted_iota(jnp.int32, (8, LANES), 0)

    def body(i, carry):
        r0 = pl.multiple_of(i * rows, rows)
        cur = x_ref[pl.ds(r0, rows), :]
        p0 = pl.multiple_of(jnp.maximum(r0 - 8, 0), 8)
        prev8 = x_ref[pl.ds(p0, 8), :]
        prev8 = jnp.where(i > 0, prev8, 0.0)
        acc = cur * w[CONV_WIDTH - 1:CONV_WIDTH, :]
        for d in range(1, CONV_WIDTH):
            rolled = pltpu.roll(cur, d, axis=0)
            head = jnp.where(sub < d, pltpu.roll(prev8, d, axis=0), rolled[:8])
            shifted = jnp.concatenate([head, rolled[8:]], axis=0)
            acc = acc + shifted * w[CONV_WIDTH - 1 - d:CONV_WIDTH - d, :]
        y = _silu(acc)
        nrm = lax.rsqrt(jnp.sum(y * y, axis=-1, keepdims=True) + RMS_EPS) * post
        y = y * jnp.where(is_qk, nrm, 1.0)
        o_ref[pl.ds(r0, rows), :] = y
        return carry

    lax.fori_loop(0, S // rows, body, 0)


def _gdn_conv(proj, conv_w, B, S):
    T = proj.shape[0]
    nc = GDN_QKV // LANES
    c0 = COL_GQ // LANES
    return pl.pallas_call(
        functools.partial(_gdn_conv_kernel, S=S, rows=256),
        out_shape=jax.ShapeDtypeStruct((T, GDN_QKV), F32),
        grid=(B, nc),
        in_specs=[pl.BlockSpec((S, LANES), lambda b, c: (b, c0 + c)),
                  pl.BlockSpec((CONV_WIDTH, LANES), lambda b, c: (0, c))],
        out_specs=pl.BlockSpec((S, LANES), lambda b, c: (b, c)),
        compiler_params=_cparams(("parallel", "arbitrary")),
        name="gdn_conv",
    )(proj, conv_w)


def _gdn_gate_kernel(x_ref, alog_ref, dtb_ref, o_ref, *, S, rows):
    lane = lax.broadcasted_iota(jnp.int32, (rows, LANES), 1)
    row = lax.broadcasted_iota(jnp.int32, (rows, LANES), 0)
    pos = row % GDN_C
    neg_rate = -jnp.exp(alog_ref[...])
    dtb = dtb_ref[...]

    def body(i, carry):
        r0 = pl.multiple_of(i * rows, rows)
        x = x_ref[pl.ds(r0, rows), :]
        t = x + dtb
        softplus = jnp.maximum(t, 0.0) + jnp.log1p(jnp.exp(-jnp.abs(t)))
        g = neg_rate * softplus
        d = 1
        while d < GDN_C:
            g = g + jnp.where(pos >= d, pltpu.roll(g, d, axis=0), 0.0)
            d *= 2
        o_ref[pl.ds(r0, rows), :] = jnp.where(lane < GDN_HEADS, _sigmoid(x), g)
        return carry

    lax.fori_loop(0, S // rows, body, 0)


def _gdn_gates(proj, alog128, dtb128, B, S):
    T = proj.shape[0]
    return pl.pallas_call(
        functools.partial(_gdn_gate_kernel, S=S, rows=256),
        out_shape=jax.ShapeDtypeStruct((T, LANES), F32),
        grid=(B,),
        in_specs=[pl.BlockSpec((S, LANES), lambda b: (b, COL_BA // LANES)),
                  pl.BlockSpec((1, LANES), lambda b: (0, 0)),
                  pl.BlockSpec((1, LANES), lambda b: (0, 0))],
        out_specs=pl.BlockSpec((S, LANES), lambda b: (b, 0)),
        compiler_params=_cparams(("parallel",)),
        name="gdn_gates",
    )(proj, alog128, dtb128)


def _gdn_chunk_kernel(q_ref, k_ref, v_ref, gt_ref, z_ref, ng_ref, o_ref, state_sc):
    @pl.when(pl.program_id(1) == 0)
    def _():
        state_sc[...] = jnp.zeros_like(state_sc)

    C = GDN_C
    H = GDN_HEADS
    heads = range(H)
    ri = lax.broadcasted_iota(jnp.int32, (C, C), 0)
    ci = lax.broadcasted_iota(jnp.int32, (C, C), 1)
    incl = ri >= ci
    strict = ri > ci
    merge = jnp.where(strict, ri ^ ci, 2 * C)
    ng = ng_ref[...]
    gt = gt_ref[...]
    gtt = gt.T

    lmat, a_qk, rhs, q_dec, k_dec_t, c_decay = [], [], [], [], [], []
    for h in heads:
        cols = slice(h * GDN_DK, (h + 1) * GDN_DK)
        q = q_ref[:, cols]
        k = k_ref[:, cols]
        v = v_ref[:, cols]
        beta = gt[:, h:h + 1]
        gc = gt[:, H + h:H + h + 1]
        gr = gtt[H + h:H + h + 1, :]
        glast = gc[C - 1:C, :]
        decay = jnp.exp(jnp.where(incl, gc - gr, NEG_BIG))
        eg = jnp.exp(gc)
        kb = k * beta
        a1 = _dot_nt(jnp.concatenate([kb, q], axis=0).astype(BF16), k.astype(BF16))
        lmat.append(a1[:C] * jnp.where(strict, decay, 0.0))
        a_qk.append((a1[C:] * decay).astype(BF16))
        rhs.append(jnp.concatenate([v * beta, kb * eg], axis=1))
        q_dec.append((q * eg).astype(BF16))
        k_dec_t.append((k * jnp.exp(glast - gc)).T.astype(BF16))
        c_decay.append(jnp.exp(glast))

    base = merge < 16
    mpow = [jnp.where(base, l, 0.0) for l in lmat]
    rinv = [-m for m in mpow]
    for _ in range(3):
        mb = [m.astype(BF16) for m in mpow]
        mpow = [_dot(m, m) for m in mb]
        rinv = [r + m + _dot(r.astype(BF16), m.astype(BF16)) for r, m in zip(rinv, mpow)]
    s = 16
    while s < C:
        level = (merge >= s) & (merge < 2 * s)
        off = [jnp.where(level, l, 0.0) for l in lmat]
        y = [c + _dot(r.astype(BF16), c.astype(BF16)) for r, c in zip(rinv, off)]
        rinv = [r - yy - _dot(yy.astype(BF16), r.astype(BF16)) for r, yy in zip(rinv, y)]
        s *= 2

    sol = [x + _dot(r.astype(BF16), x.astype(BF16)) for r, x in zip(rinv, rhs)]
    a2 = [_dot(jnp.concatenate([sol[h][:, GDN_DV:].astype(BF16), q_dec[h]], axis=0),
               state_sc[h].astype(BF16)) for h in heads]
    v_new = [(sol[h][:, :GDN_DV] - a2[h][:C]).astype(BF16) for h in heads]
    out = [a2[h][C:] + _dot(a_qk[h], v_new[h]) for h in heads]
    for h in heads:
        state_sc[h] = state_sc[h] * c_decay[h] + _dot(k_dec_t[h], v_new[h])
    for h in heads:
        cols = slice(h * GDN_DV, (h + 1) * GDN_DV)
        o = out[h]
        on = o * lax.rsqrt(jnp.mean(o * o, axis=-1, keepdims=True) + RMS_EPS) * ng
        o_ref[:, cols] = (on * _silu(z_ref[:, cols])).astype(o_ref.dtype)


def _gdn_chunks(qkv, gates, proj, norm_g, B, S):
    T = qkv.shape[0]
    R = GDN_C
    nr = S // R
    W = GDN_HEADS * GDN_DK
    row = lambda c: (lambda b, r: (b * nr + r, c))
    return pl.pallas_call(
        _gdn_chunk_kernel,
        out_shape=jax.ShapeDtypeStruct((T, GDN_HEADS * GDN_DV), BF16),
        grid=(B, nr),
        in_specs=[pl.BlockSpec((R, W), row(0)),
                  pl.BlockSpec((R, W), row(1)),
                  pl.BlockSpec((R, W), row(2)),
                  pl.BlockSpec((R, LANES), row(0)),
                  pl.BlockSpec((R, W), row(COL_Z // W)),
                  pl.BlockSpec((1, GDN_DV), lambda b, r: (0, 0))],
        out_specs=pl.BlockSpec((R, W), row(0)),
        scratch_shapes=[pltpu.VMEM((GDN_HEADS, GDN_DK, GDN_DV), F32)],
        compiler_params=_cparams(("parallel", "arbitrary")),
        name="gdn_chunks",
    )(qkv, qkv, qkv, gates, proj, norm_g)


def _out_proj_kernel(a_ref, s_ref, g_ref, x_ref, wa_ref, ws_ref, wg_ref, lg_ref, lb_ref, of_ref, ob_ref):
    y = _dot(a_ref[...], wa_ref[...]) + _dot(s_ref[...], ws_ref[...]) + _dot(g_ref[...], wg_ref[...])
    y = _layer_norm_rows(DN_ALPHA * x_ref[...] + y, lg_ref[...], lb_ref[...])
    of_ref[...] = y
    ob_ref[...] = y.astype(BF16)


def _out_proj_ln(mla, swa, gdn, x, w_out, l, ln_g, ln_b):
    T = x.shape[0]
    tm = 256
    n_mla, n_swa, n_gdn = mla.shape[1], swa.shape[1], gdn.shape[1]
    assert n_mla % n_swa == 0 and (n_mla + n_swa) % n_gdn == 0
    row = lambda w: pl.BlockSpec((tm, w), lambda i: (i, 0))
    full = lambda shape: pl.BlockSpec(shape, lambda i: (0, 0))
    return pl.pallas_call(
        _out_proj_kernel,
        out_shape=(jax.ShapeDtypeStruct((T, D_MODEL), F32), jax.ShapeDtypeStruct((T, D_MODEL), BF16)),
        grid=(T // tm,),
        in_specs=[row(n_mla), row(n_swa), row(n_gdn), row(D_MODEL),
                  pl.BlockSpec((None, n_mla, D_MODEL), lambda i: (l, 0, 0)),
                  pl.BlockSpec((None, n_swa, D_MODEL), lambda i: (l, n_mla // n_swa, 0)),
                  pl.BlockSpec((None, n_gdn, D_MODEL), lambda i: (l, (n_mla + n_swa) // n_gdn, 0)),
                  full((1, D_MODEL)), full((1, D_MODEL))],
        out_specs=(row(D_MODEL), row(D_MODEL)),
        compiler_params=_cparams(("parallel",)),
        name="out_proj_ln",
    )(mla, swa, gdn, x, w_out, w_out, w_out, ln_g, ln_b)


def _ffn_kernel(xb_ref, xf_ref, wg_ref, wu_ref, wd_ref, lg_ref, lb_ref, of_ref, ob_ref, acc_sc):
    j = pl.program_id(1)

    @pl.when(j == 0)
    def _():
        acc_sc[...] = jnp.zeros_like(acc_sc)

    xb = xb_ref[...]
    h = _silu(_dot(xb, wg_ref[...])) * _dot(xb, wu_ref[...])
    acc_sc[...] += _dot(h.astype(BF16), wd_ref[...])

    @pl.when(j == pl.num_programs(1) - 1)
    def _():
        y = _layer_norm_rows(DN_ALPHA * xf_ref[...] + acc_sc[...], lg_ref[...], lb_ref[...])
        of_ref[...] = y
        ob_ref[...] = y.astype(BF16)


def _ffn_ln(xb, xf, wg, wu, wd, m, ln_g, ln_b):
    T = xb.shape[0]
    tm, tf = 512, 512
    return pl.pallas_call(
        _ffn_kernel,
        out_shape=(jax.ShapeDtypeStruct((T, D_MODEL), F32), jax.ShapeDtypeStruct((T, D_MODEL), BF16)),
        grid=(T // tm, D_FF // tf),
        in_specs=[pl.BlockSpec((tm, D_MODEL), lambda i, j: (i, 0)),
                  pl.BlockSpec((tm, D_MODEL), lambda i, j: (i, 0)),
                  pl.BlockSpec((None, D_MODEL, tf), lambda i, j: (m, 0, j)),
                  pl.BlockSpec((None, D_MODEL, tf), lambda i, j: (m, 0, j)),
                  pl.BlockSpec((None, tf, D_MODEL), lambda i, j: (m, j, 0)),
                  pl.BlockSpec((1, D_MODEL), lambda i, j: (0, 0)),
                  pl.BlockSpec((1, D_MODEL), lambda i, j: (0, 0))],
        out_specs=(pl.BlockSpec((tm, D_MODEL), lambda i, j: (i, 0)),
                   pl.BlockSpec((tm, D_MODEL), lambda i, j: (i, 0))),
        scratch_shapes=[pltpu.VMEM((tm, D_MODEL), F32)],
        compiler_params=_cparams(("parallel", "arbitrary")),
        name="ffn_ln",
    )(xb, xf, wg, wu, wd, ln_g, ln_b)


def _router_kernel(x_ref, w_ref, o_ref):
    logits = jnp.dot(x_ref[...], w_ref[...], preferred_element_type=F32, precision=lax.Precision.HIGHEST)
    lane = lax.broadcasted_iota(jnp.int32, logits.shape, 1).astype(F32)
    logits = jnp.where(lane < N_EXPERTS, logits, -jnp.inf)
    m1 = jnp.max(logits, axis=1, keepdims=True)
    i1 = jnp.min(jnp.where(logits == m1, lane, float(LANES)), axis=1, keepdims=True)
    rest = jnp.where(lane == i1, -jnp.inf, logits)
    m2 = jnp.max(rest, axis=1, keepdims=True)
    i2 = jnp.min(jnp.where(rest == m2, lane, float(LANES)), axis=1, keepdims=True)
    e = jnp.exp(m2 - m1)
    g1 = 1.0 / (1.0 + e)
    g2 = e / (1.0 + e)
    out = jnp.where(lane == 0, i1,
                    jnp.where(lane == 1, i2,
                              jnp.where(lane == 2, g1, jnp.where(lane == 3, g2, 0.0))))
    o_ref[...] = out


def _router(xf, w128):
    T = xf.shape[0]
    tm = 512
    return pl.pallas_call(
        _router_kernel,
        out_shape=jax.ShapeDtypeStruct((T, LANES), F32),
        grid=(T // tm,),
        in_specs=[pl.BlockSpec((tm, D_MODEL), lambda i: (i, 0)),
                  pl.BlockSpec((D_MODEL, LANES), lambda i: (0, 0))],
        out_specs=pl.BlockSpec((tm, LANES), lambda i: (i, 0)),
        compiler_params=_cparams(("parallel",)),
        name="moe_router",
    )(xf, w128)


MOE_TM = 512
MOE_TF = 512


def _moe_kernel(te_ref, nu_ref, nv_ref, tok_ref, dst_ref, x_hbm, wg_ref, wu_ref, wd_ref, o_hbm,
                xg_sc, xb_sc, acc_sc, gsem, ssem):
    i = pl.program_id(0)
    j = pl.program_id(1)
    nj = pl.num_programs(1)
    base = i * MOE_TM
    active = i < nu_ref[0]
    n_valid = nv_ref[i]

    @pl.when(active & (j == 0))
    def _():
        def issue(r, carry):
            pltpu.make_async_copy(x_hbm.at[tok_ref[base + r]], xg_sc.at[r], gsem).start()
            return carry
        lax.fori_loop(0, MOE_TM, issue, 0, unroll=8)
        pltpu.make_async_copy(x_hbm.at[pl.ds(0, MOE_TM)], xg_sc, gsem).wait()
        xb_sc[...] = xg_sc[...].astype(BF16)
        acc_sc[...] = jnp.zeros_like(acc_sc)

    @pl.when(active)
    def _():
        xb = xb_sc[...]
        h = _silu(_dot(xb, wg_ref[...])) * _dot(xb, wu_ref[...])
        acc_sc[...] += _dot(h.astype(BF16), wd_ref[...])

    @pl.when(active & (j == nj - 1))
    def _():
        def issue(r, carry):
            pltpu.make_async_copy(acc_sc.at[r], o_hbm.at[dst_ref[base + r]], ssem).start()
            return carry
        lax.fori_loop(0, n_valid, issue, 0)

        def drain(r, carry):
            pltpu.make_async_copy(acc_sc.at[r], o_hbm.at[0], ssem).wait()
            return carry
        lax.fori_loop(0, n_valid, drain, 0)


def _moe_grouped(tile_expert, n_used, tile_valid, row_token, row_dest, xf, wg, wu, wd, m, out_rows):
    n_tiles = tile_expert.shape[0]
    nj = D_FF // MOE_TF

    def wcol(i, j, te, nu, nv, tok, dst):
        jj = jnp.where(i < nu[0], j, nj - 1)
        return (m, te[i], 0, jj)

    def wrow(i, j, te, nu, nv, tok, dst):
        jj = jnp.where(i < nu[0], j, nj - 1)
        return (m, te[i], jj, 0)

    grid_spec = pltpu.PrefetchScalarGridSpec(
        num_scalar_prefetch=5,
        grid=(n_tiles, nj),
        in_specs=[pl.BlockSpec(memory_space=pl.ANY),
                  pl.BlockSpec((None, None, D_MODEL, MOE_TF), wcol),
                  pl.BlockSpec((None, None, D_MODEL, MOE_TF), wcol),
                  pl.BlockSpec((None, None, MOE_TF, D_MODEL), wrow)],
        out_specs=pl.BlockSpec(memory_space=pl.ANY),
        scratch_shapes=[pltpu.VMEM((MOE_TM, D_MODEL), F32),
                        pltpu.VMEM((MOE_TM, D_MODEL), BF16),
                        pltpu.VMEM((MOE_TM, D_MODEL), F32),
                        pltpu.SemaphoreType.DMA,
                        pltpu.SemaphoreType.DMA],
    )
    return pl.pallas_call(
        _moe_kernel,
        out_shape=jax.ShapeDtypeStruct((out_rows, D_MODEL), F32),
        grid_spec=grid_spec,
        compiler_params=_cparams(("arbitrary", "arbitrary")),
        name="moe_grouped",
    )(tile_expert, n_used, tile_valid, row_token, row_dest, xf, wg, wu, wd)


def _moe_combine_kernel(o0_ref, o1_ref, rt_ref, x_ref, lg_ref, lb_ref, of_ref, ob_ref):
    rt = rt_ref[...]
    f = rt[:, 2:3] * o0_ref[...] + rt[:, 3:4] * o1_ref[...]
    y = _layer_norm_rows(DN_ALPHA * x_ref[...] + f, lg_ref[...], lb_ref[...])
    of_ref[...] = y
    ob_ref[...] = y.astype(BF16)


def _moe_combine_ln(o_rows, route, xf, ln_g, ln_b):
    T = xf.shape[0]
    tm = 512
    nt = T // tm
    return pl.pallas_call(
        _moe_combine_kernel,
        out_shape=(jax.ShapeDtypeStruct((T, D_MODEL), F32), jax.ShapeDtypeStruct((T, D_MODEL), BF16)),
        grid=(nt,),
        in_specs=[pl.BlockSpec((tm, D_MODEL), lambda i: (i, 0)),
                  pl.BlockSpec((tm, D_MODEL), lambda i: (i + nt, 0)),
                  pl.BlockSpec((tm, LANES), lambda i: (i, 0)),
                  pl.BlockSpec((tm, D_MODEL), lambda i: (i, 0)),
                  pl.BlockSpec((1, D_MODEL), lambda i: (0, 0)),
                  pl.BlockSpec((1, D_MODEL), lambda i: (0, 0))],
        out_specs=(pl.BlockSpec((tm, D_MODEL), lambda i: (i, 0)),
                   pl.BlockSpec((tm, D_MODEL), lambda i: (i, 0))),
        compiler_params=_cparams(("parallel",)),
        name="moe_combine_ln",
    )(o_rows, o_rows, route, xf, ln_g, ln_b)


def _moe_plan(route, T):
    ids = route[:, :TOP_K].astype(jnp.int32)
    e_flat = ids.reshape(-1)
    onehot = (e_flat[:, None] == jnp.arange(N_EXPERTS, dtype=jnp.int32)[None, :]).astype(jnp.int32)
    csum = jnp.cumsum(onehot, axis=0)
    rank = jnp.sum(csum * onehot, axis=1) - 1
    counts = csum[-1]
    tiles_per = (counts + MOE_TM - 1) // MOE_TM
    tile_end = jnp.cumsum(tiles_per)
    tile_start = tile_end - tiles_per
    n_used = tile_end[-1]
    n_tiles = (TOP_K * T) // MOE_TM + N_EXPERTS
    pos = tile_start[e_flat] * MOE_TM + rank
    pair = jnp.arange(TOP_K * T, dtype=jnp.int32)
    token = pair // TOP_K
    slot = pair % TOP_K
    n_rows = n_tiles * MOE_TM
    row_token = jnp.zeros((n_rows,), jnp.int32).at[pos].set(token)
    row_dest = jnp.zeros((n_rows,), jnp.int32).at[pos].set(slot * T + token)
    tiles = jnp.arange(n_tiles, dtype=jnp.int32)
    tidx = jnp.minimum(tiles, n_used - 1)
    tile_expert = jnp.sum((tidx[:, None] >= tile_end[None, :]).astype(jnp.int32), axis=1)
    tile_expert = jnp.minimum(tile_expert, N_EXPERTS - 1).astype(jnp.int32)
    tile_valid = jnp.clip(counts[tile_expert] - (tiles - tile_start[tile_expert]) * MOE_TM, 0, MOE_TM)
    tile_valid = jnp.where(tiles < n_used, tile_valid, 0).astype(jnp.int32)
    return tile_expert, n_used.reshape(1).astype(jnp.int32), tile_valid, row_token, row_dest


def _prep_w_in(w_in):
    L = w_in.shape[0]
    z = lambda n: jnp.zeros((L, D_MODEL, n), w_in.dtype)
    o_swa = MLA_IN
    o_gdn = MLA_IN + SWA_IN
    hd = SWA_HEAD_DIM
    kr = w_in[:, :, MLA_Q_RANK + MLA_KV_RANK:MLA_IN]
    half = MLA_ROPE // 2
    kr_sw = jnp.concatenate([kr[:, :, half:], kr[:, :, :half]], axis=-1)
    ba = w_in[:, :, o_gdn + GDN_QKV:o_gdn + GDN_QKV + 2 * GDN_HEADS]
    segs = [
        w_in[:, :, :MLA_Q_RANK],
        w_in[:, :, o_swa:o_swa + SWA_HEADS * hd],
        w_in[:, :, o_gdn + GDN_QKV + 2 * GDN_HEADS:],
        w_in[:, :, o_gdn:o_gdn + GDN_QKV],
        w_in[:, :, MLA_Q_RANK:MLA_Q_RANK + MLA_KV_RANK],
        w_in[:, :, o_swa + SWA_HEADS * hd:o_swa + (SWA_HEADS + SWA_KV_HEADS) * hd],
        w_in[:, :, o_swa + (SWA_HEADS + SWA_KV_HEADS) * hd:o_swa + SWA_IN],
        kr, kr_sw, z(LANES - 2 * MLA_ROPE),
        ba, z(LANES - 2 * GDN_HEADS),
        z(D_PROJ - COL_BA - LANES),
    ]
    return jnp.concatenate(segs, axis=-1).astype(BF16)


def _prep_mla_weights(w_uq, w_ukv):
    L = w_uq.shape[0]
    half = MLA_ROPE // 2
    wq = w_uq.reshape(L, MLA_Q_RANK, MLA_HEADS, MLA_NOPE + MLA_ROPE)
    nope, rope = wq[..., :MLA_NOPE], wq[..., MLA_NOPE:]
    rope_sw = jnp.concatenate([rope[..., half:], rope[..., :half]], axis=-1)
    zpad = jnp.zeros(wq.shape[:3] + (MLA_HP - MLA_NOPE - MLA_ROPE,), wq.dtype)
    wq_pad = jnp.concatenate([nope, rope, zpad], axis=-1).reshape(L, MLA_Q_RANK, MLA_HEADS * MLA_HP)
    wq_sw = jnp.concatenate([jnp.zeros_like(nope), rope_sw, zpad], axis=-1).reshape(L, MLA_Q_RANK, MLA_HEADS * MLA_HP)
    wkv = w_ukv.reshape(L, MLA_KV_RANK, MLA_HEADS, MLA_NOPE + MLA_V)
    knope, vv = wkv[..., :MLA_NOPE], wkv[..., MLA_NOPE:]
    wk_pad = jnp.concatenate([knope, jnp.zeros(knope.shape[:3] + (MLA_HP - MLA_NOPE,), wkv.dtype)], axis=-1)
    wk_pad = wk_pad.reshape(L, MLA_KV_RANK, MLA_HEADS * MLA_HP)
    wv = vv.reshape(L, MLA_KV_RANK, MLA_HEADS * MLA_V)
    return wq_pad.astype(BF16), wq_sw.astype(BF16), wk_pad.astype(BF16), wv.astype(BF16)


def _rope_tables(S):
    half = MLA_ROPE // 2
    pos = jnp.arange(S, dtype=jnp.int32)
    inv_freq = ROPE_BASE ** (-jnp.arange(half, dtype=F32) / half)
    ang = pos.astype(F32)[:, None] * inv_freq[None, :]
    cos, sin = jnp.cos(ang), jnp.sin(ang)
    cos2 = jnp.concatenate([cos, cos], axis=-1)
    sin2 = jnp.concatenate([-sin, sin], axis=-1)
    scale = (MLA_NOPE + MLA_ROPE) ** -0.5
    tail = jnp.zeros((S, MLA_HP - MLA_NOPE - MLA_ROPE), F32)
    cq_head = jnp.concatenate([jnp.ones((S, MLA_NOPE), F32), cos2, tail], axis=-1) * scale
    sq_head = jnp.concatenate([jnp.zeros((S, MLA_NOPE), F32), sin2, tail], axis=-1) * scale
    cqt = jnp.tile(cq_head, (1, MLA_HEADS))
    sqt = jnp.tile(sq_head, (1, MLA_HEADS))
    ckt = jnp.concatenate([cos2, sin2, jnp.zeros((S, LANES - 2 * MLA_ROPE), F32)], axis=-1)
    j = jnp.arange(LANES)[:, None]
    col = jnp.arange(MLA_HEADS * MLA_HP)[None, :]
    place = ((col % MLA_HP) == (MLA_NOPE + (j % MLA_ROPE))) & (j < 2 * MLA_ROPE)
    return cqt, sqt, ckt, place.astype(BF16)


def _pad_lanes(v, offset):
    out = jnp.zeros((1, LANES), F32)
    return lax.dynamic_update_slice(out, v.astype(F32).reshape(1, -1), (0, offset))


def kernel(x, w_in, mla_q_norm, mla_w_uq, mla_kv_norm, mla_w_ukv, swa_sinks, gdn_conv,
           gdn_a_log, gdn_dt_bias, gdn_norm, w_out, ln1_g, ln1_b, ln2_g, ln2_b,
           ffn_w_gate, ffn_w_up, ffn_w_down, moe_router, moe_w_gate, moe_w_up, moe_w_down):
    B, S, D = x.shape
    T = B * S
    xf = x.reshape(T, D)
    xb = xf.astype(BF16)

    w_in_b = _prep_w_in(w_in)
    wq_pad, wq_sw, wk_pad, wv = _prep_mla_weights(mla_w_uq, mla_w_ukv)
    cqt, sqt, ckt, place = _rope_tables(S)
    w_out_b = w_out.astype(BF16)
    ffn_g, ffn_u, ffn_d = ffn_w_gate.astype(BF16), ffn_w_up.astype(BF16), ffn_w_down.astype(BF16)
    moe_g, moe_u, moe_d = moe_w_gate.astype(BF16), moe_w_up.astype(BF16), moe_w_down.astype(BF16)
    router128 = jnp.pad(moe_router.astype(F32), ((0, 0), (0, 0), (0, LANES - N_EXPERTS)))

    for l in range(DEPTH):
        proj = _in_proj(xb, w_in_b, l)
        q, k, v = _mla_prep(proj, mla_q_norm[l].reshape(1, -1), mla_kv_norm[l].reshape(1, -1),
                            wq_pad, wq_sw, wk_pad, wv, l, place, cqt, sqt, ckt, S)
        h_mla = _mla_attn(q, k, v, B, S)
        h_swa = _swa(proj, swa_sinks[l].astype(F32), B, S)
        qkv = _gdn_conv(proj, gdn_conv[l], B, S)
        gates = _gdn_gates(proj, _pad_lanes(gdn_a_log[l], GDN_HEADS), _pad_lanes(gdn_dt_bias[l], GDN_HEADS), B, S)
        h_gdn = _gdn_chunks(qkv, gates, proj, gdn_norm[l].reshape(1, -1), B, S)
        xf, xb = _out_proj_ln(h_mla, h_swa, h_gdn, xf, w_out_b, l,
                              ln1_g[l].reshape(1, -1), ln1_b[l].reshape(1, -1))
        lg, lb = ln2_g[l].reshape(1, -1), ln2_b[l].reshape(1, -1)
        if l % 2 == 0:
            xf, xb = _ffn_ln(xb, xf, ffn_g, ffn_u, ffn_d, l // 2, lg, lb)
        else:
            m = l // 2
            route = _router(xf, router128[m])
            tile_expert, n_used, tile_valid, row_token, row_dest = _moe_plan(route, T)
            o_rows = _moe_grouped(tile_expert, n_used, tile_valid, row_token, row_dest, xf,
                                  moe_g, moe_u, moe_d, m, TOP_K * T)
            xf, xb = _moe_combine_ln(o_rows, route, xf, lg, lb)
    return xf.reshape(B, S, D)
```

```python
import functools
import math

import jax
import jax.numpy as jnp
from jax import lax
from jax.experimental import pallas as pl
from jax.experimental.pallas import tpu as pltpu

F32 = jnp.float32
BF16 = jnp.bfloat16

D_MODEL = 2048
DEPTH = 4
MLA_HEADS = 8
MLA_Q_RANK = 512
MLA_KV_RANK = 256
MLA_NOPE = 64
MLA_ROPE = 32
MLA_V = 64
ROPE_BASE = 10000.0
SWA_HEADS = 8
SWA_KV_HEADS = 2
SWA_GROUP = SWA_HEADS // SWA_KV_HEADS
SWA_HEAD_DIM = 64
WINDOW = 128
GDN_HEADS = 8
GDN_DK = 128
GDN_DV = 128
CONV_WIDTH = 4
GDN_C = 256
D_FF = 7168
N_EXPERTS = 8
TOP_K = 2
DN_ALPHA = float((2 * DEPTH) ** 0.25)
LN_EPS = 1e-5
RMS_EPS = 1e-6

MLA_IN = MLA_Q_RANK + MLA_KV_RANK + MLA_ROPE
SWA_IN = (SWA_HEADS + 2 * SWA_KV_HEADS) * SWA_HEAD_DIM
GDN_QKV = GDN_HEADS * (2 * GDN_DK + GDN_DV)

LANES = 128
NEG_BIG = -1e30
VMEM_LIMIT = 56 * 1024 * 1024

COL_CQ = 0
COL_SWAQ = 512
COL_Z = 1024
COL_GQ = 2048
COL_GK = 3072
COL_GV = 4096
COL_CKV = 5120
COL_SWAK = 5376
COL_SWAV = 5504
COL_KR = 5632
COL_BA = 5760
D_PROJ = 6144

MLA_HP = 128


def _cparams(sem, vmem=VMEM_LIMIT):
    return pltpu.CompilerParams(dimension_semantics=sem, vmem_limit_bytes=vmem)


def _layer_norm_rows(y, g, b):
    mu = jnp.mean(y, axis=-1, keepdims=True)
    d = y - mu
    var = jnp.mean(d * d, axis=-1, keepdims=True)
    return d * lax.rsqrt(var + LN_EPS) * g + b


def _sigmoid(x):
    return 1.0 / (1.0 + jnp.exp(-x))


def _silu(x):
    return x * _sigmoid(x)


def _dot(a, b):
    return jnp.dot(a, b, preferred_element_type=F32)


def _dot_nt(a, b):
    return lax.dot_general(a, b, (((1,), (1,)), ((), ())), preferred_element_type=F32)


def _matmul_kernel(x_ref, w_ref, o_ref):
    o_ref[...] = _dot(x_ref[...], w_ref[...])


def _in_proj(xb, w, l):
    T = xb.shape[0]
    tm, tn = 1024, 512
    return pl.pallas_call(
        _matmul_kernel,
        out_shape=jax.ShapeDtypeStruct((T, D_PROJ), F32),
        grid=(T // tm, D_PROJ // tn),
        in_specs=[pl.BlockSpec((tm, D_MODEL), lambda i, j: (i, 0)),
                  pl.BlockSpec((None, D_MODEL, tn), lambda i, j: (l, 0, j))],
        out_specs=pl.BlockSpec((tm, tn), lambda i, j: (i, j)),
        compiler_params=_cparams(("parallel", "arbitrary")),
        name="in_proj",
    )(xb, w)


def _mla_prep_kernel(cq_ref, ckv_ref, kr_ref, gq_ref, gkv_ref, wq_ref, wqs_ref, wk_ref, wv_ref,
                     e_ref, cqt_ref, sqt_ref, ckt_ref, q_ref, k_ref, v_ref):
    cq = cq_ref[...]
    hq = cq * lax.rsqrt(jnp.mean(cq * cq, axis=-1, keepdims=True) + RMS_EPS) * gq_ref[...]
    hq = hq.astype(BF16)
    q = _dot(hq, wq_ref[...]) * cqt_ref[...] + _dot(hq, wqs_ref[...]) * sqt_ref[...]
    q_ref[...] = q.astype(BF16)
    ckv = ckv_ref[...]
    hk = ckv * lax.rsqrt(jnp.mean(ckv * ckv, axis=-1, keepdims=True) + RMS_EPS) * gkv_ref[...]
    hk = hk.astype(BF16)
    krp = kr_ref[...] * ckt_ref[...]
    krp_hi = krp.astype(BF16)
    krp_lo = (krp - krp_hi.astype(F32)).astype(BF16)
    e = e_ref[...]
    k = _dot(hk, wk_ref[...]) + (_dot(krp_hi, e) + _dot(krp_lo, e))
    k_ref[...] = k.astype(BF16)
    v_ref[...] = _dot(hk, wv_ref[...]).astype(BF16)


def _mla_prep(proj, gq, gkv, wq, wqs, wk, wv, l, e, cqt, sqt, ckt, S):
    T = proj.shape[0]
    tm = 512
    sb = S // tm
    full = lambda shape: pl.BlockSpec(shape, lambda i: (0, 0))
    layer = lambda shape: pl.BlockSpec((None,) + shape, lambda i: (l, 0, 0))
    return pl.pallas_call(
        _mla_prep_kernel,
        out_shape=(jax.ShapeDtypeStruct((T, MLA_HEADS * MLA_HP), BF16),
                   jax.ShapeDtypeStruct((T, MLA_HEADS * MLA_HP), BF16),
                   jax.ShapeDtypeStruct((T, MLA_HEADS * MLA_V), BF16)),
        grid=(T // tm,),
        in_specs=[pl.BlockSpec((tm, MLA_Q_RANK), lambda i: (i, COL_CQ // MLA_Q_RANK)),
                  pl.BlockSpec((tm, MLA_KV_RANK), lambda i: (i, COL_CKV // MLA_KV_RANK)),
                  pl.BlockSpec((tm, LANES), lambda i: (i, COL_KR // LANES)),
                  full((1, MLA_Q_RANK)), full((1, MLA_KV_RANK)),
                  layer((MLA_Q_RANK, MLA_HEADS * MLA_HP)), layer((MLA_Q_RANK, MLA_HEADS * MLA_HP)),
                  layer((MLA_KV_RANK, MLA_HEADS * MLA_HP)), layer((MLA_KV_RANK, MLA_HEADS * MLA_V)),
                  full((LANES, MLA_HEADS * MLA_HP)),
                  pl.BlockSpec((tm, MLA_HEADS * MLA_HP), lambda i: (i % sb, 0)),
                  pl.BlockSpec((tm, MLA_HEADS * MLA_HP), lambda i: (i % sb, 0)),
                  pl.BlockSpec((tm, LANES), lambda i: (i % sb, 0))],
        out_specs=(pl.BlockSpec((tm, MLA_HEADS * MLA_HP), lambda i: (i, 0)),
                   pl.BlockSpec((tm, MLA_HEADS * MLA_HP), lambda i: (i, 0)),
                   pl.BlockSpec((tm, MLA_HEADS * MLA_V), lambda i: (i, 0))),
        compiler_params=_cparams(("parallel",)),
        name="mla_prep",
    )(proj, proj, proj, gq, gkv, wq, wqs, wk, wv, e, cqt, sqt, ckt)


MLA_G = 2


def _mla_attn_kernel(q_ref, k_ref, v_ref, o_ref, acc_sc, m_sc, l_sc, *, tq, tk):
    qi = pl.program_id(2)
    acc_sc[...] = jnp.zeros_like(acc_sc)
    m_sc[...] = jnp.full_like(m_sc, NEG_BIG)
    l_sc[...] = jnp.zeros_like(l_sc)
    ratio = tq // tk

    def step(kj, diag):
        rows = pl.ds(pl.multiple_of(kj * tk, tk), tk)
        for g in range(MLA_G):
            s = _dot_nt(q_ref[:, g * MLA_HP:(g + 1) * MLA_HP], k_ref[rows, g * MLA_HP:(g + 1) * MLA_HP])
            if diag is not None:
                r = lax.broadcasted_iota(jnp.int32, s.shape, 0)
                c = lax.broadcasted_iota(jnp.int32, s.shape, 1)
                s = jnp.where(c + diag * tk <= r, s, NEG_BIG)
            m_prev = m_sc[g]
            m_new = jnp.maximum(m_prev, jnp.max(s, axis=1, keepdims=True))
            alpha = jnp.exp(m_prev - m_new)
            p = jnp.exp(s - m_new[:, :1])
            l_sc[g] = alpha * l_sc[g] + jnp.sum(p, axis=1, keepdims=True)
            pv = _dot(p.astype(BF16), v_ref[rows, (g // 2) * LANES:(g // 2 + 1) * LANES])
            acc_sc[g] = acc_sc[g] * alpha + pv
            m_sc[g] = m_new

    def body(kj, carry):
        step(kj, None)
        return carry

    lax.fori_loop(0, qi * ratio, body, 0)
    for d in range(ratio):
        step(qi * ratio + d, d)
    lane = lax.broadcasted_iota(jnp.int32, (tq, LANES), 1)
    for pr in range(MLA_G // 2):
        o0 = acc_sc[2 * pr] / l_sc[2 * pr]
        o1 = acc_sc[2 * pr + 1] / l_sc[2 * pr + 1]
        o_ref[:, pr * LANES:(pr + 1) * LANES] = jnp.where(lane < MLA_V, o0, o1).astype(o_ref.dtype)


def _mla_attn(q, k, v, B, S):
    T = q.shape[0]
    tq, tk = 512, 512
    nq = S // tq
    groups = MLA_HEADS // MLA_G
    return pl.pallas_call(
        functools.partial(_mla_attn_kernel, tq=tq, tk=tk),
        out_shape=jax.ShapeDtypeStruct((T, MLA_HEADS * MLA_V), BF16),
        grid=(B, groups, nq),
        in_specs=[pl.BlockSpec((tq, MLA_G * MLA_HP), lambda b, h, i: (b * nq + i, h)),
                  pl.BlockSpec((S, MLA_G * MLA_HP), lambda b, h, i: (b, h)),
                  pl.BlockSpec((S, MLA_G * MLA_V), lambda b, h, i: (b, h))],
        out_specs=pl.BlockSpec((tq, MLA_G * MLA_V), lambda b, h, i: (b * nq + i, h)),
        scratch_shapes=[pltpu.VMEM((MLA_G, tq, LANES), F32),
                        pltpu.VMEM((MLA_G, tq, LANES), F32),
                        pltpu.VMEM((MLA_G, tq, LANES), F32)],
        compiler_params=_cparams(("parallel", "parallel", "arbitrary")),
        name="mla_attn",
    )(q, k, v)


def _swa_kernel(sink_ref, q_ref, kc_ref, kp_ref, vc_ref, vp_ref, o_ref):
    n = pl.program_id(1)
    hd = SWA_HEAD_DIM
    qi = lax.broadcasted_iota(jnp.int32, (WINDOW, 2 * WINDOW), 0)
    kj = lax.broadcasted_iota(jnp.int32, (WINDOW, 2 * WINDOW), 1)
    delta = qi + WINDOW - kj
    first_key = jnp.where(n > 0, 0, WINDOW)
    valid = (delta >= 0) & (delta < WINDOW) & (kj >= first_key)
    deltaf = delta.astype(F32)
    for g in range(SWA_KV_HEADS):
        kcat = jnp.concatenate([kp_ref[:, g * hd:(g + 1) * hd], kc_ref[:, g * hd:(g + 1) * hd]], axis=0)
        vcat = jnp.concatenate([vp_ref[:, g * hd:(g + 1) * hd], vc_ref[:, g * hd:(g + 1) * hd]], axis=0)
        kcat = kcat.astype(BF16)
        vcat = vcat.astype(BF16)
        for r in range(SWA_GROUP):
            h = g * SWA_GROUP + r
            slope = 2.0 ** (-8.0 * (h + 1) / SWA_HEADS)
            q = q_ref[:, h * hd:(h + 1) * hd].astype(BF16)
            s = _dot_nt(q, kcat) * (hd ** -0.5) - slope * deltaf
            s = jnp.where(valid, s, NEG_BIG)
            sink = sink_ref[h]
            m = jnp.maximum(jnp.max(s, axis=1, keepdims=True), sink)
            p = jnp.exp(s - m)
            denom = jnp.sum(p, axis=1, keepdims=True) + jnp.exp(sink - m)
            o = _dot(p.astype(BF16), vcat) / denom
            o_ref[:, h * hd:(h + 1) * hd] = o.astype(o_ref.dtype)


def _swa(proj, sinks, B, S):
    T = proj.shape[0]
    nb = S // WINDOW
    cq = COL_SWAQ // (SWA_HEADS * SWA_HEAD_DIM)
    ck = COL_SWAK // LANES
    cv = COL_SWAV // LANES
    cur = lambda c: (lambda b, n: (b * nb + n, c))
    prev = lambda c: (lambda b, n: (b * nb + jnp.maximum(n - 1, 0), c))
    return pl.pallas_call(
        _swa_kernel,
        out_shape=jax.ShapeDtypeStruct((T, SWA_HEADS * SWA_HEAD_DIM), BF16),
        grid=(B, nb),
        in_specs=[pl.BlockSpec(memory_space=pltpu.SMEM),
                  pl.BlockSpec((WINDOW, SWA_HEADS * SWA_HEAD_DIM), cur(cq)),
                  pl.BlockSpec((WINDOW, LANES), cur(ck)),
                  pl.BlockSpec((WINDOW, LANES), prev(ck)),
                  pl.BlockSpec((WINDOW, LANES), cur(cv)),
                  pl.BlockSpec((WINDOW, LANES), prev(cv))],
        out_specs=pl.BlockSpec((WINDOW, SWA_HEADS * SWA_HEAD_DIM), lambda b, n: (b * nb + n, 0)),
        compiler_params=_cparams(("parallel", "arbitrary")),
        name="swa_attn",
    )(sinks, proj, proj, proj, proj, proj)


def _gdn_conv_kernel(x_ref, w_ref, o_ref, *, S, rows):
    c = pl.program_id(1)
    w = w_ref[...]
    is_qk = c < 2 * GDN_HEADS
    post = jnp.where(c < GDN_HEADS, GDN_DK ** -0.5, 1.0).astype(F32)
    sub = lax.broadcasted_iota(jnp.int32, (8, LANES), 0)

    def body(i, carry):
        r0 = pl.multiple_of(i * rows, rows)
        cur = x_ref[pl.ds(r0, rows), :]
        p0 = pl.multiple_of(jnp.maximum(r0 - 8, 0), 8)
        prev8 = x_ref[pl.ds(p0, 8), :]
        prev8 = jnp.where(i > 0, prev8, 0.0)
        acc = cur * w[CONV_WIDTH - 1:CONV_WIDTH, :]
        for d in range(1, CONV_WIDTH):
            rolled = pltpu.roll(cur, d, axis=0)
            head = jnp.where(sub < d, pltpu.roll(prev8, d, axis=0), rolled[:8])
            shifted = jnp.concatenate([head, rolled[8:]], axis=0)
            acc = acc + shifted * w[CONV_WIDTH - 1 - d:CONV_WIDTH - d, :]
        y = _silu(acc)
        nrm = lax.rsqrt(jnp.sum(y * y, axis=-1, keepdims=True) + RMS_EPS) * post
        y = y * jnp.where(is_qk, nrm, 1.0)
        o_ref[pl.ds(r0, rows), :] = y
        return carry

    lax.fori_loop(0, S // rows, body, 0)


def _gdn_conv(proj, conv_w, B, S):
    T = proj.shape[0]
    nc = GDN_QKV // LANES
    c0 = COL_GQ // LANES
    return pl.pallas_call(
        functools.partial(_gdn_conv_kernel, S=S, rows=256),
        out_shape=jax.ShapeDtypeStruct((T, GDN_QKV), F32),
        grid=(B, nc),
        in_specs=[pl.BlockSpec((S, LANES), lambda b, c: (b, c0 + c)),
                  pl.BlockSpec((CONV_WIDTH, LANES), lambda b, c: (0, c))],
        out_specs=pl.BlockSpec((S, LANES), lambda b, c: (b, c)),
        compiler_params=_cparams(("parallel", "arbitrary")),
        name="gdn_conv",
    )(proj, conv_w)


def _gdn_gate_kernel(x_ref, alog_ref, dtb_ref, o_ref, *, S, rows):
    lane = lax.broadcasted_iota(jnp.int32, (rows, LANES), 1)
    row = lax.broadcasted_iota(jnp.int32, (rows, LANES), 0)
    pos = row % GDN_C
    neg_rate = -jnp.exp(alog_ref[...])
    dtb = dtb_ref[...]

    def body(i, carry):
        r0 = pl.multiple_of(i * rows, rows)
        x = x_ref[pl.ds(r0, rows), :]
        t = x + dtb
        softplus = jnp.maximum(t, 0.0) + jnp.log1p(jnp.exp(-jnp.abs(t)))
        g = neg_rate * softplus
        d = 1
        while d < GDN_C:
            g = g + jnp.where(pos >= d, pltpu.roll(g, d, axis=0), 0.0)
            d *= 2
        o_ref[pl.ds(r0, rows), :] = jnp.where(lane < GDN_HEADS, _sigmoid(x), g)
        return carry

    lax.fori_loop(0, S // rows, body, 0)


def _gdn_gates(proj, alog128, dtb128, B, S):
    T = proj.shape[0]
    return pl.pallas_call(
        functools.partial(_gdn_gate_kernel, S=S, rows=256),
        out_shape=jax.ShapeDtypeStruct((T, LANES), F32),
        grid=(B,),
        in_specs=[pl.BlockSpec((S, LANES), lambda b: (b, COL_BA // LANES)),
                  pl.BlockSpec((1, LANES), lambda b: (0, 0)),
                  pl.BlockSpec((1, LANES), lambda b: (0, 0))],
        out_specs=pl.BlockSpec((S, LANES), lambda b: (b, 0)),
        compiler_params=_cparams(("parallel",)),
        name="gdn_gates",
    )(proj, alog128, dtb128)


def _gdn_chunk_kernel(q_ref, k_ref, v_ref, gt_ref, z_ref, ng_ref, o_ref, state_sc):
    @pl.when(pl.program_id(1) == 0)
    def _():
        state_sc[...] = jnp.zeros_like(state_sc)

    C = GDN_C
    H = GDN_HEADS
    heads = range(H)
    ri = lax.broadcasted_iota(jnp.int32, (C, C), 0)
    ci = lax.broadcasted_iota(jnp.int32, (C, C), 1)
    incl = ri >= ci
    strict = ri > ci
    merge = jnp.where(strict, ri ^ ci, 2 * C)
    ng = ng_ref[...]
    gt = gt_ref[...]
    gtt = gt.T

    lmat, a_qk, rhs, q_dec, k_dec_t, c_decay = [], [], [], [], [], []
    for h in heads:
        cols = slice(h * GDN_DK, (h + 1) * GDN_DK)
        q = q_ref[:, cols]
        k = k_ref[:, cols]
        v = v_ref[:, cols]
        beta = gt[:, h:h + 1]
        gc = gt[:, H + h:H + h + 1]
        gr = gtt[H + h:H + h + 1, :]
        glast = gc[C - 1:C, :]
        decay = jnp.exp(jnp.where(incl, gc - gr, NEG_BIG))
        eg = jnp.exp(gc)
        kb = k * beta
        a1 = _dot_nt(jnp.concatenate([kb, q], axis=0).astype(BF16), k.astype(BF16))
        lmat.append(a1[:C] * jnp.where(strict, decay, 0.0))
        a_qk.append((a1[C:] * decay).astype(BF16))
        rhs.append(jnp.concatenate([v * beta, kb * eg], axis=1))
        q_dec.append((q * eg).astype(BF16))
        k_dec_t.append((k * jnp.exp(glast - gc)).T.astype(BF16))
        c_decay.append(jnp.exp(glast))

    base = merge < 16
    mpow = [jnp.where(base, l, 0.0) for l in lmat]
    rinv = [-m for m in mpow]
    for _ in range(3):
        mb = [m.astype(BF16) for m in mpow]
        mpow = [_dot(m, m) for m in mb]
        rinv = [r + m + _dot(r.astype(BF16), m.astype(BF16)) for r, m in zip(rinv, mpow)]
    s = 16
    while s < C:
        level = (merge >= s) & (merge < 2 * s)
        off = [jnp.where(level, l, 0.0) for l in lmat]
        y = [c + _dot(r.astype(BF16), c.astype(BF16)) for r, c in zip(rinv, off)]
        rinv = [r - yy - _dot(yy.astype(BF16), r.astype(BF16)) for r, yy in zip(rinv, y)]
        s *= 2

    sol = [x + _dot(r.astype(BF16), x.astype(BF16)) for r, x in zip(rinv, rhs)]
    a2 = [_dot(jnp.concatenate([sol[h][:, GDN_DV:].astype(BF16), q_dec[h]], axis=0),
               state_sc[h].astype(BF16)) for h in heads]
    v_new = [(sol[h][:, :GDN_DV] - a2[h][:C]).astype(BF16) for h in heads]
    out = [a2[h][C:] + _dot(a_qk[h], v_new[h]) for h in heads]
    for h in heads:
        state_sc[h] = state_sc[h] * c_decay[h] + _dot(k_dec_t[h], v_new[h])
    for h in heads:
        cols = slice(h * GDN_DV, (h + 1) * GDN_DV)
        o = out[h]
        on = o * lax.rsqrt(jnp.mean(o * o, axis=-1, keepdims=True) + RMS_EPS) * ng
        o_ref[:, cols] = (on * _silu(z_ref[:, cols])).astype(o_ref.dtype)


def _gdn_chunks(qkv, gates, proj, norm_g, B, S):
    T = qkv.shape[0]
    R = GDN_C
    nr = S // R
    W = GDN_HEADS * GDN_DK
    row = lambda c: (lambda b, r: (b * nr + r, c))
    return pl.pallas_call(
        _gdn_chunk_kernel,
        out_shape=jax.ShapeDtypeStruct((T, GDN_HEADS * GDN_DV), BF16),
        grid=(B, nr),
        in_specs=[pl.BlockSpec((R, W), row(0)),
                  pl.BlockSpec((R, W), row(1)),
                  pl.BlockSpec((R, W), row(2)),
                  pl.BlockSpec((R, LANES), row(0)),
                  pl.BlockSpec((R, W), row(COL_Z // W)),
                  pl.BlockSpec((1, GDN_DV), lambda b, r: (0, 0))],
        out_specs=pl.BlockSpec((R, W), row(0)),
        scratch_shapes=[pltpu.VMEM((GDN_HEADS, GDN_DK, GDN_DV), F32)],
        compiler_params=_cparams(("parallel", "arbitrary")),
        name="gdn_chunks",
    )(qkv, qkv, qkv, gates, proj, norm_g)


def _out_proj_kernel(a_ref, s_ref, g_ref, x_ref, wa_ref, ws_ref, wg_ref, lg_ref, lb_ref, of_ref, ob_ref):
    y = _dot(a_ref[...], wa_ref[...]) + _dot(s_ref[...], ws_ref[...]) + _dot(g_ref[...], wg_ref[...])
    y = _layer_norm_rows(DN_ALPHA * x_ref[...] + y, lg_ref[...], lb_ref[...])
    of_ref[...] = y
    ob_ref[...] = y.astype(BF16)


def _out_proj_ln(mla, swa, gdn, x, w_out, l, ln_g, ln_b):
    T = x.shape[0]
    tm = 256
    n_mla, n_swa, n_gdn = mla.shape[1], swa.shape[1], gdn.shape[1]
    assert n_mla % n_swa == 0 and (n_mla + n_swa) % n_gdn == 0
    row = lambda w: pl.BlockSpec((tm, w), lambda i: (i, 0))
    full = lambda shape: pl.BlockSpec(shape, lambda i: (0, 0))
    return pl.pallas_call(
        _out_proj_kernel,
        out_shape=(jax.ShapeDtypeStruct((T, D_MODEL), F32), jax.ShapeDtypeStruct((T, D_MODEL), BF16)),
        grid=(T // tm,),
        in_specs=[row(n_mla), row(n_swa), row(n_gdn), row(D_MODEL),
                  pl.BlockSpec((None, n_mla, D_MODEL), lambda i: (l, 0, 0)),
                  pl.BlockSpec((None, n_swa, D_MODEL), lambda i: (l, n_mla // n_swa, 0)),
                  pl.BlockSpec((None, n_gdn, D_MODEL), lambda i: (l, (n_mla + n_swa) // n_gdn, 0)),
                  full((1, D_MODEL)), full((1, D_MODEL))],
        out_specs=(row(D_MODEL), row(D_MODEL)),
        compiler_params=_cparams(("parallel",)),
        name="out_proj_ln",
    )(mla, swa, gdn, x, w_out, w_out, w_out, ln_g, ln_b)


def _ffn_kernel(xb_ref, xf_ref, wg_ref, wu_ref, wd_ref, lg_ref, lb_ref, of_ref, ob_ref, acc_sc):
    j = pl.program_id(1)

    @pl.when(j == 0)
    def _():
        acc_sc[...] = jnp.zeros_like(acc_sc)

    xb = xb_ref[...]
    h = _silu(_dot(xb, wg_ref[...])) * _dot(xb, wu_ref[...])
    acc_sc[...] += _dot(h.astype(BF16), wd_ref[...])

    @pl.when(j == pl.num_programs(1) - 1)
    def _():
        y = _layer_norm_rows(DN_ALPHA * xf_ref[...] + acc_sc[...], lg_ref[...], lb_ref[...])
        of_ref[...] = y
        ob_ref[...] = y.astype(BF16)


def _ffn_ln(xb, xf, wg, wu, wd, m, ln_g, ln_b):
    T = xb.shape[0]
    tm, tf = 512, 512
    return pl.pallas_call(
        _ffn_kernel,
        out_shape=(jax.ShapeDtypeStruct((T, D_MODEL), F32), jax.ShapeDtypeStruct((T, D_MODEL), BF16)),
        grid=(T // tm, D_FF // tf),
        in_specs=[pl.BlockSpec((tm, D_MODEL), lambda i, j: (i, 0)),
                  pl.BlockSpec((tm, D_MODEL), lambda i, j: (i, 0)),
                  pl.BlockSpec((None, D_MODEL, tf), lambda i, j: (m, 0, j)),
                  pl.BlockSpec((None, D_MODEL, tf), lambda i, j: (m, 0, j)),
                  pl.BlockSpec((None, tf, D_MODEL), lambda i, j: (m, j, 0)),
                  pl.BlockSpec((1, D_MODEL), lambda i, j: (0, 0)),
                  pl.BlockSpec((1, D_MODEL), lambda i, j: (0, 0))],
        out_specs=(pl.BlockSpec((tm, D_MODEL), lambda i, j: (i, 0)),
                   pl.BlockSpec((tm, D_MODEL), lambda i, j: (i, 0))),
        scratch_shapes=[pltpu.VMEM((tm, D_MODEL), F32)],
        compiler_params=_cparams(("parallel", "arbitrary")),
        name="ffn_ln",
    )(xb, xf, wg, wu, wd, ln_g, ln_b)


def _router_kernel(x_ref, w_ref, o_ref):
    logits = jnp.dot(x_ref[...], w_ref[...], preferred_element_type=F32, precision=lax.Precision.HIGHEST)
    lane = lax.broadcasted_iota(jnp.int32, logits.shape, 1).astype(F32)
    logits = jnp.where(lane < N_EXPERTS, logits, -jnp.inf)
    m1 = jnp.max(logits, axis=1, keepdims=True)
    i1 = jnp.min(jnp.where(logits == m1, lane, float(LANES)), axis=1, keepdims=True)
    rest = jnp.where(lane == i1, -jnp.inf, logits)
    m2 = jnp.max(rest, axis=1, keepdims=True)
    i2 = jnp.min(jnp.where(rest == m2, lane, float(LANES)), axis=1, keepdims=True)
    e = jnp.exp(m2 - m1)
    g1 = 1.0 / (1.0 + e)
    g2 = e / (1.0 + e)
    out = jnp.where(lane == 0, i1,
                    jnp.where(lane == 1, i2,
                              jnp.where(lane == 2, g1, jnp.where(lane == 3, g2, 0.0))))
    o_ref[...] = out


def _router(xf, w128):
    T = xf.shape[0]
    tm = 512
    return pl.pallas_call(
        _router_kernel,
        out_shape=jax.ShapeDtypeStruct((T, LANES), F32),
        grid=(T // tm,),
        in_specs=[pl.BlockSpec((tm, D_MODEL), lambda i: (i, 0)),
                  pl.BlockSpec((D_MODEL, LANES), lambda i: (0, 0))],
        out_specs=pl.BlockSpec((tm, LANES), lambda i: (i, 0)),
        compiler_params=_cparams(("parallel",)),
        name="moe_router",
    )(xf, w128)


MOE_TM = 512
MOE_TF = 512


def _moe_kernel(te_ref, nu_ref, tok_ref, dst_ref, x_hbm, wg_ref, wu_ref, wd_ref, o_hbm,
                xg_sc, xb_sc, acc_sc, gsem, ssem, *, n_tokens):
    i = pl.program_id(0)
    j = pl.program_id(1)
    nj = pl.num_programs(1)
    n_used = nu_ref[0]
    active = i < n_used
    slot = lax.rem(i, 2)

    def gather_start(tile, s):
        def issue(r, carry):
            pltpu.make_async_copy(x_hbm.at[tok_ref[tile * MOE_TM + r]], xg_sc.at[s, r], gsem.at[s]).start()
            return carry
        lax.fori_loop(0, MOE_TM, issue, 0, unroll=8)

    def gather_wait(s):
        pltpu.make_async_copy(x_hbm.at[pl.ds(0, MOE_TM)], xg_sc.at[s], gsem.at[s]).wait()

    def scatter_start(tile, s):
        def issue(r, carry):
            pltpu.make_async_copy(acc_sc.at[s, r], o_hbm.at[dst_ref[tile * MOE_TM + r]], ssem.at[s]).start()
            return carry
        lax.fori_loop(0, MOE_TM, issue, 0, unroll=8)

    def scatter_wait(s):
        pltpu.make_async_copy(acc_sc.at[s], o_hbm.at[pl.ds(0, MOE_TM)], ssem.at[s]).wait()

    @pl.when(active & (j == 0))
    def _():
        @pl.when(i == 0)
        def _():
            gather_start(0, 0)

        gather_wait(slot)
        xb_sc[...] = xg_sc[slot].astype(BF16)
        acc_sc[slot] = jnp.zeros((MOE_TM, D_MODEL), F32)

        @pl.when(i == 0)
        def _():
            for s in range(2):
                spill = o_hbm.at[pl.ds(TOP_K * n_tokens + s * MOE_TM, MOE_TM)]
                cp = pltpu.make_async_copy(acc_sc.at[0], spill, ssem.at[0])
                cp.start()
                cp.wait()

        @pl.when(i + 1 < n_used)
        def _():
            gather_start(i + 1, 1 - slot)

    @pl.when(active)
    def _():
        xb = xb_sc[...]
        h = _silu(_dot(xb, wg_ref[...])) * _dot(xb, wu_ref[...])
        acc_sc[slot] += _dot(h.astype(BF16), wd_ref[...])

    @pl.when(active & (j == nj - 1))
    def _():
        scatter_start(i, slot)

        @pl.when(i >= 1)
        def _():
            scatter_wait(1 - slot)

        @pl.when(i == n_used - 1)
        def _():
            scatter_wait(slot)


def _moe_grouped(tile_expert, n_used, row_token, row_dest, xf, wg, wu, wd, m):
    n_tiles = tile_expert.shape[0]
    n_tokens = xf.shape[0]
    nj = D_FF // MOE_TF

    def wcol(i, j, te, nu, tok, dst):
        jj = jnp.where(i < nu[0], j, nj - 1)
        return (m, te[i], 0, jj)

    def wrow(i, j, te, nu, tok, dst):
        jj = jnp.where(i < nu[0], j, nj - 1)
        return (m, te[i], jj, 0)

    grid_spec = pltpu.PrefetchScalarGridSpec(
        num_scalar_prefetch=4,
        grid=(n_tiles, nj),
        in_specs=[pl.BlockSpec(memory_space=pl.ANY),
                  pl.BlockSpec((None, None, D_MODEL, MOE_TF), wcol),
                  pl.BlockSpec((None, None, D_MODEL, MOE_TF), wcol),
                  pl.BlockSpec((None, None, MOE_TF, D_MODEL), wrow)],
        out_specs=pl.BlockSpec(memory_space=pl.ANY),
        scratch_shapes=[pltpu.VMEM((2, MOE_TM, D_MODEL), F32),
                        pltpu.VMEM((MOE_TM, D_MODEL), BF16),
                        pltpu.VMEM((2, MOE_TM, D_MODEL), F32),
                        pltpu.SemaphoreType.DMA((2,)),
                        pltpu.SemaphoreType.DMA((2,))],
    )
    return pl.pallas_call(
        functools.partial(_moe_kernel, n_tokens=n_tokens),
        out_shape=jax.ShapeDtypeStruct((TOP_K * n_tokens + 2 * MOE_TM, D_MODEL), F32),
        grid_spec=grid_spec,
        compiler_params=_cparams(("arbitrary", "arbitrary")),
        name="moe_grouped",
    )(tile_expert, n_used, row_token, row_dest, xf, wg, wu, wd)


def _moe_combine_kernel(o0_ref, o1_ref, rt_ref, x_ref, lg_ref, lb_ref, of_ref, ob_ref):
    rt = rt_ref[...]
    f = rt[:, 2:3] * o0_ref[...] + rt[:, 3:4] * o1_ref[...]
    y = _layer_norm_rows(DN_ALPHA * x_ref[...] + f, lg_ref[...], lb_ref[...])
    of_ref[...] = y
    ob_ref[...] = y.astype(BF16)


def _moe_combine_ln(o_rows, route, xf, ln_g, ln_b):
    T = xf.shape[0]
    tm = 512
    nt = T // tm
    return pl.pallas_call(
        _moe_combine_kernel,
        out_shape=(jax.ShapeDtypeStruct((T, D_MODEL), F32), jax.ShapeDtypeStruct((T, D_MODEL), BF16)),
        grid=(nt,),
        in_specs=[pl.BlockSpec((tm, D_MODEL), lambda i: (i, 0)),
                  pl.BlockSpec((tm, D_MODEL), lambda i: (i + nt, 0)),
                  pl.BlockSpec((tm, LANES), lambda i: (i, 0)),
                  pl.BlockSpec((tm, D_MODEL), lambda i: (i, 0)),
                  pl.BlockSpec((1, D_MODEL), lambda i: (0, 0)),
                  pl.BlockSpec((1, D_MODEL), lambda i: (0, 0))],
        out_specs=(pl.BlockSpec((tm, D_MODEL), lambda i: (i, 0)),
                   pl.BlockSpec((tm, D_MODEL), lambda i: (i, 0))),
        compiler_params=_cparams(("parallel",)),
        name="moe_combine_ln",
    )(o_rows, o_rows, route, xf, ln_g, ln_b)


def _moe_plan(route, T):
    ids = route[:, :TOP_K].astype(jnp.int32)
    e_flat = ids.reshape(-1)
    onehot = (e_flat[:, None] == jnp.arange(N_EXPERTS, dtype=jnp.int32)[None, :]).astype(jnp.int32)
    csum = jnp.cumsum(onehot, axis=0)
    rank = jnp.sum(csum * onehot, axis=1) - 1
    counts = csum[-1]
    tiles_per = (counts + MOE_TM - 1) // MOE_TM
    tile_end = jnp.cumsum(tiles_per)
    tile_start = tile_end - tiles_per
    n_used = tile_end[-1]
    n_tiles = (TOP_K * T) // MOE_TM + N_EXPERTS
    pos = tile_start[e_flat] * MOE_TM + rank
    pair = jnp.arange(TOP_K * T, dtype=jnp.int32)
    token = pair // TOP_K
    slot = pair % TOP_K
    n_rows = n_tiles * MOE_TM
    rows = jnp.arange(n_rows, dtype=jnp.int32)
    spill = TOP_K * T + ((rows // MOE_TM) % 2) * MOE_TM + rows % MOE_TM
    row_token = jnp.zeros((n_rows,), jnp.int32).at[pos].set(token)
    row_dest = spill.at[pos].set(slot * T + token)
    tidx = jnp.minimum(jnp.arange(n_tiles, dtype=jnp.int32), n_used - 1)
    tile_expert = jnp.sum((tidx[:, None] >= tile_end[None, :]).astype(jnp.int32), axis=1)
    tile_expert = jnp.minimum(tile_expert, N_EXPERTS - 1).astype(jnp.int32)
    return tile_expert, n_used.reshape(1).astype(jnp.int32), row_token, row_dest


def _prep_w_in(w_in):
    L = w_in.shape[0]
    z = lambda n: jnp.zeros((L, D_MODEL, n), w_in.dtype)
    o_swa = MLA_IN
    o_gdn = MLA_IN + SWA_IN
    hd = SWA_HEAD_DIM
    kr = w_in[:, :, MLA_Q_RANK + MLA_KV_RANK:MLA_IN]
    half = MLA_ROPE // 2
    kr_sw = jnp.concatenate([kr[:, :, half:], kr[:, :, :half]], axis=-1)
    ba = w_in[:, :, o_gdn + GDN_QKV:o_gdn + GDN_QKV + 2 * GDN_HEADS]
    segs = [
        w_in[:, :, :MLA_Q_RANK],
        w_in[:, :, o_swa:o_swa + SWA_HEADS * hd],
        w_in[:, :, o_gdn + GDN_QKV + 2 * GDN_HEADS:],
        w_in[:, :, o_gdn:o_gdn + GDN_QKV],
        w_in[:, :, MLA_Q_RANK:MLA_Q_RANK + MLA_KV_RANK],
        w_in[:, :, o_swa + SWA_HEADS * hd:o_swa + (SWA_HEADS + SWA_KV_HEADS) * hd],
        w_in[:, :, o_swa + (SWA_HEADS + SWA_KV_HEADS) * hd:o_swa + SWA_IN],
        kr, kr_sw, z(LANES - 2 * MLA_ROPE),
        ba, z(LANES - 2 * GDN_HEADS),
        z(D_PROJ - COL_BA - LANES),
    ]
    return jnp.concatenate(segs, axis=-1).astype(BF16)


def _prep_mla_weights(w_uq, w_ukv):
    L = w_uq.shape[0]
    half = MLA_ROPE // 2
    wq = w_uq.reshape(L, MLA_Q_RANK, MLA_HEADS, MLA_NOPE + MLA_ROPE)
    nope, rope = wq[..., :MLA_NOPE], wq[..., MLA_NOPE:]
    rope_sw = jnp.concatenate([rope[..., half:], rope[..., :half]], axis=-1)
    zpad = jnp.zeros(wq.shape[:3] + (MLA_HP - MLA_NOPE - MLA_ROPE,), wq.dtype)
    wq_pad = jnp.concatenate([nope, rope, zpad], axis=-1).reshape(L, MLA_Q_RANK, MLA_HEADS * MLA_HP)
    wq_sw = jnp.concatenate([jnp.zeros_like(nope), rope_sw, zpad], axis=-1).reshape(L, MLA_Q_RANK, MLA_HEADS * MLA_HP)
    wkv = w_ukv.reshape(L, MLA_KV_RANK, MLA_HEADS, MLA_NOPE + MLA_V)
    knope, vv = wkv[..., :MLA_NOPE], wkv[..., MLA_NOPE:]
    wk_pad = jnp.concatenate([knope, jnp.zeros(knope.shape[:3] + (MLA_HP - MLA_NOPE,), wkv.dtype)], axis=-1)
    wk_pad = wk_pad.reshape(L, MLA_KV_RANK, MLA_HEADS * MLA_HP)
    wv = vv.reshape(L, MLA_KV_RANK, MLA_HEADS * MLA_V)
    return wq_pad.astype(BF16), wq_sw.astype(BF16), wk_pad.astype(BF16), wv.astype(BF16)


def _rope_tables(S):
    half = MLA_ROPE // 2
    pos = jnp.arange(S, dtype=jnp.int32)
    inv_freq = ROPE_BASE ** (-jnp.arange(half, dtype=F32) / half)
    ang = pos.astype(F32)[:, None] * inv_freq[None, :]
    cos, sin = jnp.cos(ang), jnp.sin(ang)
    cos2 = jnp.concatenate([cos, cos], axis=-1)
    sin2 = jnp.concatenate([-sin, sin], axis=-1)
    scale = (MLA_NOPE + MLA_ROPE) ** -0.5
    tail = jnp.zeros((S, MLA_HP - MLA_NOPE - MLA_ROPE), F32)
    cq_head = jnp.concatenate([jnp.ones((S, MLA_NOPE), F32), cos2, tail], axis=-1) * scale
    sq_head = jnp.concatenate([jnp.zeros((S, MLA_NOPE), F32), sin2, tail], axis=-1) * scale
    cqt = jnp.tile(cq_head, (1, MLA_HEADS))
    sqt = jnp.tile(sq_head, (1, MLA_HEADS))
    ckt = jnp.concatenate([cos2, sin2, jnp.zeros((S, LANES - 2 * MLA_ROPE), F32)], axis=-1)
    j = jnp.arange(LANES)[:, None]
    col = jnp.arange(MLA_HEADS * MLA_HP)[None, :]
    place = ((col % MLA_HP) == (MLA_NOPE + (j % MLA_ROPE))) & (j < 2 * MLA_ROPE)
    return cqt, sqt, ckt, place.astype(BF16)


def _pad_lanes(v, offset):
    out = jnp.zeros((1, LANES), F32)
    return lax.dynamic_update_slice(out, v.astype(F32).reshape(1, -1), (0, offset))


def kernel(x, w_in, mla_q_norm, mla_w_uq, mla_kv_norm, mla_w_ukv, swa_sinks, gdn_conv,
           gdn_a_log, gdn_dt_bias, gdn_norm, w_out, ln1_g, ln1_b, ln2_g, ln2_b,
           ffn_w_gate, ffn_w_up, ffn_w_down, moe_router, moe_w_gate, moe_w_up, moe_w_down):
    B, S, D = x.shape
    T = B * S
    xf = x.reshape(T, D)
    xb = xf.astype(BF16)

    w_in_b = _prep_w_in(w_in)
    wq_pad, wq_sw, wk_pad, wv = _prep_mla_weights(mla_w_uq, mla_w_ukv)
    cqt, sqt, ckt, place = _rope_tables(S)
    w_out_b = w_out.astype(BF16)
    ffn_g, ffn_u, ffn_d = ffn_w_gate.astype(BF16), ffn_w_up.astype(BF16), ffn_w_down.astype(BF16)
    moe_g, moe_u, moe_d = moe_w_gate.astype(BF16), moe_w_up.astype(BF16), moe_w_down.astype(BF16)
    router128 = jnp.pad(moe_router.astype(F32), ((0, 0), (0, 0), (0, LANES - N_EXPERTS)))

    for l in range(DEPTH):
        proj = _in_proj(xb, w_in_b, l)
        q, k, v = _mla_prep(proj, mla_q_norm[l].reshape(1, -1), mla_kv_norm[l].reshape(1, -1),
                            wq_pad, wq_sw, wk_pad, wv, l, place, cqt, sqt, ckt, S)
        h_mla = _mla_attn(q, k, v, B, S)
        h_swa = _swa(proj, swa_sinks[l].astype(F32), B, S)
        qkv = _gdn_conv(proj, gdn_conv[l], B, S)
        gates = _gdn_gates(proj, _pad_lanes(gdn_a_log[l], GDN_HEADS), _pad_lanes(gdn_dt_bias[l], GDN_HEADS), B, S)
        h_gdn = _gdn_chunks(qkv, gates, proj, gdn_norm[l].reshape(1, -1), B, S)
        xf, xb = _out_proj_ln(h_mla, h_swa, h_gdn, xf, w_out_b, l,
                              ln1_g[l].reshape(1, -1), ln1_b[l].reshape(1, -1))
        lg, lb = ln2_g[l].reshape(1, -1), ln2_b[l].reshape(1, -1)
        if l % 2 == 0:
            xf, xb = _ffn_ln(xb, xf, ffn_g, ffn_u, ffn_d, l // 2, lg, lb)
        else:
            m = l // 2
            route = _router(xf, router128[m])
            tile_expert, n_used, row_token, row_dest = _moe_plan(route, T)
            o_rows = _moe_grouped(tile_expert, n_used, row_token, row_dest, xf, moe_g, moe_u, moe_d, m)
            xf, xb = _moe_combine_ln(o_rows, route, xf, lg, lb)
    return xf.reshape(B, S, D)
```

```python
import functools
import math

import jax
import jax.numpy as jnp
from jax import lax
from jax.experimental import pallas as pl
from jax.experimental.pallas import tpu as pltpu

F32 = jnp.float32
BF16 = jnp.bfloat16

D_MODEL = 2048
DEPTH = 4
MLA_HEADS = 8
MLA_Q_RANK = 512
MLA_KV_RANK = 256
MLA_NOPE = 64
MLA_ROPE = 32
MLA_V = 64
ROPE_BASE = 10000.0
SWA_HEADS = 8
SWA_KV_HEADS = 2
SWA_GROUP = SWA_HEADS // SWA_KV_HEADS
SWA_HEAD_DIM = 64
WINDOW = 128
GDN_HEADS = 8
GDN_DK = 128
GDN_DV = 128
CONV_WIDTH = 4
GDN_C = 256
D_FF = 7168
N_EXPERTS = 8
TOP_K = 2
DN_ALPHA = float((2 * DEPTH) ** 0.25)
LN_EPS = 1e-5
RMS_EPS = 1e-6

MLA_IN = MLA_Q_RANK + MLA_KV_RANK + MLA_ROPE
SWA_IN = (SWA_HEADS + 2 * SWA_KV_HEADS) * SWA_HEAD_DIM
GDN_QKV = GDN_HEADS * (2 * GDN_DK + GDN_DV)

LANES = 128
NEG_BIG = -1e30
VMEM_LIMIT = 56 * 1024 * 1024

COL_CQ = 0
COL_SWAQ = 512
COL_Z = 1024
COL_GQ = 2048
COL_GK = 3072
COL_GV = 4096
COL_CKV = 5120
COL_SWAK = 5376
COL_SWAV = 5504
COL_KR = 5632
COL_BA = 5760
D_PROJ = 6144

MLA_HP = 128


def _cparams(sem, vmem=VMEM_LIMIT):
    return pltpu.CompilerParams(dimension_semantics=sem, vmem_limit_bytes=vmem)


def _layer_norm_rows(y, g, b):
    mu = jnp.mean(y, axis=-1, keepdims=True)
    d = y - mu
    var = jnp.mean(d * d, axis=-1, keepdims=True)
    return d * lax.rsqrt(var + LN_EPS) * g + b


def _sigmoid(x):
    return 1.0 / (1.0 + jnp.exp(-x))


def _silu(x):
    return x * _sigmoid(x)


def _dot(a, b):
    return jnp.dot(a, b, preferred_element_type=F32)


def _dot_nt(a, b):
    return lax.dot_general(a, b, (((1,), (1,)), ((), ())), preferred_element_type=F32)


def _matmul_kernel(x_ref, w_ref, o_ref):
    o_ref[...] = _dot(x_ref[...], w_ref[...])


def _in_proj(xb, w, l):
    T = xb.shape[0]
    tm, tn = 1024, 512
    return pl.pallas_call(
        _matmul_kernel,
        out_shape=jax.ShapeDtypeStruct((T, D_PROJ), F32),
        grid=(T // tm, D_PROJ // tn),
        in_specs=[pl.BlockSpec((tm, D_MODEL), lambda i, j: (i, 0)),
                  pl.BlockSpec((None, D_MODEL, tn), lambda i, j: (l, 0, j))],
        out_specs=pl.BlockSpec((tm, tn), lambda i, j: (i, j)),
        compiler_params=_cparams(("parallel", "arbitrary")),
        name="in_proj",
    )(xb, w)


def _mla_prep_kernel(cq_ref, ckv_ref, kr_ref, gq_ref, gkv_ref, wq_ref, wqs_ref, wk_ref, wv_ref,
                     e_ref, cqt_ref, sqt_ref, ckt_ref, q_ref, k_ref, v_ref):
    cq = cq_ref[...]
    hq = cq * lax.rsqrt(jnp.mean(cq * cq, axis=-1, keepdims=True) + RMS_EPS) * gq_ref[...]
    hq = hq.astype(BF16)
    q = _dot(hq, wq_ref[...]) * cqt_ref[...] + _dot(hq, wqs_ref[...]) * sqt_ref[...]
    q_ref[...] = q.astype(BF16)
    ckv = ckv_ref[...]
    hk = ckv * lax.rsqrt(jnp.mean(ckv * ckv, axis=-1, keepdims=True) + RMS_EPS) * gkv_ref[...]
    hk = hk.astype(BF16)
    krp = kr_ref[...] * ckt_ref[...]
    krp_hi = krp.astype(BF16)
    krp_lo = (krp - krp_hi.astype(F32)).astype(BF16)
    e = e_ref[...]
    k = _dot(hk, wk_ref[...]) + (_dot(krp_hi, e) + _dot(krp_lo, e))
    for h in range(MLA_HEADS):
        k_ref[h * MLA_HP:(h + 1) * MLA_HP, :] = k[:, h * MLA_HP:(h + 1) * MLA_HP].T.astype(BF16)
    v_ref[...] = _dot(hk, wv_ref[...]).astype(BF16)


def _mla_prep(proj, gq, gkv, wq, wqs, wk, wv, l, e, cqt, sqt, ckt, S):
    T = proj.shape[0]
    tm = 512
    sb = S // tm
    full = lambda shape: pl.BlockSpec(shape, lambda i: (0, 0))
    layer = lambda shape: pl.BlockSpec((None,) + shape, lambda i: (l, 0, 0))
    return pl.pallas_call(
        _mla_prep_kernel,
        out_shape=(jax.ShapeDtypeStruct((T, MLA_HEADS * MLA_HP), BF16),
                   jax.ShapeDtypeStruct((T // S * MLA_HEADS * MLA_HP, S), BF16),
                   jax.ShapeDtypeStruct((T, MLA_HEADS * MLA_V), BF16)),
        grid=(T // tm,),
        in_specs=[pl.BlockSpec((tm, MLA_Q_RANK), lambda i: (i, COL_CQ // MLA_Q_RANK)),
                  pl.BlockSpec((tm, MLA_KV_RANK), lambda i: (i, COL_CKV // MLA_KV_RANK)),
                  pl.BlockSpec((tm, LANES), lambda i: (i, COL_KR // LANES)),
                  full((1, MLA_Q_RANK)), full((1, MLA_KV_RANK)),
                  layer((MLA_Q_RANK, MLA_HEADS * MLA_HP)), layer((MLA_Q_RANK, MLA_HEADS * MLA_HP)),
                  layer((MLA_KV_RANK, MLA_HEADS * MLA_HP)), layer((MLA_KV_RANK, MLA_HEADS * MLA_V)),
                  full((LANES, MLA_HEADS * MLA_HP)),
                  pl.BlockSpec((tm, MLA_HEADS * MLA_HP), lambda i: (i % sb, 0)),
                  pl.BlockSpec((tm, MLA_HEADS * MLA_HP), lambda i: (i % sb, 0)),
                  pl.BlockSpec((tm, LANES), lambda i: (i % sb, 0))],
        out_specs=(pl.BlockSpec((tm, MLA_HEADS * MLA_HP), lambda i: (i, 0)),
                   pl.BlockSpec((MLA_HEADS * MLA_HP, tm), lambda i: (i // sb, i % sb)),
                   pl.BlockSpec((tm, MLA_HEADS * MLA_V), lambda i: (i, 0))),
        compiler_params=_cparams(("parallel",)),
        name="mla_prep",
    )(proj, proj, proj, gq, gkv, wq, wqs, wk, wv, e, cqt, sqt, ckt)


MLA_G = 2


def _mla_attn_kernel(q_ref, k_ref, v_ref, o_ref, acc_sc, m_sc, l_sc, *, tq, tk):
    qi = pl.program_id(2)
    acc_sc[...] = jnp.zeros_like(acc_sc)
    m_sc[...] = jnp.full_like(m_sc, NEG_BIG)
    l_sc[...] = jnp.zeros_like(l_sc)
    ratio = tq // tk

    def step(kj, diag):
        rows = pl.ds(pl.multiple_of(kj * tk, tk), tk)
        for g in range(MLA_G):
            s = _dot(q_ref[:, g * MLA_HP:(g + 1) * MLA_HP], k_ref[g * MLA_HP:(g + 1) * MLA_HP, rows])
            if diag is not None:
                r = lax.broadcasted_iota(jnp.int32, s.shape, 0)
                c = lax.broadcasted_iota(jnp.int32, s.shape, 1)
                s = jnp.where(c + diag * tk <= r, s, NEG_BIG)
            m_prev = m_sc[g]
            m_new = jnp.maximum(m_prev, jnp.max(s, axis=1, keepdims=True))
            alpha = jnp.exp(m_prev - m_new)
            p = jnp.exp(s - m_new[:, :1])
            l_sc[g] = alpha * l_sc[g] + jnp.sum(p, axis=1, keepdims=True)
            pv = _dot(p.astype(BF16), v_ref[rows, (g // 2) * LANES:(g // 2 + 1) * LANES])
            acc_sc[g] = acc_sc[g] * alpha + pv
            m_sc[g] = m_new

    def body(kj, carry):
        step(kj, None)
        return carry

    lax.fori_loop(0, qi * ratio, body, 0)
    for d in range(ratio):
        step(qi * ratio + d, d)
    lane = lax.broadcasted_iota(jnp.int32, (tq, LANES), 1)
    for pr in range(MLA_G // 2):
        o0 = acc_sc[2 * pr] / l_sc[2 * pr]
        o1 = acc_sc[2 * pr + 1] / l_sc[2 * pr + 1]
        o_ref[:, pr * LANES:(pr + 1) * LANES] = jnp.where(lane < MLA_V, o0, o1).astype(o_ref.dtype)


def _mla_attn(q, k, v, B, S):
    T = q.shape[0]
    tq, tk = 512, 512
    nq = S // tq
    groups = MLA_HEADS // MLA_G
    return pl.pallas_call(
        functools.partial(_mla_attn_kernel, tq=tq, tk=tk),
        out_shape=jax.ShapeDtypeStruct((T, MLA_HEADS * MLA_V), BF16),
        grid=(B, groups, nq),
        in_specs=[pl.BlockSpec((tq, MLA_G * MLA_HP), lambda b, h, i: (b * nq + i, h)),
                  pl.BlockSpec((MLA_G * MLA_HP, S), lambda b, h, i: (b * groups + h, 0)),
                  pl.BlockSpec((S, MLA_G * MLA_V), lambda b, h, i: (b, h))],
        out_specs=pl.BlockSpec((tq, MLA_G * MLA_V), lambda b, h, i: (b * nq + i, h)),
        scratch_shapes=[pltpu.VMEM((MLA_G, tq, LANES), F32),
                        pltpu.VMEM((MLA_G, tq, LANES), F32),
                        pltpu.VMEM((MLA_G, tq, LANES), F32)],
        compiler_params=_cparams(("parallel", "parallel", "arbitrary")),
        name="mla_attn",
    )(q, k, v)


def _swa_kernel(sink_ref, q_ref, kc_ref, kp_ref, vc_ref, vp_ref, o_ref):
    n = pl.program_id(1)
    hd = SWA_HEAD_DIM
    qi = lax.broadcasted_iota(jnp.int32, (WINDOW, 2 * WINDOW), 0)
    kj = lax.broadcasted_iota(jnp.int32, (WINDOW, 2 * WINDOW), 1)
    delta = qi + WINDOW - kj
    first_key = jnp.where(n > 0, 0, WINDOW)
    valid = (delta >= 0) & (delta < WINDOW) & (kj >= first_key)
    deltaf = delta.astype(F32)
    for g in range(SWA_KV_HEADS):
        kcat = jnp.concatenate([kp_ref[:, g * hd:(g + 1) * hd], kc_ref[:, g * hd:(g + 1) * hd]], axis=0)
        vcat = jnp.concatenate([vp_ref[:, g * hd:(g + 1) * hd], vc_ref[:, g * hd:(g + 1) * hd]], axis=0)
        kcat = kcat.astype(BF16)
        vcat = vcat.astype(BF16)
        for r in range(SWA_GROUP):
            h = g * SWA_GROUP + r
            slope = 2.0 ** (-8.0 * (h + 1) / SWA_HEADS)
            q = q_ref[:, h * hd:(h + 1) * hd].astype(BF16)
            s = _dot_nt(q, kcat) * (hd ** -0.5) - slope * deltaf
            s = jnp.where(valid, s, NEG_BIG)
            sink = sink_ref[h]
            m = jnp.maximum(jnp.max(s, axis=1, keepdims=True), sink)
            p = jnp.exp(s - m)
            denom = jnp.sum(p, axis=1, keepdims=True) + jnp.exp(sink - m)
            o = _dot(p.astype(BF16), vcat) / denom
            o_ref[:, h * hd:(h + 1) * hd] = o.astype(o_ref.dtype)


def _swa(proj, sinks, B, S):
    T = proj.shape[0]
    nb = S // WINDOW
    cq = COL_SWAQ // (SWA_HEADS * SWA_HEAD_DIM)
    ck = COL_SWAK // LANES
    cv = COL_SWAV // LANES
    cur = lambda c: (lambda b, n: (b * nb + n, c))
    prev = lambda c: (lambda b, n: (b * nb + jnp.maximum(n - 1, 0), c))
    return pl.pallas_call(
        _swa_kernel,
        out_shape=jax.ShapeDtypeStruct((T, SWA_HEADS * SWA_HEAD_DIM), BF16),
        grid=(B, nb),
        in_specs=[pl.BlockSpec(memory_space=pltpu.SMEM),
                  pl.BlockSpec((WINDOW, SWA_HEADS * SWA_HEAD_DIM), cur(cq)),
                  pl.BlockSpec((WINDOW, LANES), cur(ck)),
                  pl.BlockSpec((WINDOW, LANES), prev(ck)),
                  pl.BlockSpec((WINDOW, LANES), cur(cv)),
                  pl.BlockSpec((WINDOW, LANES), prev(cv))],
        out_specs=pl.BlockSpec((WINDOW, SWA_HEADS * SWA_HEAD_DIM), lambda b, n: (b * nb + n, 0)),
        compiler_params=_cparams(("parallel", "arbitrary")),
        name="swa_attn",
    )(sinks, proj, proj, proj, proj, proj)


def _conv_silu(cur, prev8, w, sub):
    acc = cur * w[CONV_WIDTH - 1:CONV_WIDTH, :]
    for d in range(1, CONV_WIDTH):
        rolled = pltpu.roll(cur, d, axis=0)
        head = jnp.where(sub < d, pltpu.roll(prev8, d, axis=0), rolled[:8])
        shifted = jnp.concatenate([head, rolled[8:]], axis=0)
        acc = acc + shifted * w[CONV_WIDTH - 1 - d:CONV_WIDTH - d, :]
    return _silu(acc)


def _l2_normalize(y):
    return y * lax.rsqrt(jnp.sum(y * y, axis=-1, keepdims=True) + RMS_EPS)


def _gdn_chunk_kernel(xq_ref, xk_ref, xv_ref, ba_ref, z_ref, cw_ref, alog_ref, dtb_ref, ng_ref, o_ref,
                      state_sc, tail_sc):
    @pl.when(pl.program_id(1) == 0)
    def _():
        state_sc[...] = jnp.zeros_like(state_sc)
        tail_sc[...] = jnp.zeros_like(tail_sc)

    C = GDN_C
    H = GDN_HEADS
    W = H * GDN_DK
    heads = range(H)
    ri = lax.broadcasted_iota(jnp.int32, (C, C), 0)
    ci = lax.broadcasted_iota(jnp.int32, (C, C), 1)
    incl = ri >= ci
    strict = ri > ci
    merge = jnp.where(strict, ri ^ ci, 2 * C)
    ng = ng_ref[...]
    sub = lax.broadcasted_iota(jnp.int32, (8, LANES), 0)

    ba = ba_ref[...]
    lane = lax.broadcasted_iota(jnp.int32, (C, LANES), 1)
    pos = lax.broadcasted_iota(jnp.int32, (C, LANES), 0)
    t = ba + dtb_ref[...]
    g = -jnp.exp(alog_ref[...]) * (jnp.maximum(t, 0.0) + jnp.log1p(jnp.exp(-jnp.abs(t))))
    step = 1
    while step < C:
        g = g + jnp.where(pos >= step, pltpu.roll(g, step, axis=0), 0.0)
        step *= 2
    gt = jnp.where(lane < H, _sigmoid(ba), g)
    gtt = gt.T

    lmat, a_qk, rhs, q_dec, k_dec_t, c_decay = [], [], [], [], [], []
    for h in heads:
        cols = slice(h * GDN_DK, (h + 1) * GDN_DK)
        q = _l2_normalize(_conv_silu(xq_ref[:, cols], tail_sc[0, :, cols], cw_ref[:, h * GDN_DK:(h + 1) * GDN_DK],
                                     sub)) * (GDN_DK ** -0.5)
        k = _l2_normalize(_conv_silu(xk_ref[:, cols], tail_sc[1, :, cols],
                                     cw_ref[:, W + h * GDN_DK:W + (h + 1) * GDN_DK], sub))
        v = _conv_silu(xv_ref[:, cols], tail_sc[2, :, cols],
                       cw_ref[:, 2 * W + h * GDN_DV:2 * W + (h + 1) * GDN_DV], sub)
        beta = gt[:, h:h + 1]
        gc = gt[:, H + h:H + h + 1]
        gr = gtt[H + h:H + h + 1, :]
        glast = gc[C - 1:C, :]
        decay = jnp.exp(jnp.where(incl, gc - gr, NEG_BIG))
        eg = jnp.exp(gc)
        kb = k * beta
        a1 = _dot_nt(jnp.concatenate([kb, q], axis=0).astype(BF16), k.astype(BF16))
        lmat.append(a1[:C] * jnp.where(strict, decay, 0.0))
        a_qk.append((a1[C:] * decay).astype(BF16))
        rhs.append(jnp.concatenate([v * beta, kb * eg], axis=1))
        q_dec.append((q * eg).astype(BF16))
        k_dec_t.append((k * jnp.exp(glast - gc)).T.astype(BF16))
        c_decay.append(jnp.exp(glast))

    base = merge < 16
    mpow = [jnp.where(base, l, 0.0) for l in lmat]
    rinv = [-m for m in mpow]
    for _ in range(3):
        mb = [m.astype(BF16) for m in mpow]
        mpow = [_dot(m, m) for m in mb]
        rinv = [r + m + _dot(r.astype(BF16), m.astype(BF16)) for r, m in zip(rinv, mpow)]
    s = 16
    while s < C:
        level = (merge >= s) & (merge < 2 * s)
        off = [jnp.where(level, l, 0.0) for l in lmat]
        y = [c + _dot(r.astype(BF16), c.astype(BF16)) for r, c in zip(rinv, off)]
        rinv = [r - yy - _dot(yy.astype(BF16), r.astype(BF16)) for r, yy in zip(rinv, y)]
        s *= 2

    sol = [x + _dot(r.astype(BF16), x.astype(BF16)) for r, x in zip(rinv, rhs)]
    a2 = [_dot(jnp.concatenate([sol[h][:, GDN_DV:].astype(BF16), q_dec[h]], axis=0),
               state_sc[h].astype(BF16)) for h in heads]
    v_new = [(sol[h][:, :GDN_DV] - a2[h][:C]).astype(BF16) for h in heads]
    out = [a2[h][C:] + _dot(a_qk[h], v_new[h]) for h in heads]
    for h in heads:
        state_sc[h] = state_sc[h] * c_decay[h] + _dot(k_dec_t[h], v_new[h])
    for h in heads:
        cols = slice(h * GDN_DV, (h + 1) * GDN_DV)
        o = out[h]
        on = o * lax.rsqrt(jnp.mean(o * o, axis=-1, keepdims=True) + RMS_EPS) * ng
        o_ref[:, cols] = (on * _silu(z_ref[:, cols])).astype(o_ref.dtype)
    tail_sc[0] = xq_ref[C - 8:C, :]
    tail_sc[1] = xk_ref[C - 8:C, :]
    tail_sc[2] = xv_ref[C - 8:C, :]


def _gdn(proj, conv_w, alog128, dtb128, norm_g, B, S):
    T = proj.shape[0]
    R = GDN_C
    nr = S // R
    W = GDN_HEADS * GDN_DK
    row = lambda c: (lambda b, r: (b * nr + r, c))
    full = lambda shape: pl.BlockSpec(shape, lambda b, r: (0, 0))
    return pl.pallas_call(
        _gdn_chunk_kernel,
        out_shape=jax.ShapeDtypeStruct((T, GDN_HEADS * GDN_DV), BF16),
        grid=(B, nr),
        in_specs=[pl.BlockSpec((R, W), row(COL_GQ // W)),
                  pl.BlockSpec((R, W), row(COL_GK // W)),
                  pl.BlockSpec((R, W), row(COL_GV // W)),
                  pl.BlockSpec((R, LANES), row(COL_BA // LANES)),
                  pl.BlockSpec((R, W), row(COL_Z // W)),
                  full((CONV_WIDTH, GDN_QKV)), full((1, LANES)), full((1, LANES)), full((1, GDN_DV))],
        out_specs=pl.BlockSpec((R, W), row(0)),
        scratch_shapes=[pltpu.VMEM((GDN_HEADS, GDN_DK, GDN_DV), F32),
                        pltpu.VMEM((3, 8, W), F32)],
        compiler_params=_cparams(("parallel", "arbitrary")),
        name="gdn",
    )(proj, proj, proj, proj, proj, conv_w, alog128, dtb128, norm_g)


def _out_proj_kernel(a_ref, s_ref, g_ref, x_ref, wa_ref, ws_ref, wg_ref, lg_ref, lb_ref, of_ref, ob_ref):
    y = _dot(a_ref[...], wa_ref[...]) + _dot(s_ref[...], ws_ref[...]) + _dot(g_ref[...], wg_ref[...])
    y = _layer_norm_rows(DN_ALPHA * x_ref[...] + y, lg_ref[...], lb_ref[...])
    of_ref[...] = y
    ob_ref[...] = y.astype(BF16)


def _out_proj_ln(mla, swa, gdn, x, w_out, l, ln_g, ln_b):
    T = x.shape[0]
    tm = 256
    n_mla, n_swa, n_gdn = mla.shape[1], swa.shape[1], gdn.shape[1]
    assert n_mla % n_swa == 0 and (n_mla + n_swa) % n_gdn == 0
    row = lambda w: pl.BlockSpec((tm, w), lambda i: (i, 0))
    full = lambda shape: pl.BlockSpec(shape, lambda i: (0, 0))
    return pl.pallas_call(
        _out_proj_kernel,
        out_shape=(jax.ShapeDtypeStruct((T, D_MODEL), F32), jax.ShapeDtypeStruct((T, D_MODEL), BF16)),
        grid=(T // tm,),
        in_specs=[row(n_mla), row(n_swa), row(n_gdn), row(D_MODEL),
                  pl.BlockSpec((None, n_mla, D_MODEL), lambda i: (l, 0, 0)),
                  pl.BlockSpec((None, n_swa, D_MODEL), lambda i: (l, n_mla // n_swa, 0)),
                  pl.BlockSpec((None, n_gdn, D_MODEL), lambda i: (l, (n_mla + n_swa) // n_gdn, 0)),
                  full((1, D_MODEL)), full((1, D_MODEL))],
        out_specs=(row(D_MODEL), row(D_MODEL)),
        compiler_params=_cparams(("parallel",)),
        name="out_proj_ln",
    )(mla, swa, gdn, x, w_out, w_out, w_out, ln_g, ln_b)


def _ffn_kernel(xb_ref, xf_ref, wg_ref, wu_ref, wd_ref, lg_ref, lb_ref, of_ref, ob_ref, acc_sc):
    j = pl.program_id(1)

    @pl.when(j == 0)
    def _():
        acc_sc[...] = jnp.zeros_like(acc_sc)

    xb = xb_ref[...]
    h = _silu(_dot(xb, wg_ref[...])) * _dot(xb, wu_ref[...])
    acc_sc[...] += _dot(h.astype(BF16), wd_ref[...])

    @pl.when(j == pl.num_programs(1) - 1)
    def _():
        y = _layer_norm_rows(DN_ALPHA * xf_ref[...] + acc_sc[...], lg_ref[...], lb_ref[...])
        of_ref[...] = y
        ob_ref[...] = y.astype(BF16)


def _ffn_ln(xb, xf, wg, wu, wd, m, ln_g, ln_b):
    T = xb.shape[0]
    tm, tf = 512, 512
    return pl.pallas_call(
        _ffn_kernel,
        out_shape=(jax.ShapeDtypeStruct((T, D_MODEL), F32), jax.ShapeDtypeStruct((T, D_MODEL), BF16)),
        grid=(T // tm, D_FF // tf),
        in_specs=[pl.BlockSpec((tm, D_MODEL), lambda i, j: (i, 0)),
                  pl.BlockSpec((tm, D_MODEL), lambda i, j: (i, 0)),
                  pl.BlockSpec((None, D_MODEL, tf), lambda i, j: (m, 0, j)),
                  pl.BlockSpec((None, D_MODEL, tf), lambda i, j: (m, 0, j)),
                  pl.BlockSpec((None, tf, D_MODEL), lambda i, j: (m, j, 0)),
                  pl.BlockSpec((1, D_MODEL), lambda i, j: (0, 0)),
                  pl.BlockSpec((1, D_MODEL), lambda i, j: (0, 0))],
        out_specs=(pl.BlockSpec((tm, D_MODEL), lambda i, j: (i, 0)),
                   pl.BlockSpec((tm, D_MODEL), lambda i, j: (i, 0))),
        scratch_shapes=[pltpu.VMEM((tm, D_MODEL), F32)],
        compiler_params=_cparams(("parallel", "arbitrary")),
        name="ffn_ln",
    )(xb, xf, wg, wu, wd, ln_g, ln_b)


def _router_kernel(x_ref, w_ref, o_ref):
    logits = jnp.dot(x_ref[...], w_ref[...], preferred_element_type=F32, precision=lax.Precision.HIGHEST)
    lane = lax.broadcasted_iota(jnp.int32, logits.shape, 1).astype(F32)
    logits = jnp.where(lane < N_EXPERTS, logits, -jnp.inf)
    m1 = jnp.max(logits, axis=1, keepdims=True)
    i1 = jnp.min(jnp.where(logits == m1, lane, float(LANES)), axis=1, keepdims=True)
    rest = jnp.where(lane == i1, -jnp.inf, logits)
    m2 = jnp.max(rest, axis=1, keepdims=True)
    i2 = jnp.min(jnp.where(rest == m2, lane, float(LANES)), axis=1, keepdims=True)
    e = jnp.exp(m2 - m1)
    g1 = 1.0 / (1.0 + e)
    g2 = e / (1.0 + e)
    out = jnp.where(lane == 0, i1,
                    jnp.where(lane == 1, i2,
                              jnp.where(lane == 2, g1, jnp.where(lane == 3, g2, 0.0))))
    o_ref[...] = out


def _router(xf, w128):
    T = xf.shape[0]
    tm = 512
    return pl.pallas_call(
        _router_kernel,
        out_shape=jax.ShapeDtypeStruct((T, LANES), F32),
        grid=(T // tm,),
        in_specs=[pl.BlockSpec((tm, D_MODEL), lambda i: (i, 0)),
                  pl.BlockSpec((D_MODEL, LANES), lambda i: (0, 0))],
        out_specs=pl.BlockSpec((tm, LANES), lambda i: (i, 0)),
        compiler_params=_cparams(("parallel",)),
        name="moe_router",
    )(xf, w128)


MOE_TM = 512
MOE_TF = 1024


def _moe_kernel(te_ref, nu_ref, tok_ref, dst_ref, x_hbm, wg_ref, wu_ref, wd_ref, o_hbm,
                xg_sc, xb_sc, acc_sc, gsem, ssem, *, n_tokens):
    i = pl.program_id(0)
    j = pl.program_id(1)
    nj = pl.num_programs(1)
    n_used = nu_ref[0]
    active = i < n_used
    slot = lax.rem(i, 2)

    def gather_start(tile, s):
        def issue(r, carry):
            pltpu.make_async_copy(x_hbm.at[tok_ref[tile * MOE_TM + r]], xg_sc.at[s, r], gsem.at[s]).start()
            return carry
        lax.fori_loop(0, MOE_TM, issue, 0, unroll=8)

    def gather_wait(s):
        pltpu.make_async_copy(x_hbm.at[pl.ds(0, MOE_TM)], xg_sc.at[s], gsem.at[s]).wait()

    def scatter_start(tile, s):
        def issue(r, carry):
            pltpu.make_async_copy(acc_sc.at[s, r], o_hbm.at[dst_ref[tile * MOE_TM + r]], ssem.at[s]).start()
            return carry
        lax.fori_loop(0, MOE_TM, issue, 0, unroll=8)

    def scatter_wait(s):
        pltpu.make_async_copy(acc_sc.at[s], o_hbm.at[pl.ds(0, MOE_TM)], ssem.at[s]).wait()

    @pl.when(active & (j == 0))
    def _():
        @pl.when(i == 0)
        def _():
            gather_start(0, 0)

        gather_wait(slot)
        xb_sc[...] = xg_sc[slot].astype(BF16)
        acc_sc[slot] = jnp.zeros((MOE_TM, D_MODEL), F32)

        @pl.when(i == 0)
        def _():
            for s in range(2):
                spill = o_hbm.at[pl.ds(TOP_K * n_tokens + s * MOE_TM, MOE_TM)]
                cp = pltpu.make_async_copy(acc_sc.at[0], spill, ssem.at[0])
                cp.start()
                cp.wait()

        @pl.when(i + 1 < n_used)
        def _():
            gather_start(i + 1, 1 - slot)

    @pl.when(active)
    def _():
        xb = xb_sc[...]
        h = _silu(_dot(xb, wg_ref[...])) * _dot(xb, wu_ref[...])
        acc_sc[slot] += _dot(h.astype(BF16), wd_ref[...])

    @pl.when(active & (j == nj - 1))
    def _():
        scatter_start(i, slot)

        @pl.when(i >= 1)
        def _():
            scatter_wait(1 - slot)

        @pl.when(i == n_used - 1)
        def _():
            scatter_wait(slot)


def _moe_grouped(tile_expert, n_used, row_token, row_dest, xf, wg, wu, wd, m):
    n_tiles = tile_expert.shape[0]
    n_tokens = xf.shape[0]
    nj = D_FF // MOE_TF

    def wcol(i, j, te, nu, tok, dst):
        jj = jnp.where(i < nu[0], j, nj - 1)
        return (m, te[i], 0, jj)

    def wrow(i, j, te, nu, tok, dst):
        jj = jnp.where(i < nu[0], j, nj - 1)
        return (m, te[i], jj, 0)

    grid_spec = pltpu.PrefetchScalarGridSpec(
        num_scalar_prefetch=4,
        grid=(n_tiles, nj),
        in_specs=[pl.BlockSpec(memory_space=pl.ANY),
                  pl.BlockSpec((None, None, D_MODEL, MOE_TF), wcol),
                  pl.BlockSpec((None, None, D_MODEL, MOE_TF), wcol),
                  pl.BlockSpec((None, None, MOE_TF, D_MODEL), wrow)],
        out_specs=pl.BlockSpec(memory_space=pl.ANY),
        scratch_shapes=[pltpu.VMEM((2, MOE_TM, D_MODEL), F32),
                        pltpu.VMEM((MOE_TM, D_MODEL), BF16),
                        pltpu.VMEM((2, MOE_TM, D_MODEL), F32),
                        pltpu.SemaphoreType.DMA((2,)),
                        pltpu.SemaphoreType.DMA((2,))],
    )
    return pl.pallas_call(
        functools.partial(_moe_kernel, n_tokens=n_tokens),
        out_shape=jax.ShapeDtypeStruct((TOP_K * n_tokens + 2 * MOE_TM, D_MODEL), F32),
        grid_spec=grid_spec,
        compiler_params=_cparams(("arbitrary", "arbitrary")),
        name="moe_grouped",
    )(tile_expert, n_used, row_token, row_dest, xf, wg, wu, wd)


def _moe_combine_kernel(o0_ref, o1_ref, rt_ref, x_ref, lg_ref, lb_ref, of_ref, ob_ref):
    rt = rt_ref[...]
    f = rt[:, 2:3] * o0_ref[...] + rt[:, 3:4] * o1_ref[...]
    y = _layer_norm_rows(DN_ALPHA * x_ref[...] + f, lg_ref[...], lb_ref[...])
    of_ref[...] = y
    ob_ref[...] = y.astype(BF16)


def _moe_combine_ln(o_rows, route, xf, ln_g, ln_b):
    T = xf.shape[0]
    tm = 512
    nt = T // tm
    return pl.pallas_call(
        _moe_combine_kernel,
        out_shape=(jax.ShapeDtypeStruct((T, D_MODEL), F32), jax.ShapeDtypeStruct((T, D_MODEL), BF16)),
        grid=(nt,),
        in_specs=[pl.BlockSpec((tm, D_MODEL), lambda i: (i, 0)),
                  pl.BlockSpec((tm, D_MODEL), lambda i: (i + nt, 0)),
                  pl.BlockSpec((tm, LANES), lambda i: (i, 0)),
                  pl.BlockSpec((tm, D_MODEL), lambda i: (i, 0)),
                  pl.BlockSpec((1, D_MODEL), lambda i: (0, 0)),
                  pl.BlockSpec((1, D_MODEL), lambda i: (0, 0))],
        out_specs=(pl.BlockSpec((tm, D_MODEL), lambda i: (i, 0)),
                   pl.BlockSpec((tm, D_MODEL), lambda i: (i, 0))),
        compiler_params=_cparams(("parallel",)),
        name="moe_combine_ln",
    )(o_rows, o_rows, route, xf, ln_g, ln_b)


def _moe_plan(route, T):
    n_pairs = TOP_K * T
    ids = route[:, :TOP_K].astype(jnp.int32)
    e_flat = ids.reshape(-1)
    pair = jnp.arange(n_pairs, dtype=jnp.int32)
    order = jnp.sort(e_flat * n_pairs + pair) % n_pairs
    onehot = (e_flat[:, None] == jnp.arange(N_EXPERTS, dtype=jnp.int32)[None, :]).astype(jnp.int32)
    counts = jnp.sum(onehot, axis=0)
    first = jnp.cumsum(counts) - counts
    tiles_per = (counts + MOE_TM - 1) // MOE_TM
    tile_end = jnp.cumsum(tiles_per)
    tile_start = tile_end - tiles_per
    n_used = tile_end[-1]
    n_tiles = n_pairs // MOE_TM + N_EXPERTS
    tiles = jnp.arange(n_tiles, dtype=jnp.int32)
    tidx = jnp.minimum(tiles, n_used - 1)
    tile_expert = jnp.sum((tidx[:, None] >= tile_end[None, :]).astype(jnp.int32), axis=1)
    tile_expert = jnp.minimum(tile_expert, N_EXPERTS - 1).astype(jnp.int32)
    r_in = jnp.arange(MOE_TM, dtype=jnp.int32)[None, :]
    k = (tiles - tile_start[tile_expert])[:, None] * MOE_TM + r_in
    valid = (k < counts[tile_expert][:, None]) & (tiles < n_used)[:, None]
    src = jnp.clip(first[tile_expert][:, None] + k, 0, n_pairs - 1)
    p = order[src]
    spill = n_pairs + (tiles % 2)[:, None] * MOE_TM + r_in
    row_token = jnp.where(valid, p // TOP_K, 0).reshape(-1)
    row_dest = jnp.where(valid, (p % TOP_K) * T + p // TOP_K, spill).reshape(-1)
    return tile_expert, n_used.reshape(1).astype(jnp.int32), row_token, row_dest


def _prep_w_in(w_in):
    L = w_in.shape[0]
    z = lambda n: jnp.zeros((L, D_MODEL, n), w_in.dtype)
    o_swa = MLA_IN
    o_gdn = MLA_IN + SWA_IN
    hd = SWA_HEAD_DIM
    kr = w_in[:, :, MLA_Q_RANK + MLA_KV_RANK:MLA_IN]
    half = MLA_ROPE // 2
    kr_sw = jnp.concatenate([kr[:, :, half:], kr[:, :, :half]], axis=-1)
    ba = w_in[:, :, o_gdn + GDN_QKV:o_gdn + GDN_QKV + 2 * GDN_HEADS]
    segs = [
        w_in[:, :, :MLA_Q_RANK],
        w_in[:, :, o_swa:o_swa + SWA_HEADS * hd],
        w_in[:, :, o_gdn + GDN_QKV + 2 * GDN_HEADS:],
        w_in[:, :, o_gdn:o_gdn + GDN_QKV],
        w_in[:, :, MLA_Q_RANK:MLA_Q_RANK + MLA_KV_RANK],
        w_in[:, :, o_swa + SWA_HEADS * hd:o_swa + (SWA_HEADS + SWA_KV_HEADS) * hd],
        w_in[:, :, o_swa + (SWA_HEADS + SWA_KV_HEADS) * hd:o_swa + SWA_IN],
        kr, kr_sw, z(LANES - 2 * MLA_ROPE),
        ba, z(LANES - 2 * GDN_HEADS),
        z(D_PROJ - COL_BA - LANES),
    ]
    return jnp.concatenate([seg.astype(BF16) for seg in segs], axis=-1)


def _prep_mla_weights(w_uq, w_ukv):
    L = w_uq.shape[0]
    half = MLA_ROPE // 2
    wq = w_uq.reshape(L, MLA_Q_RANK, MLA_HEADS, MLA_NOPE + MLA_ROPE)
    nope, rope = wq[..., :MLA_NOPE], wq[..., MLA_NOPE:]
    rope_sw = jnp.concatenate([rope[..., half:], rope[..., :half]], axis=-1)
    zpad = jnp.zeros(wq.shape[:3] + (MLA_HP - MLA_NOPE - MLA_ROPE,), wq.dtype)
    wq_pad = jnp.concatenate([nope, rope, zpad], axis=-1).reshape(L, MLA_Q_RANK, MLA_HEADS * MLA_HP)
    wq_sw = jnp.concatenate([jnp.zeros_like(nope), rope_sw, zpad], axis=-1).reshape(L, MLA_Q_RANK, MLA_HEADS * MLA_HP)
    wkv = w_ukv.reshape(L, MLA_KV_RANK, MLA_HEADS, MLA_NOPE + MLA_V)
    knope, vv = wkv[..., :MLA_NOPE], wkv[..., MLA_NOPE:]
    wk_pad = jnp.concatenate([knope, jnp.zeros(knope.shape[:3] + (MLA_HP - MLA_NOPE,), wkv.dtype)], axis=-1)
    wk_pad = wk_pad.reshape(L, MLA_KV_RANK, MLA_HEADS * MLA_HP)
    wv = vv.reshape(L, MLA_KV_RANK, MLA_HEADS * MLA_V)
    return wq_pad.astype(BF16), wq_sw.astype(BF16), wk_pad.astype(BF16), wv.astype(BF16)


def _rope_tables(S):
    half = MLA_ROPE // 2
    pos = jnp.arange(S, dtype=jnp.int32)
    inv_freq = ROPE_BASE ** (-jnp.arange(half, dtype=F32) / half)
    ang = pos.astype(F32)[:, None] * inv_freq[None, :]
    cos, sin = jnp.cos(ang), jnp.sin(ang)
    cos2 = jnp.concatenate([cos, cos], axis=-1)
    sin2 = jnp.concatenate([-sin, sin], axis=-1)
    scale = (MLA_NOPE + MLA_ROPE) ** -0.5
    tail = jnp.zeros((S, MLA_HP - MLA_NOPE - MLA_ROPE), F32)
    cq_head = jnp.concatenate([jnp.ones((S, MLA_NOPE), F32), cos2, tail], axis=-1) * scale
    sq_head = jnp.concatenate([jnp.zeros((S, MLA_NOPE), F32), sin2, tail], axis=-1) * scale
    cqt = jnp.tile(cq_head, (1, MLA_HEADS))
    sqt = jnp.tile(sq_head, (1, MLA_HEADS))
    ckt = jnp.concatenate([cos2, sin2, jnp.zeros((S, LANES - 2 * MLA_ROPE), F32)], axis=-1)
    j = jnp.arange(LANES)[:, None]
    col = jnp.arange(MLA_HEADS * MLA_HP)[None, :]
    place = ((col % MLA_HP) == (MLA_NOPE + (j % MLA_ROPE))) & (j < 2 * MLA_ROPE)
    return cqt, sqt, ckt, place.astype(BF16)


def _pad_lanes(v, offset):
    out = jnp.zeros((1, LANES), F32)
    return lax.dynamic_update_slice(out, v.astype(F32).reshape(1, -1), (0, offset))


def kernel(x, w_in, mla_q_norm, mla_w_uq, mla_kv_norm, mla_w_ukv, swa_sinks, gdn_conv,
           gdn_a_log, gdn_dt_bias, gdn_norm, w_out, ln1_g, ln1_b, ln2_g, ln2_b,
           ffn_w_gate, ffn_w_up, ffn_w_down, moe_router, moe_w_gate, moe_w_up, moe_w_down):
    B, S, D = x.shape
    T = B * S
    xf = x.reshape(T, D)
    xb = xf.astype(BF16)

    w_in_b = _prep_w_in(w_in)
    wq_pad, wq_sw, wk_pad, wv = _prep_mla_weights(mla_w_uq, mla_w_ukv)
    cqt, sqt, ckt, place = _rope_tables(S)
    w_out_b = w_out.astype(BF16)
    ffn_g, ffn_u, ffn_d = ffn_w_gate.astype(BF16), ffn_w_up.astype(BF16), ffn_w_down.astype(BF16)
    moe_g, moe_u, moe_d = moe_w_gate.astype(BF16), moe_w_up.astype(BF16), moe_w_down.astype(BF16)
    router128 = jnp.pad(moe_router.astype(F32), ((0, 0), (0, 0), (0, LANES - N_EXPERTS)))

    for l in range(DEPTH):
        proj = _in_proj(xb, w_in_b, l)
        q, k, v = _mla_prep(proj, mla_q_norm[l].reshape(1, -1), mla_kv_norm[l].reshape(1, -1),
                            wq_pad, wq_sw, wk_pad, wv, l, place, cqt, sqt, ckt, S)
        h_mla = _mla_attn(q, k, v, B, S)
        h_swa = _swa(proj, swa_sinks[l].astype(F32), B, S)
        h_gdn = _gdn(proj, gdn_conv[l], _pad_lanes(gdn_a_log[l], GDN_HEADS), _pad_lanes(gdn_dt_bias[l], GDN_HEADS),
                     gdn_norm[l].reshape(1, -1), B, S)
        xf, xb = _out_proj_ln(h_mla, h_swa, h_gdn, xf, w_out_b, l,
                              ln1_g[l].reshape(1, -1), ln1_b[l].reshape(1, -1))
        lg, lb = ln2_g[l].reshape(1, -1), ln2_b[l].reshape(1, -1)
        if l % 2 == 0:
            xf, xb = _ffn_ln(xb, xf, ffn_g, ffn_u, ffn_d, l // 2, lg, lb)
        else:
            m = l // 2
            route = _router(xf, router128[m])
            tile_expert, n_used, row_token, row_dest = _moe_plan(route, T)
            o_rows = _moe_grouped(tile_expert, n_used, row_token, row_dest, xf, moe_g, moe_u, moe_d, m)
            xf, xb = _moe_combine_ln(o_rows, route, xf, lg, lb)
    return xf.reshape(B, S, D)
```

```python
import functools
import math

import jax
import jax.numpy as jnp
from jax import lax
from jax.experimental import pallas as pl
from jax.experimental.pallas import tpu as pltpu

F32 = jnp.float32
BF16 = jnp.bfloat16

D_MODEL = 2048
DEPTH = 4
MLA_HEADS = 8
MLA_Q_RANK = 512
MLA_KV_RANK = 256
MLA_NOPE = 64
MLA_ROPE = 32
MLA_V = 64
ROPE_BASE = 10000.0
SWA_HEADS = 8
SWA_KV_HEADS = 2
SWA_GROUP = SWA_HEADS // SWA_KV_HEADS
SWA_HEAD_DIM = 64
WINDOW = 128
GDN_HEADS = 8
GDN_DK = 128
GDN_DV = 128
CONV_WIDTH = 4
GDN_C = 256
GDN_HEAD_GROUP = 4
D_FF = 7168
N_EXPERTS = 8
TOP_K = 2
DN_ALPHA = float((2 * DEPTH) ** 0.25)
LN_EPS = 1e-5
RMS_EPS = 1e-6

MLA_IN = MLA_Q_RANK + MLA_KV_RANK + MLA_ROPE
SWA_IN = (SWA_HEADS + 2 * SWA_KV_HEADS) * SWA_HEAD_DIM
GDN_QKV = GDN_HEADS * (2 * GDN_DK + GDN_DV)

LANES = 128
NEG_BIG = -1e30
VMEM_LIMIT = 56 * 1024 * 1024

COL_CQ = 0
COL_SWAQ = 512
COL_Z = 1024
COL_GQ = 2048
COL_GK = 3072
COL_GV = 4096
COL_CKV = 5120
COL_SWAK = 5376
COL_SWAV = 5504
COL_KR = 5632
COL_BA = 5760
D_PROJ = 6144

MLA_HP = 128


def _cparams(sem, vmem=VMEM_LIMIT):
    return pltpu.CompilerParams(dimension_semantics=sem, vmem_limit_bytes=vmem)


def _layer_norm_rows(y, g, b):
    mu = jnp.mean(y, axis=-1, keepdims=True)
    d = y - mu
    var = jnp.mean(d * d, axis=-1, keepdims=True)
    return d * lax.rsqrt(var + LN_EPS) * g + b


def _sigmoid(x):
    return 1.0 / (1.0 + jnp.exp(-x))


def _silu(x):
    return x * _sigmoid(x)


def _dot(a, b):
    return jnp.dot(a, b, preferred_element_type=F32)


def _dot_nt(a, b):
    return lax.dot_general(a, b, (((1,), (1,)), ((), ())), preferred_element_type=F32)


def _matmul_kernel(x_ref, w_ref, o_ref):
    o_ref[...] = _dot(x_ref[...], w_ref[...])


def _in_proj(xb, w, l):
    T = xb.shape[0]
    tm, tn = 1024, 512
    return pl.pallas_call(
        _matmul_kernel,
        out_shape=jax.ShapeDtypeStruct((T, D_PROJ), F32),
        grid=(T // tm, D_PROJ // tn),
        in_specs=[pl.BlockSpec((tm, D_MODEL), lambda i, j: (i, 0)),
                  pl.BlockSpec((None, D_MODEL, tn), lambda i, j: (l, 0, j))],
        out_specs=pl.BlockSpec((tm, tn), lambda i, j: (i, j)),
        compiler_params=_cparams(("parallel", "arbitrary")),
        name="in_proj",
    )(xb, w)


def _mla_prep_kernel(cq_ref, ckv_ref, kr_ref, gq_ref, gkv_ref, wq_ref, wqs_ref, wk_ref, wv_ref,
                     e_ref, cqt_ref, sqt_ref, ckt_ref, q_ref, k_ref, v_ref):
    cq = cq_ref[...]
    hq = cq * lax.rsqrt(jnp.mean(cq * cq, axis=-1, keepdims=True) + RMS_EPS) * gq_ref[...]
    hq = hq.astype(BF16)
    q = _dot(hq, wq_ref[...]) * cqt_ref[...] + _dot(hq, wqs_ref[...]) * sqt_ref[...]
    q_ref[...] = q.astype(BF16)
    ckv = ckv_ref[...]
    hk = ckv * lax.rsqrt(jnp.mean(ckv * ckv, axis=-1, keepdims=True) + RMS_EPS) * gkv_ref[...]
    hk = hk.astype(BF16)
    krp = kr_ref[...] * ckt_ref[...]
    krp_hi = krp.astype(BF16)
    krp_lo = (krp - krp_hi.astype(F32)).astype(BF16)
    e = e_ref[...]
    k = _dot(hk, wk_ref[...]) + (_dot(krp_hi, e) + _dot(krp_lo, e))
    for h in range(MLA_HEADS):
        k_ref[h * MLA_HP:(h + 1) * MLA_HP, :] = k[:, h * MLA_HP:(h + 1) * MLA_HP].T.astype(BF16)
    v_ref[...] = _dot(hk, wv_ref[...]).astype(BF16)


def _mla_prep(proj, gq, gkv, wq, wqs, wk, wv, l, e, cqt, sqt, ckt, S):
    T = proj.shape[0]
    tm = 512
    sb = S // tm
    full = lambda shape: pl.BlockSpec(shape, lambda i: (0, 0))
    layer = lambda shape: pl.BlockSpec((None,) + shape, lambda i: (l, 0, 0))
    return pl.pallas_call(
        _mla_prep_kernel,
        out_shape=(jax.ShapeDtypeStruct((T, MLA_HEADS * MLA_HP), BF16),
                   jax.ShapeDtypeStruct((T // S * MLA_HEADS * MLA_HP, S), BF16),
                   jax.ShapeDtypeStruct((T, MLA_HEADS * MLA_V), BF16)),
        grid=(T // tm,),
        in_specs=[pl.BlockSpec((tm, MLA_Q_RANK), lambda i: (i, COL_CQ // MLA_Q_RANK)),
                  pl.BlockSpec((tm, MLA_KV_RANK), lambda i: (i, COL_CKV // MLA_KV_RANK)),
                  pl.BlockSpec((tm, LANES), lambda i: (i, COL_KR // LANES)),
                  full((1, MLA_Q_RANK)), full((1, MLA_KV_RANK)),
                  layer((MLA_Q_RANK, MLA_HEADS * MLA_HP)), layer((MLA_Q_RANK, MLA_HEADS * MLA_HP)),
                  layer((MLA_KV_RANK, MLA_HEADS * MLA_HP)), layer((MLA_KV_RANK, MLA_HEADS * MLA_V)),
                  full((LANES, MLA_HEADS * MLA_HP)),
                  pl.BlockSpec((tm, MLA_HEADS * MLA_HP), lambda i: (i % sb, 0)),
                  pl.BlockSpec((tm, MLA_HEADS * MLA_HP), lambda i: (i % sb, 0)),
                  pl.BlockSpec((tm, LANES), lambda i: (i % sb, 0))],
        out_specs=(pl.BlockSpec((tm, MLA_HEADS * MLA_HP), lambda i: (i, 0)),
                   pl.BlockSpec((MLA_HEADS * MLA_HP, tm), lambda i: (i // sb, i % sb)),
                   pl.BlockSpec((tm, MLA_HEADS * MLA_V), lambda i: (i, 0))),
        compiler_params=_cparams(("parallel",)),
        name="mla_prep",
    )(proj, proj, proj, gq, gkv, wq, wqs, wk, wv, e, cqt, sqt, ckt)


MLA_G = 4


def _mla_attn_kernel(q_ref, k_ref, v_ref, o_ref, acc_sc, m_sc, l_sc, *, tq, tk):
    qi = pl.program_id(2)
    acc_sc[...] = jnp.zeros_like(acc_sc)
    m_sc[...] = jnp.full_like(m_sc, NEG_BIG)
    l_sc[...] = jnp.zeros_like(l_sc)
    ratio = tq // tk

    def step(kj, diag):
        rows = pl.ds(pl.multiple_of(kj * tk, tk), tk)
        for g in range(MLA_G):
            s = _dot(q_ref[:, g * MLA_HP:(g + 1) * MLA_HP], k_ref[g * MLA_HP:(g + 1) * MLA_HP, rows])
            if diag is not None:
                r = lax.broadcasted_iota(jnp.int32, s.shape, 0)
                c = lax.broadcasted_iota(jnp.int32, s.shape, 1)
                s = jnp.where(c + diag * tk <= r, s, NEG_BIG)
            m_prev = m_sc[g]
            m_new = jnp.maximum(m_prev, jnp.max(s, axis=1, keepdims=True))
            alpha = jnp.exp(m_prev - m_new)
            p = jnp.exp(s - m_new[:, :1])
            l_sc[g] = alpha * l_sc[g] + jnp.sum(p, axis=1, keepdims=True)
            pv = _dot(p.astype(BF16), v_ref[rows, (g // 2) * LANES:(g // 2 + 1) * LANES])
            acc_sc[g] = acc_sc[g] * alpha + pv
            m_sc[g] = m_new

    def body(kj, carry):
        step(kj, None)
        return carry

    lax.fori_loop(0, qi * ratio, body, 0)
    for d in range(ratio):
        step(qi * ratio + d, d)
    lane = lax.broadcasted_iota(jnp.int32, (tq, LANES), 1)
    for pr in range(MLA_G // 2):
        o0 = acc_sc[2 * pr] / l_sc[2 * pr]
        o1 = acc_sc[2 * pr + 1] / l_sc[2 * pr + 1]
        o_ref[:, pr * LANES:(pr + 1) * LANES] = jnp.where(lane < MLA_V, o0, o1).astype(o_ref.dtype)


def _mla_attn(q, k, v, B, S):
    T = q.shape[0]
    tq, tk = 512, 512
    nq = S // tq
    groups = MLA_HEADS // MLA_G
    return pl.pallas_call(
        functools.partial(_mla_attn_kernel, tq=tq, tk=tk),
        out_shape=jax.ShapeDtypeStruct((T, MLA_HEADS * MLA_V), BF16),
        grid=(B, groups, nq),
        in_specs=[pl.BlockSpec((tq, MLA_G * MLA_HP), lambda b, h, i: (b * nq + i, h)),
                  pl.BlockSpec((MLA_G * MLA_HP, S), lambda b, h, i: (b * groups + h, 0)),
                  pl.BlockSpec((S, MLA_G * MLA_V), lambda b, h, i: (b, h))],
        out_specs=pl.BlockSpec((tq, MLA_G * MLA_V), lambda b, h, i: (b * nq + i, h)),
        scratch_shapes=[pltpu.VMEM((MLA_G, tq, LANES), F32),
                        pltpu.VMEM((MLA_G, tq, LANES), F32),
                        pltpu.VMEM((MLA_G, tq, LANES), F32)],
        compiler_params=_cparams(("parallel", "parallel", "arbitrary")),
        name="mla_attn",
    )(q, k, v)


def _swa_kernel(sink_ref, q_ref, kc_ref, kp_ref, vc_ref, vp_ref, o_ref):
    n = pl.program_id(1)
    hd = SWA_HEAD_DIM
    qi = lax.broadcasted_iota(jnp.int32, (WINDOW, 2 * WINDOW), 0)
    kj = lax.broadcasted_iota(jnp.int32, (WINDOW, 2 * WINDOW), 1)
    delta = qi + WINDOW - kj
    first_key = jnp.where(n > 0, 0, WINDOW)
    valid = (delta >= 0) & (delta < WINDOW) & (kj >= first_key)
    deltaf = delta.astype(F32)
    for g in range(SWA_KV_HEADS):
        kcat = jnp.concatenate([kp_ref[:, g * hd:(g + 1) * hd], kc_ref[:, g * hd:(g + 1) * hd]], axis=0)
        vcat = jnp.concatenate([vp_ref[:, g * hd:(g + 1) * hd], vc_ref[:, g * hd:(g + 1) * hd]], axis=0)
        kcat = kcat.astype(BF16)
        vcat = vcat.astype(BF16)
        for r in range(SWA_GROUP):
            h = g * SWA_GROUP + r
            slope = 2.0 ** (-8.0 * (h + 1) / SWA_HEADS)
            q = q_ref[:, h * hd:(h + 1) * hd].astype(BF16)
            s = _dot_nt(q, kcat) * (hd ** -0.5) - slope * deltaf
            s = jnp.where(valid, s, NEG_BIG)
            sink = sink_ref[h]
            m = jnp.maximum(jnp.max(s, axis=1, keepdims=True), sink)
            p = jnp.exp(s - m)
            denom = jnp.sum(p, axis=1, keepdims=True) + jnp.exp(sink - m)
            o = _dot(p.astype(BF16), vcat) / denom
            o_ref[:, h * hd:(h + 1) * hd] = o.astype(o_ref.dtype)


def _swa(proj, sinks, B, S):
    T = proj.shape[0]
    nb = S // WINDOW
    cq = COL_SWAQ // (SWA_HEADS * SWA_HEAD_DIM)
    ck = COL_SWAK // LANES
    cv = COL_SWAV // LANES
    cur = lambda c: (lambda b, n: (b * nb + n, c))
    prev = lambda c: (lambda b, n: (b * nb + jnp.maximum(n - 1, 0), c))
    return pl.pallas_call(
        _swa_kernel,
        out_shape=jax.ShapeDtypeStruct((T, SWA_HEADS * SWA_HEAD_DIM), BF16),
        grid=(B, nb),
        in_specs=[pl.BlockSpec(memory_space=pltpu.SMEM),
                  pl.BlockSpec((WINDOW, SWA_HEADS * SWA_HEAD_DIM), cur(cq)),
                  pl.BlockSpec((WINDOW, LANES), cur(ck)),
                  pl.BlockSpec((WINDOW, LANES), prev(ck)),
                  pl.BlockSpec((WINDOW, LANES), cur(cv)),
                  pl.BlockSpec((WINDOW, LANES), prev(cv))],
        out_specs=pl.BlockSpec((WINDOW, SWA_HEADS * SWA_HEAD_DIM), lambda b, n: (b * nb + n, 0)),
        compiler_params=_cparams(("parallel", "arbitrary")),
        name="swa_attn",
    )(sinks, proj, proj, proj, proj, proj)


def _conv_silu(cur, prev8, w, sub):
    acc = cur * w[CONV_WIDTH - 1:CONV_WIDTH, :]
    for d in range(1, CONV_WIDTH):
        rolled = pltpu.roll(cur, d, axis=0)
        head = jnp.where(sub < d, pltpu.roll(prev8, d, axis=0), rolled[:8])
        shifted = jnp.concatenate([head, rolled[8:]], axis=0)
        acc = acc + shifted * w[CONV_WIDTH - 1 - d:CONV_WIDTH - d, :]
    return _silu(acc)


def _l2_normalize(y):
    return y * lax.rsqrt(jnp.sum(y * y, axis=-1, keepdims=True) + RMS_EPS)


def _gdn_chunk_kernel(xq_ref, xk_ref, xv_ref, ba_ref, z_ref, cw_ref, alog_ref, dtb_ref, ng_ref, o_ref,
                      state_sc, tail_sc):
    @pl.when(pl.program_id(1) == 0)
    def _():
        state_sc[...] = jnp.zeros_like(state_sc)
        tail_sc[...] = jnp.zeros_like(tail_sc)

    C = GDN_C
    H = GDN_HEADS
    W = H * GDN_DK
    ri = lax.broadcasted_iota(jnp.int32, (C, C), 0)
    ci = lax.broadcasted_iota(jnp.int32, (C, C), 1)
    incl = ri >= ci
    strict = ri > ci
    merge = jnp.where(strict, ri ^ ci, 2 * C)
    ng = ng_ref[...]
    sub = lax.broadcasted_iota(jnp.int32, (8, LANES), 0)

    ba = ba_ref[...]
    lane = lax.broadcasted_iota(jnp.int32, (C, LANES), 1)
    pos = lax.broadcasted_iota(jnp.int32, (C, LANES), 0)
    t = ba + dtb_ref[...]
    g = -jnp.exp(alog_ref[...]) * (jnp.maximum(t, 0.0) + jnp.log1p(jnp.exp(-jnp.abs(t))))
    step = 1
    while step < C:
        g = g + jnp.where(pos >= step, pltpu.roll(g, step, axis=0), 0.0)
        step *= 2
    gt = jnp.where(lane < H, _sigmoid(ba), g)
    gtt = gt.T

    def run(group):
        lmat, a_qk, rhs, q_dec, k_dec_t, c_decay = [], [], [], [], [], []
        for h in group:
            cols = slice(h * GDN_DK, (h + 1) * GDN_DK)
            q = _l2_normalize(_conv_silu(xq_ref[:, cols], tail_sc[0, :, cols], cw_ref[:, h * GDN_DK:(h + 1) * GDN_DK],
                                         sub)) * (GDN_DK ** -0.5)
            k = _l2_normalize(_conv_silu(xk_ref[:, cols], tail_sc[1, :, cols],
                                         cw_ref[:, W + h * GDN_DK:W + (h + 1) * GDN_DK], sub))
            v = _conv_silu(xv_ref[:, cols], tail_sc[2, :, cols],
                           cw_ref[:, 2 * W + h * GDN_DV:2 * W + (h + 1) * GDN_DV], sub)
            beta = gt[:, h:h + 1]
            gc = gt[:, H + h:H + h + 1]
            gr = gtt[H + h:H + h + 1, :]
            glast = gc[C - 1:C, :]
            decay = jnp.exp(jnp.where(incl, gc - gr, NEG_BIG))
            eg = jnp.exp(gc)
            kb = k * beta
            a1 = _dot_nt(jnp.concatenate([kb, q], axis=0).astype(BF16), k.astype(BF16))
            lmat.append(a1[:C] * jnp.where(strict, decay, 0.0))
            a_qk.append((a1[C:] * decay).astype(BF16))
            rhs.append(jnp.concatenate([v * beta, kb * eg], axis=1))
            q_dec.append((q * eg).astype(BF16))
            k_dec_t.append((k * jnp.exp(glast - gc)).T.astype(BF16))
            c_decay.append(jnp.exp(glast))

        base = merge < 16
        mpow = [jnp.where(base, l, 0.0) for l in lmat]
        rinv = [-m for m in mpow]
        for _ in range(3):
            mb = [m.astype(BF16) for m in mpow]
            mpow = [_dot(m, m) for m in mb]
            rinv = [r + m + _dot(r.astype(BF16), m.astype(BF16)) for r, m in zip(rinv, mpow)]
        s = 16
        while s < C:
            level = (merge >= s) & (merge < 2 * s)
            off = [jnp.where(level, l, 0.0) for l in lmat]
            y = [c + _dot(r.astype(BF16), c.astype(BF16)) for r, c in zip(rinv, off)]
            rinv = [r - yy - _dot(yy.astype(BF16), r.astype(BF16)) for r, yy in zip(rinv, y)]
            s *= 2

        sol = [x + _dot(r.astype(BF16), x.astype(BF16)) for r, x in zip(rinv, rhs)]
        idx = range(len(group))
        a2 = [_dot(jnp.concatenate([sol[i][:, GDN_DV:].astype(BF16), q_dec[i]], axis=0),
                   state_sc[group[i]].astype(BF16)) for i in idx]
        v_new = [(sol[i][:, :GDN_DV] - a2[i][:C]).astype(BF16) for i in idx]
        out = [a2[i][C:] + _dot(a_qk[i], v_new[i]) for i in idx]
        for i in idx:
            state_sc[group[i]] = state_sc[group[i]] * c_decay[i] + _dot(k_dec_t[i], v_new[i])
        for i in idx:
            h = group[i]
            cols = slice(h * GDN_DV, (h + 1) * GDN_DV)
            o = out[i]
            on = o * lax.rsqrt(jnp.mean(o * o, axis=-1, keepdims=True) + RMS_EPS) * ng
            o_ref[:, cols] = (on * _silu(z_ref[:, cols])).astype(o_ref.dtype)

    for first in range(0, H, GDN_HEAD_GROUP):
        run(list(range(first, first + GDN_HEAD_GROUP)))
    tail_sc[0] = xq_ref[C - 8:C, :]
    tail_sc[1] = xk_ref[C - 8:C, :]
    tail_sc[2] = xv_ref[C - 8:C, :]


def _gdn(proj, conv_w, alog128, dtb128, norm_g, B, S):
    T = proj.shape[0]
    R = GDN_C
    nr = S // R
    W = GDN_HEADS * GDN_DK
    row = lambda c: (lambda b, r: (b * nr + r, c))
    full = lambda shape: pl.BlockSpec(shape, lambda b, r: (0, 0))
    return pl.pallas_call(
        _gdn_chunk_kernel,
        out_shape=jax.ShapeDtypeStruct((T, GDN_HEADS * GDN_DV), BF16),
        grid=(B, nr),
        in_specs=[pl.BlockSpec((R, W), row(COL_GQ // W)),
                  pl.BlockSpec((R, W), row(COL_GK // W)),
                  pl.BlockSpec((R, W), row(COL_GV // W)),
                  pl.BlockSpec((R, LANES), row(COL_BA // LANES)),
                  pl.BlockSpec((R, W), row(COL_Z // W)),
                  full((CONV_WIDTH, GDN_QKV)), full((1, LANES)), full((1, LANES)), full((1, GDN_DV))],
        out_specs=pl.BlockSpec((R, W), row(0)),
        scratch_shapes=[pltpu.VMEM((GDN_HEADS, GDN_DK, GDN_DV), F32),
                        pltpu.VMEM((3, 8, W), F32)],
        compiler_params=_cparams(("parallel", "arbitrary")),
        name="gdn",
    )(proj, proj, proj, proj, proj, conv_w, alog128, dtb128, norm_g)


def _out_proj_kernel(a_ref, s_ref, g_ref, x_ref, wa_ref, ws_ref, wg_ref, lg_ref, lb_ref, of_ref, ob_ref):
    y = _dot(a_ref[...], wa_ref[...]) + _dot(s_ref[...], ws_ref[...]) + _dot(g_ref[...], wg_ref[...])
    y = _layer_norm_rows(DN_ALPHA * x_ref[...] + y, lg_ref[...], lb_ref[...])
    of_ref[...] = y
    ob_ref[...] = y.astype(BF16)


def _out_proj_ln(mla, swa, gdn, x, w_out, l, ln_g, ln_b):
    T = x.shape[0]
    tm = 256
    n_mla, n_swa, n_gdn = mla.shape[1], swa.shape[1], gdn.shape[1]
    assert n_mla % n_swa == 0 and (n_mla + n_swa) % n_gdn == 0
    row = lambda w: pl.BlockSpec((tm, w), lambda i: (i, 0))
    full = lambda shape: pl.BlockSpec(shape, lambda i: (0, 0))
    return pl.pallas_call(
        _out_proj_kernel,
        out_shape=(jax.ShapeDtypeStruct((T, D_MODEL), F32), jax.ShapeDtypeStruct((T, D_MODEL), BF16)),
        grid=(T // tm,),
        in_specs=[row(n_mla), row(n_swa), row(n_gdn), row(D_MODEL),
                  pl.BlockSpec((None, n_mla, D_MODEL), lambda i: (l, 0, 0)),
                  pl.BlockSpec((None, n_swa, D_MODEL), lambda i: (l, n_mla // n_swa, 0)),
                  pl.BlockSpec((None, n_gdn, D_MODEL), lambda i: (l, (n_mla + n_swa) // n_gdn, 0)),
                  full((1, D_MODEL)), full((1, D_MODEL))],
        out_specs=(row(D_MODEL), row(D_MODEL)),
        compiler_params=_cparams(("parallel",)),
        name="out_proj_ln",
    )(mla, swa, gdn, x, w_out, w_out, w_out, ln_g, ln_b)


def _ffn_kernel(xb_ref, xf_ref, wg_ref, wu_ref, wd_ref, lg_ref, lb_ref, of_ref, ob_ref, acc_sc):
    j = pl.program_id(1)

    @pl.when(j == 0)
    def _():
        acc_sc[...] = jnp.zeros_like(acc_sc)

    xb = xb_ref[...]
    h = _silu(_dot(xb, wg_ref[...])) * _dot(xb, wu_ref[...])
    acc_sc[...] += _dot(h.astype(BF16), wd_ref[...])

    @pl.when(j == pl.num_programs(1) - 1)
    def _():
        y = _layer_norm_rows(DN_ALPHA * xf_ref[...] + acc_sc[...], lg_ref[...], lb_ref[...])
        of_ref[...] = y
        ob_ref[...] = y.astype(BF16)


def _ffn_ln(xb, xf, wg, wu, wd, m, ln_g, ln_b):
    T = xb.shape[0]
    tm, tf = 512, 512
    return pl.pallas_call(
        _ffn_kernel,
        out_shape=(jax.ShapeDtypeStruct((T, D_MODEL), F32), jax.ShapeDtypeStruct((T, D_MODEL), BF16)),
        grid=(T // tm, D_FF // tf),
        in_specs=[pl.BlockSpec((tm, D_MODEL), lambda i, j: (i, 0)),
                  pl.BlockSpec((tm, D_MODEL), lambda i, j: (i, 0)),
                  pl.BlockSpec((None, D_MODEL, tf), lambda i, j: (m, 0, j)),
                  pl.BlockSpec((None, D_MODEL, tf), lambda i, j: (m, 0, j)),
                  pl.BlockSpec((None, tf, D_MODEL), lambda i, j: (m, j, 0)),
                  pl.BlockSpec((1, D_MODEL), lambda i, j: (0, 0)),
                  pl.BlockSpec((1, D_MODEL), lambda i, j: (0, 0))],
        out_specs=(pl.BlockSpec((tm, D_MODEL), lambda i, j: (i, 0)),
                   pl.BlockSpec((tm, D_MODEL), lambda i, j: (i, 0))),
        scratch_shapes=[pltpu.VMEM((tm, D_MODEL), F32)],
        compiler_params=_cparams(("parallel", "arbitrary")),
        name="ffn_ln",
    )(xb, xf, wg, wu, wd, ln_g, ln_b)


def _router_kernel(x_ref, w_ref, o_ref):
    logits = jnp.dot(x_ref[...], w_ref[...], preferred_element_type=F32, precision=lax.Precision.HIGHEST)
    lane = lax.broadcasted_iota(jnp.int32, logits.shape, 1).astype(F32)
    logits = jnp.where(lane < N_EXPERTS, logits, -jnp.inf)
    m1 = jnp.max(logits, axis=1, keepdims=True)
    i1 = jnp.min(jnp.where(logits == m1, lane, float(LANES)), axis=1, keepdims=True)
    rest = jnp.where(lane == i1, -jnp.inf, logits)
    m2 = jnp.max(rest, axis=1, keepdims=True)
    i2 = jnp.min(jnp.where(rest == m2, lane, float(LANES)), axis=1, keepdims=True)
    e = jnp.exp(m2 - m1)
    g1 = 1.0 / (1.0 + e)
    g2 = e / (1.0 + e)
    out = jnp.where(lane == 0, i1,
                    jnp.where(lane == 1, i2,
                              jnp.where(lane == 2, g1, jnp.where(lane == 3, g2, 0.0))))
    o_ref[...] = out


def _router(xf, w128):
    T = xf.shape[0]
    tm = 512
    return pl.pallas_call(
        _router_kernel,
        out_shape=jax.ShapeDtypeStruct((T, LANES), F32),
        grid=(T // tm,),
        in_specs=[pl.BlockSpec((tm, D_MODEL), lambda i: (i, 0)),
                  pl.BlockSpec((D_MODEL, LANES), lambda i: (0, 0))],
        out_specs=pl.BlockSpec((tm, LANES), lambda i: (i, 0)),
        compiler_params=_cparams(("parallel",)),
        name="moe_router",
    )(xf, w128)


MOE_TM = 512
MOE_TF = 1024
MOE_NJ = D_FF // MOE_TF
MOE_RPS = -(-MOE_TM // MOE_NJ)
MOE_MOVED = MOE_RPS * MOE_NJ
MOE_TMX = -(-MOE_MOVED // 8) * 8


def _moe_kernel(te_ref, nu_ref, tok_ref, dst_ref, x_hbm, wg_ref, wu_ref, wd_ref, o_hbm,
                xg_sc, xb_sc, acc_sc, gsem, ssem, *, n_tokens):
    i = pl.program_id(0)
    j = pl.program_id(1)
    n_used = nu_ref[0]
    active = i < n_used

    def gather_row(tile, s, q):
        pltpu.make_async_copy(x_hbm.at[tok_ref[tile * MOE_TMX + q]], xg_sc.at[s, q], gsem.at[s]).start()

    def scatter_row(seg, s, q):
        pltpu.make_async_copy(acc_sc.at[s, q], o_hbm.at[dst_ref[seg * MOE_TMX + q]], ssem.at[s]).start()

    def gather_wait(s):
        pltpu.make_async_copy(x_hbm.at[pl.ds(0, MOE_TM)], xg_sc.at[s, pl.ds(0, MOE_TM)], gsem.at[s]).wait()
        for q in range(MOE_TM, MOE_MOVED):
            pltpu.make_async_copy(x_hbm.at[0], xg_sc.at[s, q], gsem.at[s]).wait()

    def scatter_wait(s):
        pltpu.make_async_copy(acc_sc.at[s, pl.ds(0, MOE_TM)], o_hbm.at[pl.ds(0, MOE_TM)], ssem.at[s]).wait()
        for q in range(MOE_TM, MOE_MOVED):
            pltpu.make_async_copy(acc_sc.at[s, q], o_hbm.at[0], ssem.at[s]).wait()

    def run(slot):
        other = 1 - slot

        @pl.when(j == 0)
        def _():
            @pl.when(i == 0)
            def _():
                def issue(q, carry):
                    gather_row(0, 0, q)
                    return carry
                lax.fori_loop(0, MOE_MOVED, issue, 0, unroll=2)
                acc_sc[...] = jnp.zeros_like(acc_sc)
                for s in range(2):
                    spill = o_hbm.at[pl.ds(TOP_K * n_tokens + s * MOE_TMX, MOE_TMX)]
                    cp = pltpu.make_async_copy(acc_sc.at[0], spill, ssem.at[0])
                    cp.start()
                    cp.wait()

            @pl.when(i > 0)
            def _():
                scatter_wait(slot)
                acc_sc[slot, pl.ds(0, MOE_TM), :] = jnp.zeros((MOE_TM, D_MODEL), F32)

            gather_wait(slot)
            xb_sc[...] = xg_sc[slot, pl.ds(0, MOE_TM), :].astype(BF16)

        for r in range(MOE_RPS):
            scatter_row(i, other, j * MOE_RPS + r)
        xb = xb_sc[...]
        h = _silu(_dot(xb, wg_ref[...])) * _dot(xb, wu_ref[...])
        acc_sc[slot, pl.ds(0, MOE_TM), :] += _dot(h.astype(BF16), wd_ref[...])
        nxt = jnp.minimum(i + 1, n_used - 1)
        for r in range(MOE_RPS):
            gather_row(nxt, other, j * MOE_RPS + r)

        @pl.when((j == MOE_NJ - 1) & (i == n_used - 1))
        def _():
            gather_wait(other)
            scatter_wait(other)

            def issue(q, carry):
                scatter_row(i + 1, slot, q)
                return carry
            lax.fori_loop(0, MOE_MOVED, issue, 0, unroll=2)
            scatter_wait(slot)

    for sl in range(2):
        pl.when(active & (lax.rem(i, 2) == sl))(functools.partial(run, sl))


def _moe_grouped(tile_expert, n_used, row_token, row_dest, xf, wg, wu, wd, m):
    n_tiles = tile_expert.shape[0]
    n_tokens = xf.shape[0]

    def wcol(i, j, te, nu, tok, dst):
        jj = jnp.where(i < nu[0], j, MOE_NJ - 1)
        return (m, te[i], 0, jj)

    def wrow(i, j, te, nu, tok, dst):
        jj = jnp.where(i < nu[0], j, MOE_NJ - 1)
        return (m, te[i], jj, 0)

    grid_spec = pltpu.PrefetchScalarGridSpec(
        num_scalar_prefetch=4,
        grid=(n_tiles, MOE_NJ),
        in_specs=[pl.BlockSpec(memory_space=pl.ANY),
                  pl.BlockSpec((None, None, D_MODEL, MOE_TF), wcol),
                  pl.BlockSpec((None, None, D_MODEL, MOE_TF), wcol),
                  pl.BlockSpec((None, None, MOE_TF, D_MODEL), wrow)],
        out_specs=pl.BlockSpec(memory_space=pl.ANY),
        scratch_shapes=[pltpu.VMEM((2, MOE_TMX, D_MODEL), F32),
                        pltpu.VMEM((MOE_TM, D_MODEL), BF16),
                        pltpu.VMEM((2, MOE_TMX, D_MODEL), F32),
                        pltpu.SemaphoreType.DMA((2,)),
                        pltpu.SemaphoreType.DMA((2,))],
    )
    return pl.pallas_call(
        functools.partial(_moe_kernel, n_tokens=n_tokens),
        out_shape=jax.ShapeDtypeStruct((TOP_K * n_tokens + 2 * MOE_TMX, D_MODEL), F32),
        grid_spec=grid_spec,
        compiler_params=_cparams(("arbitrary", "arbitrary")),
        name="moe_grouped",
    )(tile_expert, n_used, row_token, row_dest, xf, wg, wu, wd)


def _moe_combine_kernel(o0_ref, o1_ref, rt_ref, x_ref, lg_ref, lb_ref, of_ref, ob_ref):
    rt = rt_ref[...]
    f = rt[:, 2:3] * o0_ref[...] + rt[:, 3:4] * o1_ref[...]
    y = _layer_norm_rows(DN_ALPHA * x_ref[...] + f, lg_ref[...], lb_ref[...])
    of_ref[...] = y
    ob_ref[...] = y.astype(BF16)


def _moe_combine_ln(o_rows, route, xf, ln_g, ln_b):
    T = xf.shape[0]
    tm = 512
    nt = T // tm
    return pl.pallas_call(
        _moe_combine_kernel,
        out_shape=(jax.ShapeDtypeStruct((T, D_MODEL), F32), jax.ShapeDtypeStruct((T, D_MODEL), BF16)),
        grid=(nt,),
        in_specs=[pl.BlockSpec((tm, D_MODEL), lambda i: (i, 0)),
                  pl.BlockSpec((tm, D_MODEL), lambda i: (i + nt, 0)),
                  pl.BlockSpec((tm, LANES), lambda i: (i, 0)),
                  pl.BlockSpec((tm, D_MODEL), lambda i: (i, 0)),
                  pl.BlockSpec((1, D_MODEL), lambda i: (0, 0)),
                  pl.BlockSpec((1, D_MODEL), lambda i: (0, 0))],
        out_specs=(pl.BlockSpec((tm, D_MODEL), lambda i: (i, 0)),
                   pl.BlockSpec((tm, D_MODEL), lambda i: (i, 0))),
        compiler_params=_cparams(("parallel",)),
        name="moe_combine_ln",
    )(o_rows, o_rows, route, xf, ln_g, ln_b)


def _moe_plan(route, T):
    n_pairs = TOP_K * T
    ids = route[:, :TOP_K].astype(jnp.int32)
    e_flat = ids.reshape(-1)
    pair = jnp.arange(n_pairs, dtype=jnp.int32)
    order = jnp.sort(e_flat * n_pairs + pair) % n_pairs
    onehot = (e_flat[:, None] == jnp.arange(N_EXPERTS, dtype=jnp.int32)[None, :]).astype(jnp.int32)
    counts = jnp.sum(onehot, axis=0)
    first = jnp.cumsum(counts) - counts
    tiles_per = (counts + MOE_TM - 1) // MOE_TM
    tile_end = jnp.cumsum(tiles_per)
    tile_start = tile_end - tiles_per
    n_used = tile_end[-1]
    n_tiles = n_pairs // MOE_TM + N_EXPERTS
    tiles = jnp.arange(n_tiles, dtype=jnp.int32)
    tidx = jnp.minimum(tiles, n_used - 1)
    tile_expert = jnp.sum((tidx[:, None] >= tile_end[None, :]).astype(jnp.int32), axis=1)
    tile_expert = jnp.minimum(tile_expert, N_EXPERTS - 1).astype(jnp.int32)
    r_in = jnp.arange(MOE_TMX, dtype=jnp.int32)[None, :]
    k = (tiles - tile_start[tile_expert])[:, None] * MOE_TM + r_in
    valid = (r_in < MOE_TM) & (k < counts[tile_expert][:, None]) & (tiles < n_used)[:, None]
    src = jnp.clip(first[tile_expert][:, None] + k, 0, n_pairs - 1)
    p = order[src]
    spill = n_pairs + (tiles % 2)[:, None] * MOE_TMX + r_in
    row_token = jnp.where(valid, p // TOP_K, 0).reshape(-1)
    row_dest = jnp.where(valid, (p % TOP_K) * T + p // TOP_K, spill)
    lead = n_pairs + MOE_TMX + r_in
    row_dest = jnp.concatenate([lead, row_dest], axis=0).reshape(-1)
    return tile_expert, n_used.reshape(1).astype(jnp.int32), row_token, row_dest


def _prep_w_in(w_in):
    L = w_in.shape[0]
    z = lambda n: jnp.zeros((L, D_MODEL, n), w_in.dtype)
    o_swa = MLA_IN
    o_gdn = MLA_IN + SWA_IN
    hd = SWA_HEAD_DIM
    kr = w_in[:, :, MLA_Q_RANK + MLA_KV_RANK:MLA_IN]
    half = MLA_ROPE // 2
    kr_sw = jnp.concatenate([kr[:, :, half:], kr[:, :, :half]], axis=-1)
    ba = w_in[:, :, o_gdn + GDN_QKV:o_gdn + GDN_QKV + 2 * GDN_HEADS]
    segs = [
        w_in[:, :, :MLA_Q_RANK],
        w_in[:, :, o_swa:o_swa + SWA_HEADS * hd],
        w_in[:, :, o_gdn + GDN_QKV + 2 * GDN_HEADS:],
        w_in[:, :, o_gdn:o_gdn + GDN_QKV],
        w_in[:, :, MLA_Q_RANK:MLA_Q_RANK + MLA_KV_RANK],
        w_in[:, :, o_swa + SWA_HEADS * hd:o_swa + (SWA_HEADS + SWA_KV_HEADS) * hd],
        w_in[:, :, o_swa + (SWA_HEADS + SWA_KV_HEADS) * hd:o_swa + SWA_IN],
        kr, kr_sw, z(LANES - 2 * MLA_ROPE),
        ba, z(LANES - 2 * GDN_HEADS),
        z(D_PROJ - COL_BA - LANES),
    ]
    return jnp.concatenate([seg.astype(BF16) for seg in segs], axis=-1)


def _prep_mla_weights(w_uq, w_ukv):
    L = w_uq.shape[0]
    half = MLA_ROPE // 2
    wq = w_uq.reshape(L, MLA_Q_RANK, MLA_HEADS, MLA_NOPE + MLA_ROPE)
    nope, rope = wq[..., :MLA_NOPE], wq[..., MLA_NOPE:]
    rope_sw = jnp.concatenate([rope[..., half:], rope[..., :half]], axis=-1)
    zpad = jnp.zeros(wq.shape[:3] + (MLA_HP - MLA_NOPE - MLA_ROPE,), wq.dtype)
    wq_pad = jnp.concatenate([nope, rope, zpad], axis=-1).reshape(L, MLA_Q_RANK, MLA_HEADS * MLA_HP)
    wq_sw = jnp.concatenate([jnp.zeros_like(nope), rope_sw, zpad], axis=-1).reshape(L, MLA_Q_RANK, MLA_HEADS * MLA_HP)
    wkv = w_ukv.reshape(L, MLA_KV_RANK, MLA_HEADS, MLA_NOPE + MLA_V)
    knope, vv = wkv[..., :MLA_NOPE], wkv[..., MLA_NOPE:]
    wk_pad = jnp.concatenate([knope, jnp.zeros(knope.shape[:3] + (MLA_HP - MLA_NOPE,), wkv.dtype)], axis=-1)
    wk_pad = wk_pad.reshape(L, MLA_KV_RANK, MLA_HEADS * MLA_HP)
    wv = vv.reshape(L, MLA_KV_RANK, MLA_HEADS * MLA_V)
    return wq_pad.astype(BF16), wq_sw.astype(BF16), wk_pad.astype(BF16), wv.astype(BF16)


def _rope_tables(S):
    half = MLA_ROPE // 2
    pos = jnp.arange(S, dtype=jnp.int32)
    inv_freq = ROPE_BASE ** (-jnp.arange(half, dtype=F32) / half)
    ang = pos.astype(F32)[:, None] * inv_freq[None, :]
    cos, sin = jnp.cos(ang), jnp.sin(ang)
    cos2 = jnp.concatenate([cos, cos], axis=-1)
    sin2 = jnp.concatenate([-sin, sin], axis=-1)
    scale = (MLA_NOPE + MLA_ROPE) ** -0.5
    tail = jnp.zeros((S, MLA_HP - MLA_NOPE - MLA_ROPE), F32)
    cq_head = jnp.concatenate([jnp.ones((S, MLA_NOPE), F32), cos2, tail], axis=-1) * scale
    sq_head = jnp.concatenate([jnp.zeros((S, MLA_NOPE), F32), sin2, tail], axis=-1) * scale
    cqt = jnp.tile(cq_head, (1, MLA_HEADS))
    sqt = jnp.tile(sq_head, (1, MLA_HEADS))
    ckt = jnp.concatenate([cos2, sin2, jnp.zeros((S, LANES - 2 * MLA_ROPE), F32)], axis=-1)
    j = jnp.arange(LANES)[:, None]
    col = jnp.arange(MLA_HEADS * MLA_HP)[None, :]
    place = ((col % MLA_HP) == (MLA_NOPE + (j % MLA_ROPE))) & (j < 2 * MLA_ROPE)
    return cqt, sqt, ckt, place.astype(BF16)


def _pad_lanes(v, offset):
    out = jnp.zeros((1, LANES), F32)
    return lax.dynamic_update_slice(out, v.astype(F32).reshape(1, -1), (0, offset))


def kernel(x, w_in, mla_q_norm, mla_w_uq, mla_kv_norm, mla_w_ukv, swa_sinks, gdn_conv,
           gdn_a_log, gdn_dt_bias, gdn_norm, w_out, ln1_g, ln1_b, ln2_g, ln2_b,
           ffn_w_gate, ffn_w_up, ffn_w_down, moe_router, moe_w_gate, moe_w_up, moe_w_down):
    B, S, D = x.shape
    T = B * S
    xf = x.reshape(T, D)
    xb = xf.astype(BF16)

    w_in_b = _prep_w_in(w_in)
    wq_pad, wq_sw, wk_pad, wv = _prep_mla_weights(mla_w_uq, mla_w_ukv)
    cqt, sqt, ckt, place = _rope_tables(S)
    w_out_b = w_out.astype(BF16)
    ffn_g, ffn_u, ffn_d = ffn_w_gate.astype(BF16), ffn_w_up.astype(BF16), ffn_w_down.astype(BF16)
    moe_g, moe_u, moe_d = moe_w_gate.astype(BF16), moe_w_up.astype(BF16), moe_w_down.astype(BF16)
    router128 = jnp.pad(moe_router.astype(F32), ((0, 0), (0, 0), (0, LANES - N_EXPERTS)))

    for l in range(DEPTH):
        proj = _in_proj(xb, w_in_b, l)
        q, k, v = _mla_prep(proj, mla_q_norm[l].reshape(1, -1), mla_kv_norm[l].reshape(1, -1),
                            wq_pad, wq_sw, wk_pad, wv, l, place, cqt, sqt, ckt, S)
        h_mla = _mla_attn(q, k, v, B, S)
        h_swa = _swa(proj, swa_sinks[l].astype(F32), B, S)
        h_gdn = _gdn(proj, gdn_conv[l], _pad_lanes(gdn_a_log[l], GDN_HEADS), _pad_lanes(gdn_dt_bias[l], GDN_HEADS),
                     gdn_norm[l].reshape(1, -1), B, S)
        xf, xb = _out_proj_ln(h_mla, h_swa, h_gdn, xf, w_out_b, l,
                              ln1_g[l].reshape(1, -1), ln1_b[l].reshape(1, -1))
        lg, lb = ln2_g[l].reshape(1, -1), ln2_b[l].reshape(1, -1)
        if l % 2 == 0:
            xf, xb = _ffn_ln(xb, xf, ffn_g, ffn_u, ffn_d, l // 2, lg, lb)
        else:
            m = l // 2
            route = _router(xf, router128[m])
            tile_expert, n_used, row_token, row_dest = _moe_plan(route, T)
            o_rows = _moe_grouped(tile_expert, n_used, row_token, row_dest, xf, moe_g, moe_u, moe_d, m)
            xf, xb = _moe_combine_ln(o_rows, route, xf, lg, lb)
    return xf.reshape(B, S, D)
```

```python
import functools
import math

import jax
import jax.numpy as jnp
from jax import lax
from jax.experimental import pallas as pl
from jax.experimental.pallas import tpu as pltpu

F32 = jnp.float32
BF16 = jnp.bfloat16

D_MODEL = 2048
DEPTH = 4
MLA_HEADS = 8
MLA_Q_RANK = 512
MLA_KV_RANK = 256
MLA_NOPE = 64
MLA_ROPE = 32
MLA_V = 64
ROPE_BASE = 10000.0
SWA_HEADS = 8
SWA_KV_HEADS = 2
SWA_GROUP = SWA_HEADS // SWA_KV_HEADS
SWA_HEAD_DIM = 64
WINDOW = 128
GDN_HEADS = 8
GDN_DK = 128
GDN_DV = 128
CONV_WIDTH = 4
GDN_C = 256
GDN_HEAD_GROUP = 4
D_FF = 7168
N_EXPERTS = 8
TOP_K = 2
DN_ALPHA = float((2 * DEPTH) ** 0.25)
LN_EPS = 1e-5
RMS_EPS = 1e-6

MLA_IN = MLA_Q_RANK + MLA_KV_RANK + MLA_ROPE
SWA_IN = (SWA_HEADS + 2 * SWA_KV_HEADS) * SWA_HEAD_DIM
GDN_QKV = GDN_HEADS * (2 * GDN_DK + GDN_DV)

LANES = 128
NEG_BIG = -1e30
VMEM_LIMIT = 56 * 1024 * 1024

COL_CQ = 0
COL_SWAQ = 512
COL_Z = 1024
COL_GQ = 2048
COL_GK = 3072
COL_GV = 4096
COL_CKV = 5120
COL_SWAK = 5376
COL_SWAV = 5504
COL_KR = 5632
COL_BA = 5760
D_PROJ = 6144

MLA_HP = 128


def _cparams(sem, vmem=VMEM_LIMIT):
    return pltpu.CompilerParams(dimension_semantics=sem, vmem_limit_bytes=vmem)


def _layer_norm_rows(y, g, b):
    mu = jnp.mean(y, axis=-1, keepdims=True)
    d = y - mu
    var = jnp.mean(d * d, axis=-1, keepdims=True)
    return d * lax.rsqrt(var + LN_EPS) * g + b


def _sigmoid(x):
    return 1.0 / (1.0 + jnp.exp(-x))


def _silu(x):
    return x * _sigmoid(x)


def _dot(a, b):
    return jnp.dot(a, b, preferred_element_type=F32)


def _dot_nt(a, b):
    return lax.dot_general(a, b, (((1,), (1,)), ((), ())), preferred_element_type=F32)


def _matmul_kernel(x_ref, w_ref, o_ref):
    o_ref[...] = _dot(x_ref[...], w_ref[...])


def _in_proj(xb, w, l):
    T = xb.shape[0]
    tm, tn = 1024, 1024
    return pl.pallas_call(
        _matmul_kernel,
        out_shape=jax.ShapeDtypeStruct((T, D_PROJ), F32),
        grid=(T // tm, D_PROJ // tn),
        in_specs=[pl.BlockSpec((tm, D_MODEL), lambda i, j: (i, 0)),
                  pl.BlockSpec((None, D_MODEL, tn), lambda i, j: (l, 0, j))],
        out_specs=pl.BlockSpec((tm, tn), lambda i, j: (i, j)),
        compiler_params=_cparams(("parallel", "arbitrary")),
        name="in_proj",
    )(xb, w)


def _mla_prep_kernel(cq_ref, ckv_ref, kr_ref, gq_ref, gkv_ref, wq_ref, wqs_ref, wk_ref, wv_ref,
                     e_ref, cqt_ref, sqt_ref, ckt_ref, q_ref, k_ref, v_ref):
    cq = cq_ref[...]
    hq = cq * lax.rsqrt(jnp.mean(cq * cq, axis=-1, keepdims=True) + RMS_EPS) * gq_ref[...]
    hq = hq.astype(BF16)
    q = _dot(hq, wq_ref[...]) * cqt_ref[...] + _dot(hq, wqs_ref[...]) * sqt_ref[...]
    q_ref[...] = q.astype(BF16)
    ckv = ckv_ref[...]
    hk = ckv * lax.rsqrt(jnp.mean(ckv * ckv, axis=-1, keepdims=True) + RMS_EPS) * gkv_ref[...]
    hk = hk.astype(BF16)
    krp = kr_ref[...] * ckt_ref[...]
    krp_hi = krp.astype(BF16)
    krp_lo = (krp - krp_hi.astype(F32)).astype(BF16)
    e = e_ref[...]
    k = _dot(hk, wk_ref[...]) + (_dot(krp_hi, e) + _dot(krp_lo, e))
    for h in range(MLA_HEADS):
        k_ref[h * MLA_HP:(h + 1) * MLA_HP, :] = k[:, h * MLA_HP:(h + 1) * MLA_HP].T.astype(BF16)
    v_ref[...] = _dot(hk, wv_ref[...]).astype(BF16)


def _mla_prep(proj, gq, gkv, wq, wqs, wk, wv, l, e, cqt, sqt, ckt, S):
    T = proj.shape[0]
    tm = 512
    sb = S // tm
    full = lambda shape: pl.BlockSpec(shape, lambda i: (0, 0))
    layer = lambda shape: pl.BlockSpec((None,) + shape, lambda i: (l, 0, 0))
    return pl.pallas_call(
        _mla_prep_kernel,
        out_shape=(jax.ShapeDtypeStruct((T, MLA_HEADS * MLA_HP), BF16),
                   jax.ShapeDtypeStruct((T // S * MLA_HEADS * MLA_HP, S), BF16),
                   jax.ShapeDtypeStruct((T, MLA_HEADS * MLA_V), BF16)),
        grid=(T // tm,),
        in_specs=[pl.BlockSpec((tm, MLA_Q_RANK), lambda i: (i, COL_CQ // MLA_Q_RANK)),
                  pl.BlockSpec((tm, MLA_KV_RANK), lambda i: (i, COL_CKV // MLA_KV_RANK)),
                  pl.BlockSpec((tm, LANES), lambda i: (i, COL_KR // LANES)),
                  full((1, MLA_Q_RANK)), full((1, MLA_KV_RANK)),
                  layer((MLA_Q_RANK, MLA_HEADS * MLA_HP)), layer((MLA_Q_RANK, MLA_HEADS * MLA_HP)),
                  layer((MLA_KV_RANK, MLA_HEADS * MLA_HP)), layer((MLA_KV_RANK, MLA_HEADS * MLA_V)),
                  full((LANES, MLA_HEADS * MLA_HP)),
                  pl.BlockSpec((tm, MLA_HEADS * MLA_HP), lambda i: (i % sb, 0)),
                  pl.BlockSpec((tm, MLA_HEADS * MLA_HP), lambda i: (i % sb, 0)),
                  pl.BlockSpec((tm, LANES), lambda i: (i % sb, 0))],
        out_specs=(pl.BlockSpec((tm, MLA_HEADS * MLA_HP), lambda i: (i, 0)),
                   pl.BlockSpec((MLA_HEADS * MLA_HP, tm), lambda i: (i // sb, i % sb)),
                   pl.BlockSpec((tm, MLA_HEADS * MLA_V), lambda i: (i, 0))),
        compiler_params=_cparams(("parallel",)),
        name="mla_prep",
    )(proj, proj, proj, gq, gkv, wq, wqs, wk, wv, e, cqt, sqt, ckt)


MLA_G = 4


def _mla_attn_kernel(q_ref, k_ref, v_ref, o_ref, acc_sc, m_sc, l_sc, *, tq, tk):
    qi = pl.program_id(2)
    acc_sc[...] = jnp.zeros_like(acc_sc)
    m_sc[...] = jnp.full_like(m_sc, NEG_BIG)
    l_sc[...] = jnp.zeros_like(l_sc)
    ratio = tq // tk

    def step(kj, diag):
        rows = pl.ds(pl.multiple_of(kj * tk, tk), tk)
        for g in range(MLA_G):
            s = _dot(q_ref[:, g * MLA_HP:(g + 1) * MLA_HP], k_ref[g * MLA_HP:(g + 1) * MLA_HP, rows])
            if diag is not None:
                r = lax.broadcasted_iota(jnp.int32, s.shape, 0)
                c = lax.broadcasted_iota(jnp.int32, s.shape, 1)
                s = jnp.where(c + diag * tk <= r, s, NEG_BIG)
            m_prev = m_sc[g]
            m_new = jnp.maximum(m_prev, jnp.max(s, axis=1, keepdims=True))
            alpha = jnp.exp(m_prev - m_new)
            p = jnp.exp(s - m_new[:, :1])
            l_sc[g] = alpha * l_sc[g] + jnp.sum(p, axis=1, keepdims=True)
            pv = _dot(p.astype(BF16), v_ref[rows, (g // 2) * LANES:(g // 2 + 1) * LANES])
            acc_sc[g] = acc_sc[g] * alpha + pv
            m_sc[g] = m_new

    def body(kj, carry):
        step(kj, None)
        return carry

    lax.fori_loop(0, qi * ratio, body, 0)
    for d in range(ratio):
        step(qi * ratio + d, d)
    lane = lax.broadcasted_iota(jnp.int32, (tq, LANES), 1)
    for pr in range(MLA_G // 2):
        o0 = acc_sc[2 * pr] / l_sc[2 * pr]
        o1 = acc_sc[2 * pr + 1] / l_sc[2 * pr + 1]
        o_ref[:, pr * LANES:(pr + 1) * LANES] = jnp.where(lane < MLA_V, o0, o1).astype(o_ref.dtype)


def _mla_attn(q, k, v, B, S):
    T = q.shape[0]
    tq, tk = 512, 512
    nq = S // tq
    groups = MLA_HEADS // MLA_G
    return pl.pallas_call(
        functools.partial(_mla_attn_kernel, tq=tq, tk=tk),
        out_shape=jax.ShapeDtypeStruct((T, MLA_HEADS * MLA_V), BF16),
        grid=(B, groups, nq),
        in_specs=[pl.BlockSpec((tq, MLA_G * MLA_HP), lambda b, h, i: (b * nq + i, h)),
                  pl.BlockSpec((MLA_G * MLA_HP, S), lambda b, h, i: (b * groups + h, 0)),
                  pl.BlockSpec((S, MLA_G * MLA_V), lambda b, h, i: (b, h))],
        out_specs=pl.BlockSpec((tq, MLA_G * MLA_V), lambda b, h, i: (b * nq + i, h)),
        scratch_shapes=[pltpu.VMEM((MLA_G, tq, LANES), F32),
                        pltpu.VMEM((MLA_G, tq, LANES), F32),
                        pltpu.VMEM((MLA_G, tq, LANES), F32)],
        compiler_params=_cparams(("parallel", "parallel", "arbitrary")),
        name="mla_attn",
    )(q, k, v)


def _swa_kernel(sink_ref, q_ref, kc_ref, kp_ref, vc_ref, vp_ref, o_ref):
    n = pl.program_id(1)
    hd = SWA_HEAD_DIM
    qi = lax.broadcasted_iota(jnp.int32, (WINDOW, 2 * WINDOW), 0)
    kj = lax.broadcasted_iota(jnp.int32, (WINDOW, 2 * WINDOW), 1)
    delta = qi + WINDOW - kj
    first_key = jnp.where(n > 0, 0, WINDOW)
    valid = (delta >= 0) & (delta < WINDOW) & (kj >= first_key)
    deltaf = delta.astype(F32)
    for g in range(SWA_KV_HEADS):
        kcat = jnp.concatenate([kp_ref[:, g * hd:(g + 1) * hd], kc_ref[:, g * hd:(g + 1) * hd]], axis=0)
        vcat = jnp.concatenate([vp_ref[:, g * hd:(g + 1) * hd], vc_ref[:, g * hd:(g + 1) * hd]], axis=0)
        kcat = kcat.astype(BF16)
        vcat = vcat.astype(BF16)
        for r in range(SWA_GROUP):
            h = g * SWA_GROUP + r
            slope = 2.0 ** (-8.0 * (h + 1) / SWA_HEADS)
            q = q_ref[:, h * hd:(h + 1) * hd].astype(BF16)
            s = _dot_nt(q, kcat) * (hd ** -0.5) - slope * deltaf
            s = jnp.where(valid, s, NEG_BIG)
            sink = sink_ref[h]
            m = jnp.maximum(jnp.max(s, axis=1, keepdims=True), sink)
            p = jnp.exp(s - m)
            denom = jnp.sum(p, axis=1, keepdims=True) + jnp.exp(sink - m)
            o = _dot(p.astype(BF16), vcat) / denom
            o_ref[:, h * hd:(h + 1) * hd] = o.astype(o_ref.dtype)


def _swa(proj, sinks, B, S):
    T = proj.shape[0]
    nb = S // WINDOW
    cq = COL_SWAQ // (SWA_HEADS * SWA_HEAD_DIM)
    ck = COL_SWAK // LANES
    cv = COL_SWAV // LANES
    cur = lambda c: (lambda b, n: (b * nb + n, c))
    prev = lambda c: (lambda b, n: (b * nb + jnp.maximum(n - 1, 0), c))
    return pl.pallas_call(
        _swa_kernel,
        out_shape=jax.ShapeDtypeStruct((T, SWA_HEADS * SWA_HEAD_DIM), BF16),
        grid=(B, nb),
        in_specs=[pl.BlockSpec(memory_space=pltpu.SMEM),
                  pl.BlockSpec((WINDOW, SWA_HEADS * SWA_HEAD_DIM), cur(cq)),
                  pl.BlockSpec((WINDOW, LANES), cur(ck)),
                  pl.BlockSpec((WINDOW, LANES), prev(ck)),
                  pl.BlockSpec((WINDOW, LANES), cur(cv)),
                  pl.BlockSpec((WINDOW, LANES), prev(cv))],
        out_specs=pl.BlockSpec((WINDOW, SWA_HEADS * SWA_HEAD_DIM), lambda b, n: (b * nb + n, 0)),
        compiler_params=_cparams(("parallel", "arbitrary")),
        name="swa_attn",
    )(sinks, proj, proj, proj, proj, proj)


def _conv_silu(cur, prev8, w, sub):
    acc = cur * w[CONV_WIDTH - 1:CONV_WIDTH, :]
    for d in range(1, CONV_WIDTH):
        rolled = pltpu.roll(cur, d, axis=0)
        head = jnp.where(sub < d, pltpu.roll(prev8, d, axis=0), rolled[:8])
        shifted = jnp.concatenate([head, rolled[8:]], axis=0)
        acc = acc + shifted * w[CONV_WIDTH - 1 - d:CONV_WIDTH - d, :]
    return _silu(acc)


def _l2_normalize(y):
    return y * lax.rsqrt(jnp.sum(y * y, axis=-1, keepdims=True) + RMS_EPS)


def _gdn_chunk_kernel(xq_ref, xk_ref, xv_ref, ba_ref, z_ref, cw_ref, alog_ref, dtb_ref, ng_ref, o_ref,
                      state_sc, tail_sc):
    @pl.when(pl.program_id(1) == 0)
    def _():
        state_sc[...] = jnp.zeros_like(state_sc)
        tail_sc[...] = jnp.zeros_like(tail_sc)

    C = GDN_C
    H = GDN_HEADS
    W = H * GDN_DK
    ri = lax.broadcasted_iota(jnp.int32, (C, C), 0)
    ci = lax.broadcasted_iota(jnp.int32, (C, C), 1)
    incl = ri >= ci
    strict = ri > ci
    merge = jnp.where(strict, ri ^ ci, 2 * C)
    ng = ng_ref[...]
    sub = lax.broadcasted_iota(jnp.int32, (8, LANES), 0)

    ba = ba_ref[...]
    lane = lax.broadcasted_iota(jnp.int32, (C, LANES), 1)
    pos = lax.broadcasted_iota(jnp.int32, (C, LANES), 0)
    t = ba + dtb_ref[...]
    g = -jnp.exp(alog_ref[...]) * (jnp.maximum(t, 0.0) + jnp.log1p(jnp.exp(-jnp.abs(t))))
    step = 1
    while step < C:
        g = g + jnp.where(pos >= step, pltpu.roll(g, step, axis=0), 0.0)
        step *= 2
    gt = jnp.where(lane < H, _sigmoid(ba), g)
    gtt = gt.T

    def run(group):
        lmat, a_qk, rhs, q_dec, k_dec_t, c_decay = [], [], [], [], [], []
        for h in group:
            cols = slice(h * GDN_DK, (h + 1) * GDN_DK)
            q = _l2_normalize(_conv_silu(xq_ref[:, cols], tail_sc[0, :, cols], cw_ref[:, h * GDN_DK:(h + 1) * GDN_DK],
                                         sub)) * (GDN_DK ** -0.5)
            k = _l2_normalize(_conv_silu(xk_ref[:, cols], tail_sc[1, :, cols],
                                         cw_ref[:, W + h * GDN_DK:W + (h + 1) * GDN_DK], sub))
            v = _conv_silu(xv_ref[:, cols], tail_sc[2, :, cols],
                           cw_ref[:, 2 * W + h * GDN_DV:2 * W + (h + 1) * GDN_DV], sub)
            beta = gt[:, h:h + 1]
            gc = gt[:, H + h:H + h + 1]
            gr = gtt[H + h:H + h + 1, :]
            glast = gc[C - 1:C, :]
            decay = jnp.exp(jnp.where(incl, gc - gr, NEG_BIG))
            eg = jnp.exp(gc)
            kb = k * beta
            a1 = _dot_nt(jnp.concatenate([kb, q], axis=0).astype(BF16), k.astype(BF16))
            lmat.append(a1[:C] * jnp.where(strict, decay, 0.0))
            a_qk.append((a1[C:] * decay).astype(BF16))
            rhs.append(jnp.concatenate([v * beta, kb * eg], axis=1))
            q_dec.append((q * eg).astype(BF16))
            k_dec_t.append((k * jnp.exp(glast - gc)).T.astype(BF16))
            c_decay.append(jnp.exp(glast))

        base = merge < 16
        mpow = [jnp.where(base, l, 0.0) for l in lmat]
        rinv = [-m for m in mpow]
        for _ in range(3):
            mb = [m.astype(BF16) for m in mpow]
            mpow = [_dot(m, m) for m in mb]
            rinv = [r + m + _dot(r.astype(BF16), m.astype(BF16)) for r, m in zip(rinv, mpow)]
        s = 16
        while s < C:
            level = (merge >= s) & (merge < 2 * s)
            off = [jnp.where(level, l, 0.0) for l in lmat]
            y = [c + _dot(r.astype(BF16), c.astype(BF16)) for r, c in zip(rinv, off)]
            rinv = [r - yy - _dot(yy.astype(BF16), r.astype(BF16)) for r, yy in zip(rinv, y)]
            s *= 2

        sol = [x + _dot(r.astype(BF16), x.astype(BF16)) for r, x in zip(rinv, rhs)]
        idx = range(len(group))
        a2 = [_dot(jnp.concatenate([sol[i][:, GDN_DV:].astype(BF16), q_dec[i]], axis=0),
                   state_sc[group[i]].astype(BF16)) for i in idx]
        v_new = [(sol[i][:, :GDN_DV] - a2[i][:C]).astype(BF16) for i in idx]
        out = [a2[i][C:] + _dot(a_qk[i], v_new[i]) for i in idx]
        for i in idx:
            state_sc[group[i]] = state_sc[group[i]] * c_decay[i] + _dot(k_dec_t[i], v_new[i])
        for i in idx:
            h = group[i]
            cols = slice(h * GDN_DV, (h + 1) * GDN_DV)
            o = out[i]
            on = o * lax.rsqrt(jnp.mean(o * o, axis=-1, keepdims=True) + RMS_EPS) * ng
            o_ref[:, cols] = (on * _silu(z_ref[:, cols])).astype(o_ref.dtype)

    for first in range(0, H, GDN_HEAD_GROUP):
        run(list(range(first, first + GDN_HEAD_GROUP)))
    tail_sc[0] = xq_ref[C - 8:C, :]
    tail_sc[1] = xk_ref[C - 8:C, :]
    tail_sc[2] = xv_ref[C - 8:C, :]


def _gdn(proj, conv_w, alog128, dtb128, norm_g, B, S):
    T = proj.shape[0]
    R = GDN_C
    nr = S // R
    W = GDN_HEADS * GDN_DK
    row = lambda c: (lambda b, r: (b * nr + r, c))
    full = lambda shape: pl.BlockSpec(shape, lambda b, r: (0, 0))
    return pl.pallas_call(
        _gdn_chunk_kernel,
        out_shape=jax.ShapeDtypeStruct((T, GDN_HEADS * GDN_DV), BF16),
        grid=(B, nr),
        in_specs=[pl.BlockSpec((R, W), row(COL_GQ // W)),
                  pl.BlockSpec((R, W), row(COL_GK // W)),
                  pl.BlockSpec((R, W), row(COL_GV // W)),
                  pl.BlockSpec((R, LANES), row(COL_BA // LANES)),
                  pl.BlockSpec((R, W), row(COL_Z // W)),
                  full((CONV_WIDTH, GDN_QKV)), full((1, LANES)), full((1, LANES)), full((1, GDN_DV))],
        out_specs=pl.BlockSpec((R, W), row(0)),
        scratch_shapes=[pltpu.VMEM((GDN_HEADS, GDN_DK, GDN_DV), F32),
                        pltpu.VMEM((3, 8, W), F32)],
        compiler_params=_cparams(("parallel", "arbitrary")),
        name="gdn",
    )(proj, proj, proj, proj, proj, conv_w, alog128, dtb128, norm_g)


def _out_proj_kernel(a_ref, s_ref, g_ref, x_ref, wa_ref, ws_ref, wg_ref, lg_ref, lb_ref, of_ref, ob_ref):
    rows = 128
    for r in range(0, x_ref.shape[0], rows):
        sl = slice(r, r + rows)
        y = _dot(a_ref[sl, :], wa_ref[...]) + _dot(s_ref[sl, :], ws_ref[...]) + _dot(g_ref[sl, :], wg_ref[...])
        y = _layer_norm_rows(DN_ALPHA * x_ref[sl, :] + y, lg_ref[...], lb_ref[...])
        of_ref[sl, :] = y
        ob_ref[sl, :] = y.astype(BF16)


def _out_proj_ln(mla, swa, gdn, x, w_out, l, ln_g, ln_b):
    T = x.shape[0]
    tm = 512
    n_mla, n_swa, n_gdn = mla.shape[1], swa.shape[1], gdn.shape[1]
    assert n_mla % n_swa == 0 and (n_mla + n_swa) % n_gdn == 0
    row = lambda w: pl.BlockSpec((tm, w), lambda i: (i, 0))
    full = lambda shape: pl.BlockSpec(shape, lambda i: (0, 0))
    return pl.pallas_call(
        _out_proj_kernel,
        out_shape=(jax.ShapeDtypeStruct((T, D_MODEL), F32), jax.ShapeDtypeStruct((T, D_MODEL), BF16)),
        grid=(T // tm,),
        in_specs=[row(n_mla), row(n_swa), row(n_gdn), row(D_MODEL),
                  pl.BlockSpec((None, n_mla, D_MODEL), lambda i: (l, 0, 0)),
                  pl.BlockSpec((None, n_swa, D_MODEL), lambda i: (l, n_mla // n_swa, 0)),
                  pl.BlockSpec((None, n_gdn, D_MODEL), lambda i: (l, (n_mla + n_swa) // n_gdn, 0)),
                  full((1, D_MODEL)), full((1, D_MODEL))],
        out_specs=(row(D_MODEL), row(D_MODEL)),
        compiler_params=_cparams(("parallel",)),
        name="out_proj_ln",
    )(mla, swa, gdn, x, w_out, w_out, w_out, ln_g, ln_b)


def _ffn_kernel(xb_ref, xf_ref, wg_ref, wu_ref, wd_ref, lg_ref, lb_ref, of_ref, ob_ref, acc_sc):
    j = pl.program_id(1)

    @pl.when(j == 0)
    def _():
        acc_sc[...] = jnp.zeros_like(acc_sc)

    xb = xb_ref[...]
    h = _silu(_dot(xb, wg_ref[...])) * _dot(xb, wu_ref[...])
    acc_sc[...] += _dot(h.astype(BF16), wd_ref[...])

    @pl.when(j == pl.num_programs(1) - 1)
    def _():
        y = _layer_norm_rows(DN_ALPHA * xf_ref[...] + acc_sc[...], lg_ref[...], lb_ref[...])
        of_ref[...] = y
        ob_ref[...] = y.astype(BF16)


def _ffn_ln(xb, xf, wg, wu, wd, m, ln_g, ln_b):
    T = xb.shape[0]
    tm, tf = 512, 512
    return pl.pallas_call(
        _ffn_kernel,
        out_shape=(jax.ShapeDtypeStruct((T, D_MODEL), F32), jax.ShapeDtypeStruct((T, D_MODEL), BF16)),
        grid=(T // tm, D_FF // tf),
        in_specs=[pl.BlockSpec((tm, D_MODEL), lambda i, j: (i, 0)),
                  pl.BlockSpec((tm, D_MODEL), lambda i, j: (i, 0)),
                  pl.BlockSpec((None, D_MODEL, tf), lambda i, j: (m, 0, j)),
                  pl.BlockSpec((None, D_MODEL, tf), lambda i, j: (m, 0, j)),
                  pl.BlockSpec((None, tf, D_MODEL), lambda i, j: (m, j, 0)),
                  pl.BlockSpec((1, D_MODEL), lambda i, j: (0, 0)),
                  pl.BlockSpec((1, D_MODEL), lambda i, j: (0, 0))],
        out_specs=(pl.BlockSpec((tm, D_MODEL), lambda i, j: (i, 0)),
                   pl.BlockSpec((tm, D_MODEL), lambda i, j: (i, 0))),
        scratch_shapes=[pltpu.VMEM((tm, D_MODEL), F32)],
        compiler_params=_cparams(("parallel", "arbitrary")),
        name="ffn_ln",
    )(xb, xf, wg, wu, wd, ln_g, ln_b)


def _router_kernel(x_ref, w_ref, o_ref):
    logits = jnp.dot(x_ref[...], w_ref[...], preferred_element_type=F32, precision=lax.Precision.HIGHEST)
    lane = lax.broadcasted_iota(jnp.int32, logits.shape, 1).astype(F32)
    logits = jnp.where(lane < N_EXPERTS, logits, -jnp.inf)
    m1 = jnp.max(logits, axis=1, keepdims=True)
    i1 = jnp.min(jnp.where(logits == m1, lane, float(LANES)), axis=1, keepdims=True)
    rest = jnp.where(lane == i1, -jnp.inf, logits)
    m2 = jnp.max(rest, axis=1, keepdims=True)
    i2 = jnp.min(jnp.where(rest == m2, lane, float(LANES)), axis=1, keepdims=True)
    e = jnp.exp(m2 - m1)
    g1 = 1.0 / (1.0 + e)
    g2 = e / (1.0 + e)
    out = jnp.where(lane == 0, i1,
                    jnp.where(lane == 1, i2,
                              jnp.where(lane == 2, g1, jnp.where(lane == 3, g2, 0.0))))
    o_ref[...] = out


def _router(xf, w128):
    T = xf.shape[0]
    tm = 512
    return pl.pallas_call(
        _router_kernel,
        out_shape=jax.ShapeDtypeStruct((T, LANES), F32),
        grid=(T // tm,),
        in_specs=[pl.BlockSpec((tm, D_MODEL), lambda i: (i, 0)),
                  pl.BlockSpec((D_MODEL, LANES), lambda i: (0, 0))],
        out_specs=pl.BlockSpec((tm, LANES), lambda i: (i, 0)),
        compiler_params=_cparams(("parallel",)),
        name="moe_router",
    )(xf, w128)


MOE_TM = 512
MOE_TF = 1024
MOE_NJ = D_FF // MOE_TF
MOE_RPS = -(-MOE_TM // MOE_NJ)
MOE_MOVED = MOE_RPS * MOE_NJ
MOE_TMX = -(-MOE_MOVED // 8) * 8


def _moe_kernel(te_ref, nu_ref, tok_ref, dst_ref, x_hbm, wg_ref, wu_ref, wd_ref, o_hbm,
                xg_sc, xb_sc, acc_sc, gsem, ssem, *, n_tokens):
    i = pl.program_id(0)
    j = pl.program_id(1)
    n_used = nu_ref[0]
    active = i < n_used

    def gather_row(tile, s, q):
        pltpu.make_async_copy(x_hbm.at[tok_ref[tile * MOE_TMX + q]], xg_sc.at[s, q], gsem.at[s]).start()

    def scatter_row(seg, s, q):
        pltpu.make_async_copy(acc_sc.at[s, q], o_hbm.at[dst_ref[seg * MOE_TMX + q]], ssem.at[s]).start()

    def gather_wait(s):
        pltpu.make_async_copy(x_hbm.at[pl.ds(0, MOE_TM)], xg_sc.at[s, pl.ds(0, MOE_TM)], gsem.at[s]).wait()
        for q in range(MOE_TM, MOE_MOVED):
            pltpu.make_async_copy(x_hbm.at[0], xg_sc.at[s, q], gsem.at[s]).wait()

    def scatter_wait(s):
        pltpu.make_async_copy(acc_sc.at[s, pl.ds(0, MOE_TM)], o_hbm.at[pl.ds(0, MOE_TM)], ssem.at[s]).wait()
        for q in range(MOE_TM, MOE_MOVED):
            pltpu.make_async_copy(acc_sc.at[s, q], o_hbm.at[0], ssem.at[s]).wait()

    def run(slot):
        other = 1 - slot

        @pl.when(j == 0)
        def _():
            @pl.when(i == 0)
            def _():
                def issue(q, carry):
                    gather_row(0, 0, q)
                    return carry
                lax.fori_loop(0, MOE_MOVED, issue, 0, unroll=2)
                acc_sc[...] = jnp.zeros_like(acc_sc)
                for s in range(2):
                    spill = o_hbm.at[pl.ds(TOP_K * n_tokens + s * MOE_TMX, MOE_TMX)]
                    cp = pltpu.make_async_copy(acc_sc.at[0], spill, ssem.at[0])
                    cp.start()
                    cp.wait()

            @pl.when(i > 0)
            def _():
                scatter_wait(slot)
                acc_sc[slot, pl.ds(0, MOE_TM), :] = jnp.zeros((MOE_TM, D_MODEL), F32)

            gather_wait(slot)
            xb_sc[...] = xg_sc[slot, pl.ds(0, MOE_TM), :].astype(BF16)

        nxt = jnp.minimum(i + 1, n_used - 1)
        for r in range(MOE_RPS):
            scatter_row(i, other, j * MOE_RPS + r)
            gather_row(nxt, other, j * MOE_RPS + r)
        xb = xb_sc[...]
        h = _silu(_dot(xb, wg_ref[...])) * _dot(xb, wu_ref[...])
        acc_sc[slot, pl.ds(0, MOE_TM), :] += _dot(h.astype(BF16), wd_ref[...])

        @pl.when((j == MOE_NJ - 1) & (i == n_used - 1))
        def _():
            gather_wait(other)
            scatter_wait(other)

            def issue(q, carry):
                scatter_row(i + 1, slot, q)
                return carry
            lax.fori_loop(0, MOE_MOVED, issue, 0, unroll=2)
            scatter_wait(slot)

    for sl in range(2):
        pl.when(active & (lax.rem(i, 2) == sl))(functools.partial(run, sl))


def _moe_grouped(tile_expert, n_used, row_token, row_dest, xf, wg, wu, wd, m):
    n_tiles = tile_expert.shape[0]
    n_tokens = xf.shape[0]

    def wcol(i, j, te, nu, tok, dst):
        jj = jnp.where(i < nu[0], j, MOE_NJ - 1)
        return (m, te[i], 0, jj)

    def wrow(i, j, te, nu, tok, dst):
        jj = jnp.where(i < nu[0], j, MOE_NJ - 1)
        return (m, te[i], jj, 0)

    grid_spec = pltpu.PrefetchScalarGridSpec(
        num_scalar_prefetch=4,
        grid=(n_tiles, MOE_NJ),
        in_specs=[pl.BlockSpec(memory_space=pl.ANY),
                  pl.BlockSpec((None, None, D_MODEL, MOE_TF), wcol),
                  pl.BlockSpec((None, None, D_MODEL, MOE_TF), wcol),
                  pl.BlockSpec((None, None, MOE_TF, D_MODEL), wrow)],
        out_specs=pl.BlockSpec(memory_space=pl.ANY),
        scratch_shapes=[pltpu.VMEM((2, MOE_TMX, D_MODEL), F32),
                        pltpu.VMEM((MOE_TM, D_MODEL), BF16),
                        pltpu.VMEM((2, MOE_TMX, D_MODEL), F32),
                        pltpu.SemaphoreType.DMA((2,)),
                        pltpu.SemaphoreType.DMA((2,))],
    )
    return pl.pallas_call(
        functools.partial(_moe_kernel, n_tokens=n_tokens),
        out_shape=jax.ShapeDtypeStruct((TOP_K * n_tokens + 2 * MOE_TMX, D_MODEL), F32),
        grid_spec=grid_spec,
        compiler_params=_cparams(("arbitrary", "arbitrary")),
        name="moe_grouped",
    )(tile_expert, n_used, row_token, row_dest, xf, wg, wu, wd)


def _moe_combine_kernel(o0_ref, o1_ref, rt_ref, x_ref, lg_ref, lb_ref, of_ref, ob_ref):
    rt = rt_ref[...]
    f = rt[:, 2:3] * o0_ref[...] + rt[:, 3:4] * o1_ref[...]
    y = _layer_norm_rows(DN_ALPHA * x_ref[...] + f, lg_ref[...], lb_ref[...])
    of_ref[...] = y
    ob_ref[...] = y.astype(BF16)


def _moe_combine_ln(o_rows, route, xf, ln_g, ln_b):
    T = xf.shape[0]
    tm = 512
    nt = T // tm
    return pl.pallas_call(
        _moe_combine_kernel,
        out_shape=(jax.ShapeDtypeStruct((T, D_MODEL), F32), jax.ShapeDtypeStruct((T, D_MODEL), BF16)),
        grid=(nt,),
        in_specs=[pl.BlockSpec((tm, D_MODEL), lambda i: (i, 0)),
                  pl.BlockSpec((tm, D_MODEL), lambda i: (i + nt, 0)),
                  pl.BlockSpec((tm, LANES), lambda i: (i, 0)),
                  pl.BlockSpec((tm, D_MODEL), lambda i: (i, 0)),
                  pl.BlockSpec((1, D_MODEL), lambda i: (0, 0)),
                  pl.BlockSpec((1, D_MODEL), lambda i: (0, 0))],
        out_specs=(pl.BlockSpec((tm, D_MODEL), lambda i: (i, 0)),
                   pl.BlockSpec((tm, D_MODEL), lambda i: (i, 0))),
        compiler_params=_cparams(("parallel",)),
        name="moe_combine_ln",
    )(o_rows, o_rows, route, xf, ln_g, ln_b)


def _moe_plan(route, T):
    n_pairs = TOP_K * T
    ids = route[:, :TOP_K].astype(jnp.int32)
    e_flat = ids.reshape(-1)
    pair = jnp.arange(n_pairs, dtype=jnp.int32)
    order = jnp.sort(e_flat * n_pairs + pair) % n_pairs
    onehot = (e_flat[:, None] == jnp.arange(N_EXPERTS, dtype=jnp.int32)[None, :]).astype(jnp.int32)
    counts = jnp.sum(onehot, axis=0)
    first = jnp.cumsum(counts) - counts
    tiles_per = (counts + MOE_TM - 1) // MOE_TM
    tile_end = jnp.cumsum(tiles_per)
    tile_start = tile_end - tiles_per
    n_used = tile_end[-1]
    n_tiles = n_pairs // MOE_TM + N_EXPERTS
    tiles = jnp.arange(n_tiles, dtype=jnp.int32)
    tidx = jnp.minimum(tiles, n_used - 1)
    tile_expert = jnp.sum((tidx[:, None] >= tile_end[None, :]).astype(jnp.int32), axis=1)
    tile_expert = jnp.minimum(tile_expert, N_EXPERTS - 1).astype(jnp.int32)
    r_in = jnp.arange(MOE_TMX, dtype=jnp.int32)[None, :]
    k = (tiles - tile_start[tile_expert])[:, None] * MOE_TM + r_in
    valid = (r_in < MOE_TM) & (k < counts[tile_expert][:, None]) & (tiles < n_used)[:, None]
    src = jnp.clip(first[tile_expert][:, None] + k, 0, n_pairs - 1)
    p = order[src]
    spill = n_pairs + (tiles % 2)[:, None] * MOE_TMX + r_in
    row_token = jnp.where(valid, p // TOP_K, 0).reshape(-1)
    row_dest = jnp.where(valid, (p % TOP_K) * T + p // TOP_K, spill)
    lead = n_pairs + MOE_TMX + r_in
    row_dest = jnp.concatenate([lead, row_dest], axis=0).reshape(-1)
    return tile_expert, n_used.reshape(1).astype(jnp.int32), row_token, row_dest


def _prep_w_in(w_in):
    L = w_in.shape[0]
    z = lambda n: jnp.zeros((L, D_MODEL, n), w_in.dtype)
    o_swa = MLA_IN
    o_gdn = MLA_IN + SWA_IN
    hd = SWA_HEAD_DIM
    kr = w_in[:, :, MLA_Q_RANK + MLA_KV_RANK:MLA_IN]
    half = MLA_ROPE // 2
    kr_sw = jnp.concatenate([kr[:, :, half:], kr[:, :, :half]], axis=-1)
    ba = w_in[:, :, o_gdn + GDN_QKV:o_gdn + GDN_QKV + 2 * GDN_HEADS]
    segs = [
        w_in[:, :, :MLA_Q_RANK],
        w_in[:, :, o_swa:o_swa + SWA_HEADS * hd],
        w_in[:, :, o_gdn + GDN_QKV + 2 * GDN_HEADS:],
        w_in[:, :, o_gdn:o_gdn + GDN_QKV],
        w_in[:, :, MLA_Q_RANK:MLA_Q_RANK + MLA_KV_RANK],
        w_in[:, :, o_swa + SWA_HEADS * hd:o_swa + (SWA_HEADS + SWA_KV_HEADS) * hd],
        w_in[:, :, o_swa + (SWA_HEADS + SWA_KV_HEADS) * hd:o_swa + SWA_IN],
        kr, kr_sw, z(LANES - 2 * MLA_ROPE),
        ba, z(LANES - 2 * GDN_HEADS),
        z(D_PROJ - COL_BA - LANES),
    ]
    return jnp.concatenate([seg.astype(BF16) for seg in segs], axis=-1)


def _prep_mla_weights(w_uq, w_ukv):
    L = w_uq.shape[0]
    half = MLA_ROPE // 2
    wq = w_uq.reshape(L, MLA_Q_RANK, MLA_HEADS, MLA_NOPE + MLA_ROPE)
    nope, rope = wq[..., :MLA_NOPE], wq[..., MLA_NOPE:]
    rope_sw = jnp.concatenate([rope[..., half:], rope[..., :half]], axis=-1)
    zpad = jnp.zeros(wq.shape[:3] + (MLA_HP - MLA_NOPE - MLA_ROPE,), wq.dtype)
    wq_pad = jnp.concatenate([nope, rope, zpad], axis=-1).reshape(L, MLA_Q_RANK, MLA_HEADS * MLA_HP)
    wq_sw = jnp.concatenate([jnp.zeros_like(nope), rope_sw, zpad], axis=-1).reshape(L, MLA_Q_RANK, MLA_HEADS * MLA_HP)
    wkv = w_ukv.reshape(L, MLA_KV_RANK, MLA_HEADS, MLA_NOPE + MLA_V)
    knope, vv = wkv[..., :MLA_NOPE], wkv[..., MLA_NOPE:]
    wk_pad = jnp.concatenate([knope, jnp.zeros(knope.shape[:3] + (MLA_HP - MLA_NOPE,), wkv.dtype)], axis=-1)
    wk_pad = wk_pad.reshape(L, MLA_KV_RANK, MLA_HEADS * MLA_HP)
    wv = vv.reshape(L, MLA_KV_RANK, MLA_HEADS * MLA_V)
    return wq_pad.astype(BF16), wq_sw.astype(BF16), wk_pad.astype(BF16), wv.astype(BF16)


def _rope_tables(S):
    half = MLA_ROPE // 2
    pos = jnp.arange(S, dtype=jnp.int32)
    inv_freq = ROPE_BASE ** (-jnp.arange(half, dtype=F32) / half)
    ang = pos.astype(F32)[:, None] * inv_freq[None, :]
    cos, sin = jnp.cos(ang), jnp.sin(ang)
    cos2 = jnp.concatenate([cos, cos], axis=-1)
    sin2 = jnp.concatenate([-sin, sin], axis=-1)
    scale = (MLA_NOPE + MLA_ROPE) ** -0.5
    tail = jnp.zeros((S, MLA_HP - MLA_NOPE - MLA_ROPE), F32)
    cq_head = jnp.concatenate([jnp.ones((S, MLA_NOPE), F32), cos2, tail], axis=-1) * scale
    sq_head = jnp.concatenate([jnp.zeros((S, MLA_NOPE), F32), sin2, tail], axis=-1) * scale
    cqt = jnp.tile(cq_head, (1, MLA_HEADS))
    sqt = jnp.tile(sq_head, (1, MLA_HEADS))
    ckt = jnp.concatenate([cos2, sin2, jnp.zeros((S, LANES - 2 * MLA_ROPE), F32)], axis=-1)
    j = jnp.arange(LANES)[:, None]
    col = jnp.arange(MLA_HEADS * MLA_HP)[None, :]
    place = ((col % MLA_HP) == (MLA_NOPE + (j % MLA_ROPE))) & (j < 2 * MLA_ROPE)
    return cqt, sqt, ckt, place.astype(BF16)


def _pad_lanes(v, offset):
    out = jnp.zeros((1, LANES), F32)
    return lax.dynamic_update_slice(out, v.astype(F32).reshape(1, -1), (0, offset))


def kernel(x, w_in, mla_q_norm, mla_w_uq, mla_kv_norm, mla_w_ukv, swa_sinks, gdn_conv,
           gdn_a_log, gdn_dt_bias, gdn_norm, w_out, ln1_g, ln1_b, ln2_g, ln2_b,
           ffn_w_gate, ffn_w_up, ffn_w_down, moe_router, moe_w_gate, moe_w_up, moe_w_down):
    B, S, D = x.shape
    T = B * S
    xf = x.reshape(T, D)
    xb = xf.astype(BF16)

    w_in_b = _prep_w_in(w_in)
    wq_pad, wq_sw, wk_pad, wv = _prep_mla_weights(mla_w_uq, mla_w_ukv)
    cqt, sqt, ckt, place = _rope_tables(S)
    w_out_b = w_out.astype(BF16)
    ffn_g, ffn_u, ffn_d = ffn_w_gate.astype(BF16), ffn_w_up.astype(BF16), ffn_w_down.astype(BF16)
    moe_g, moe_u, moe_d = moe_w_gate.astype(BF16), moe_w_up.astype(BF16), moe_w_down.astype(BF16)
    router128 = jnp.pad(moe_router.astype(F32), ((0, 0), (0, 0), (0, LANES - N_EXPERTS)))

    for l in range(DEPTH):
        proj = _in_proj(xb, w_in_b, l)
        q, k, v = _mla_prep(proj, mla_q_norm[l].reshape(1, -1), mla_kv_norm[l].reshape(1, -1),
                            wq_pad, wq_sw, wk_pad, wv, l, place, cqt, sqt, ckt, S)
        h_mla = _mla_attn(q, k, v, B, S)
        h_swa = _swa(proj, swa_sinks[l].astype(F32), B, S)
        h_gdn = _gdn(proj, gdn_conv[l], _pad_lanes(gdn_a_log[l], GDN_HEADS), _pad_lanes(gdn_dt_bias[l], GDN_HEADS),
                     gdn_norm[l].reshape(1, -1), B, S)
        xf, xb = _out_proj_ln(h_mla, h_swa, h_gdn, xf, w_out_b, l,
                              ln1_g[l].reshape(1, -1), ln1_b[l].reshape(1, -1))
        lg, lb = ln2_g[l].reshape(1, -1), ln2_b[l].reshape(1, -1)
        if l % 2 == 0:
            xf, xb = _ffn_ln(xb, xf, ffn_g, ffn_u, ffn_d, l // 2, lg, lb)
        else:
            m = l // 2
            route = _router(xf, router128[m])
            tile_expert, n_used, row_token, row_dest = _moe_plan(route, T)
            o_rows = _moe_grouped(tile_expert, n_used, row_token, row_dest, xf, moe_g, moe_u, moe_d, m)
            xf, xb = _moe_combine_ln(o_rows, route, xf, lg, lb)
    return xf.reshape(B, S, D)
```

```python
import functools
import math

import jax
import jax.numpy as jnp
from jax import lax
from jax.experimental import pallas as pl
from jax.experimental.pallas import tpu as pltpu

F32 = jnp.float32
BF16 = jnp.bfloat16

D_MODEL = 2048
DEPTH = 4
MLA_HEADS = 8
MLA_Q_RANK = 512
MLA_KV_RANK = 256
MLA_NOPE = 64
MLA_ROPE = 32
MLA_V = 64
ROPE_BASE = 10000.0
SWA_HEADS = 8
SWA_KV_HEADS = 2
SWA_GROUP = SWA_HEADS // SWA_KV_HEADS
SWA_HEAD_DIM = 64
WINDOW = 128
GDN_HEADS = 8
GDN_DK = 128
GDN_DV = 128
CONV_WIDTH = 4
GDN_C = 256
GDN_HEAD_GROUP = 4
D_FF = 7168
N_EXPERTS = 8
TOP_K = 2
DN_ALPHA = float((2 * DEPTH) ** 0.25)
LN_EPS = 1e-5
RMS_EPS = 1e-6

MLA_IN = MLA_Q_RANK + MLA_KV_RANK + MLA_ROPE
SWA_IN = (SWA_HEADS + 2 * SWA_KV_HEADS) * SWA_HEAD_DIM
GDN_QKV = GDN_HEADS * (2 * GDN_DK + GDN_DV)

LANES = 128
NEG_BIG = -1e30
VMEM_LIMIT = 56 * 1024 * 1024

COL_CQ = 0
COL_SWAQ = 512
COL_Z = 1024
COL_GQ = 2048
COL_GK = 3072
COL_GV = 4096
COL_CKV = 5120
COL_SWAK = 5376
COL_SWAV = 5504
COL_KR = 5632
COL_BA = 5760
D_PROJ = 6144

MLA_HP = 128


def _cparams(sem, vmem=VMEM_LIMIT):
    return pltpu.CompilerParams(dimension_semantics=sem, vmem_limit_bytes=vmem)


def _layer_norm_rows(y, g, b):
    mu = jnp.mean(y, axis=-1, keepdims=True)
    d = y - mu
    var = jnp.mean(d * d, axis=-1, keepdims=True)
    return d * lax.rsqrt(var + LN_EPS) * g + b


def _sigmoid(x):
    return 1.0 / (1.0 + jnp.exp(-x))


def _silu(x):
    return x * _sigmoid(x)


def _dot(a, b):
    return jnp.dot(a, b, preferred_element_type=F32)


def _dot_nt(a, b):
    return lax.dot_general(a, b, (((1,), (1,)), ((), ())), preferred_element_type=F32)


def _matmul_kernel(x_ref, w_ref, o_ref):
    o_ref[...] = _dot(x_ref[...], w_ref[...])


def _in_proj(xb, w, l):
    T = xb.shape[0]
    tm, tn = 1024, 2048
    return pl.pallas_call(
        _matmul_kernel,
        out_shape=jax.ShapeDtypeStruct((T, D_PROJ), F32),
        grid=(T // tm, D_PROJ // tn),
        in_specs=[pl.BlockSpec((tm, D_MODEL), lambda i, j: (i, 0)),
                  pl.BlockSpec((None, D_MODEL, tn), lambda i, j: (l, 0, j))],
        out_specs=pl.BlockSpec((tm, tn), lambda i, j: (i, j)),
        compiler_params=_cparams(("parallel", "arbitrary")),
        name="in_proj",
    )(xb, w)


def _mla_prep_kernel(cq_ref, ckv_ref, kr_ref, gq_ref, gkv_ref, wq_ref, wqs_ref, wk_ref, wv_ref,
                     e_ref, cqt_ref, sqt_ref, ckt_ref, q_ref, k_ref, v_ref):
    cq = cq_ref[...]
    hq = cq * lax.rsqrt(jnp.mean(cq * cq, axis=-1, keepdims=True) + RMS_EPS) * gq_ref[...]
    hq = hq.astype(BF16)
    q = _dot(hq, wq_ref[...]) * cqt_ref[...] + _dot(hq, wqs_ref[...]) * sqt_ref[...]
    q_ref[...] = q.astype(BF16)
    ckv = ckv_ref[...]
    hk = ckv * lax.rsqrt(jnp.mean(ckv * ckv, axis=-1, keepdims=True) + RMS_EPS) * gkv_ref[...]
    hk = hk.astype(BF16)
    krp = kr_ref[...] * ckt_ref[...]
    krp_hi = krp.astype(BF16)
    krp_lo = (krp - krp_hi.astype(F32)).astype(BF16)
    e = e_ref[...]
    k = _dot(hk, wk_ref[...]) + (_dot(krp_hi, e) + _dot(krp_lo, e))
    for h in range(MLA_HEADS):
        k_ref[h * MLA_HP:(h + 1) * MLA_HP, :] = k[:, h * MLA_HP:(h + 1) * MLA_HP].T.astype(BF16)
    v_ref[...] = _dot(hk, wv_ref[...]).astype(BF16)


def _mla_prep(proj, gq, gkv, wq, wqs, wk, wv, l, e, cqt, sqt, ckt, S):
    T = proj.shape[0]
    tm = 512
    sb = S // tm
    full = lambda shape: pl.BlockSpec(shape, lambda i: (0, 0))
    layer = lambda shape: pl.BlockSpec((None,) + shape, lambda i: (l, 0, 0))
    return pl.pallas_call(
        _mla_prep_kernel,
        out_shape=(jax.ShapeDtypeStruct((T, MLA_HEADS * MLA_HP), BF16),
                   jax.ShapeDtypeStruct((T // S * MLA_HEADS * MLA_HP, S), BF16),
                   jax.ShapeDtypeStruct((T, MLA_HEADS * MLA_V), BF16)),
        grid=(T // tm,),
        in_specs=[pl.BlockSpec((tm, MLA_Q_RANK), lambda i: (i, COL_CQ // MLA_Q_RANK)),
                  pl.BlockSpec((tm, MLA_KV_RANK), lambda i: (i, COL_CKV // MLA_KV_RANK)),
                  pl.BlockSpec((tm, LANES), lambda i: (i, COL_KR // LANES)),
                  full((1, MLA_Q_RANK)), full((1, MLA_KV_RANK)),
                  layer((MLA_Q_RANK, MLA_HEADS * MLA_HP)), layer((MLA_Q_RANK, MLA_HEADS * MLA_HP)),
                  layer((MLA_KV_RANK, MLA_HEADS * MLA_HP)), layer((MLA_KV_RANK, MLA_HEADS * MLA_V)),
                  full((LANES, MLA_HEADS * MLA_HP)),
                  pl.BlockSpec((tm, MLA_HEADS * MLA_HP), lambda i: (i % sb, 0)),
                  pl.BlockSpec((tm, MLA_HEADS * MLA_HP), lambda i: (i % sb, 0)),
                  pl.BlockSpec((tm, LANES), lambda i: (i % sb, 0))],
        out_specs=(pl.BlockSpec((tm, MLA_HEADS * MLA_HP), lambda i: (i, 0)),
                   pl.BlockSpec((MLA_HEADS * MLA_HP, tm), lambda i: (i // sb, i % sb)),
                   pl.BlockSpec((tm, MLA_HEADS * MLA_V), lambda i: (i, 0))),
        compiler_params=_cparams(("parallel",)),
        name="mla_prep",
    )(proj, proj, proj, gq, gkv, wq, wqs, wk, wv, e, cqt, sqt, ckt)


MLA_G = 4


def _mla_attn_kernel(q_ref, k_ref, v_ref, o_ref, acc_sc, m_sc, l_sc, *, tq, tk):
    qi = pl.program_id(2)
    acc_sc[...] = jnp.zeros_like(acc_sc)
    m_sc[...] = jnp.full_like(m_sc, NEG_BIG)
    l_sc[...] = jnp.zeros_like(l_sc)
    ratio = tq // tk

    def step(kj, diag):
        rows = pl.ds(pl.multiple_of(kj * tk, tk), tk)
        for g in range(MLA_G):
            s = _dot(q_ref[:, g * MLA_HP:(g + 1) * MLA_HP], k_ref[g * MLA_HP:(g + 1) * MLA_HP, rows])
            if diag is not None:
                r = lax.broadcasted_iota(jnp.int32, s.shape, 0)
                c = lax.broadcasted_iota(jnp.int32, s.shape, 1)
                s = jnp.where(c + diag * tk <= r, s, NEG_BIG)
            m_prev = m_sc[g]
            m_new = jnp.maximum(m_prev, jnp.max(s, axis=1, keepdims=True))
            alpha = jnp.exp(m_prev - m_new)
            p = jnp.exp(s - m_new[:, :1])
            l_sc[g] = alpha * l_sc[g] + jnp.sum(p, axis=1, keepdims=True)
            pv = _dot(p.astype(BF16), v_ref[rows, (g // 2) * LANES:(g // 2 + 1) * LANES])
            acc_sc[g] = acc_sc[g] * alpha + pv
            m_sc[g] = m_new

    def body(kj, carry):
        step(kj, None)
        return carry

    lax.fori_loop(0, qi * ratio, body, 0)
    for d in range(ratio):
        step(qi * ratio + d, d)
    lane = lax.broadcasted_iota(jnp.int32, (tq, LANES), 1)
    for pr in range(MLA_G // 2):
        o0 = acc_sc[2 * pr] / l_sc[2 * pr]
        o1 = acc_sc[2 * pr + 1] / l_sc[2 * pr + 1]
        o_ref[:, pr * LANES:(pr + 1) * LANES] = jnp.where(lane < MLA_V, o0, o1).astype(o_ref.dtype)


def _mla_attn(q, k, v, B, S):
    T = q.shape[0]
    tq, tk = 512, 512
    nq = S // tq
    groups = MLA_HEADS // MLA_G
    return pl.pallas_call(
        functools.partial(_mla_attn_kernel, tq=tq, tk=tk),
        out_shape=jax.ShapeDtypeStruct((T, MLA_HEADS * MLA_V), BF16),
        grid=(B, groups, nq),
        in_specs=[pl.BlockSpec((tq, MLA_G * MLA_HP), lambda b, h, i: (b * nq + i, h)),
                  pl.BlockSpec((MLA_G * MLA_HP, S), lambda b, h, i: (b * groups + h, 0)),
                  pl.BlockSpec((S, MLA_G * MLA_V), lambda b, h, i: (b, h))],
        out_specs=pl.BlockSpec((tq, MLA_G * MLA_V), lambda b, h, i: (b * nq + i, h)),
        scratch_shapes=[pltpu.VMEM((MLA_G, tq, LANES), F32),
                        pltpu.VMEM((MLA_G, tq, LANES), F32),
                        pltpu.VMEM((MLA_G, tq, LANES), F32)],
        compiler_params=_cparams(("parallel", "parallel", "arbitrary")),
        name="mla_attn",
    )(q, k, v)


def _swa_kernel(sink_ref, q_ref, kc_ref, kp_ref, vc_ref, vp_ref, o_ref):
    n = pl.program_id(1)
    hd = SWA_HEAD_DIM
    qi = lax.broadcasted_iota(jnp.int32, (WINDOW, 2 * WINDOW), 0)
    kj = lax.broadcasted_iota(jnp.int32, (WINDOW, 2 * WINDOW), 1)
    delta = qi + WINDOW - kj
    first_key = jnp.where(n > 0, 0, WINDOW)
    valid = (delta >= 0) & (delta < WINDOW) & (kj >= first_key)
    deltaf = delta.astype(F32)
    for g in range(SWA_KV_HEADS):
        kcat = jnp.concatenate([kp_ref[:, g * hd:(g + 1) * hd], kc_ref[:, g * hd:(g + 1) * hd]], axis=0)
        vcat = jnp.concatenate([vp_ref[:, g * hd:(g + 1) * hd], vc_ref[:, g * hd:(g + 1) * hd]], axis=0)
        kcat = kcat.astype(BF16)
        vcat = vcat.astype(BF16)
        for r in range(SWA_GROUP):
            h = g * SWA_GROUP + r
            slope = 2.0 ** (-8.0 * (h + 1) / SWA_HEADS)
            q = q_ref[:, h * hd:(h + 1) * hd].astype(BF16)
            s = _dot_nt(q, kcat) * (hd ** -0.5) - slope * deltaf
            s = jnp.where(valid, s, NEG_BIG)
            sink = sink_ref[h]
            m = jnp.maximum(jnp.max(s, axis=1, keepdims=True), sink)
            p = jnp.exp(s - m)
            denom = jnp.sum(p, axis=1, keepdims=True) + jnp.exp(sink - m)
            o = _dot(p.astype(BF16), vcat) / denom
            o_ref[:, h * hd:(h + 1) * hd] = o.astype(o_ref.dtype)


def _swa(proj, sinks, B, S):
    T = proj.shape[0]
    nb = S // WINDOW
    cq = COL_SWAQ // (SWA_HEADS * SWA_HEAD_DIM)
    ck = COL_SWAK // LANES
    cv = COL_SWAV // LANES
    cur = lambda c: (lambda b, n: (b * nb + n, c))
    prev = lambda c: (lambda b, n: (b * nb + jnp.maximum(n - 1, 0), c))
    return pl.pallas_call(
        _swa_kernel,
        out_shape=jax.ShapeDtypeStruct((T, SWA_HEADS * SWA_HEAD_DIM), BF16),
        grid=(B, nb),
        in_specs=[pl.BlockSpec(memory_space=pltpu.SMEM),
                  pl.BlockSpec((WINDOW, SWA_HEADS * SWA_HEAD_DIM), cur(cq)),
                  pl.BlockSpec((WINDOW, LANES), cur(ck)),
                  pl.BlockSpec((WINDOW, LANES), prev(ck)),
                  pl.BlockSpec((WINDOW, LANES), cur(cv)),
                  pl.BlockSpec((WINDOW, LANES), prev(cv))],
        out_specs=pl.BlockSpec((WINDOW, SWA_HEADS * SWA_HEAD_DIM), lambda b, n: (b * nb + n, 0)),
        compiler_params=_cparams(("parallel", "arbitrary")),
        name="swa_attn",
    )(sinks, proj, proj, proj, proj, proj)


def _conv_silu(cur, prev8, w, sub):
    acc = cur * w[CONV_WIDTH - 1:CONV_WIDTH, :]
    for d in range(1, CONV_WIDTH):
        rolled = pltpu.roll(cur, d, axis=0)
        head = jnp.where(sub < d, pltpu.roll(prev8, d, axis=0), rolled[:8])
        shifted = jnp.concatenate([head, rolled[8:]], axis=0)
        acc = acc + shifted * w[CONV_WIDTH - 1 - d:CONV_WIDTH - d, :]
    return _silu(acc)


def _l2_normalize(y):
    return y * lax.rsqrt(jnp.sum(y * y, axis=-1, keepdims=True) + RMS_EPS)


def _gdn_chunk_kernel(xq_ref, xk_ref, xv_ref, ba_ref, z_ref, cw_ref, alog_ref, dtb_ref, ng_ref, o_ref,
                      state_sc, tail_sc):
    @pl.when(pl.program_id(1) == 0)
    def _():
        state_sc[...] = jnp.zeros_like(state_sc)
        tail_sc[...] = jnp.zeros_like(tail_sc)

    C = GDN_C
    H = GDN_HEADS
    W = H * GDN_DK
    ri = lax.broadcasted_iota(jnp.int32, (C, C), 0)
    ci = lax.broadcasted_iota(jnp.int32, (C, C), 1)
    incl = ri >= ci
    strict = ri > ci
    merge = jnp.where(strict, ri ^ ci, 2 * C)
    ng = ng_ref[...]
    sub = lax.broadcasted_iota(jnp.int32, (8, LANES), 0)

    ba = ba_ref[...]
    lane = lax.broadcasted_iota(jnp.int32, (C, LANES), 1)
    pos = lax.broadcasted_iota(jnp.int32, (C, LANES), 0)
    t = ba + dtb_ref[...]
    g = -jnp.exp(alog_ref[...]) * (jnp.maximum(t, 0.0) + jnp.log1p(jnp.exp(-jnp.abs(t))))
    step = 1
    while step < C:
        g = g + jnp.where(pos >= step, pltpu.roll(g, step, axis=0), 0.0)
        step *= 2
    gt = jnp.where(lane < H, _sigmoid(ba), g)
    gtt = gt.T

    def run(group):
        lmat, a_qk, rhs, q_dec, k_dec_t, c_decay = [], [], [], [], [], []
        for h in group:
            cols = slice(h * GDN_DK, (h + 1) * GDN_DK)
            q = _l2_normalize(_conv_silu(xq_ref[:, cols], tail_sc[0, :, cols], cw_ref[:, h * GDN_DK:(h + 1) * GDN_DK],
                                         sub)) * (GDN_DK ** -0.5)
            k = _l2_normalize(_conv_silu(xk_ref[:, cols], tail_sc[1, :, cols],
                                         cw_ref[:, W + h * GDN_DK:W + (h + 1) * GDN_DK], sub))
            v = _conv_silu(xv_ref[:, cols], tail_sc[2, :, cols],
                           cw_ref[:, 2 * W + h * GDN_DV:2 * W + (h + 1) * GDN_DV], sub)
            beta = gt[:, h:h + 1]
            gc = gt[:, H + h:H + h + 1]
            gr = gtt[H + h:H + h + 1, :]
            glast = gc[C - 1:C, :]
            decay = jnp.exp(jnp.where(incl, gc - gr, NEG_BIG))
            eg = jnp.exp(gc)
            kb = k * beta
            a1 = _dot_nt(jnp.concatenate([kb, q], axis=0).astype(BF16), k.astype(BF16))
            lmat.append(a1[:C] * jnp.where(strict, decay, 0.0))
            a_qk.append((a1[C:] * decay).astype(BF16))
            rhs.append(jnp.concatenate([v * beta, kb * eg], axis=1))
            q_dec.append((q * eg).astype(BF16))
            k_dec_t.append((k * jnp.exp(glast - gc)).T.astype(BF16))
            c_decay.append(jnp.exp(glast))

        base = merge < 16
        mpow = [jnp.where(base, l, 0.0) for l in lmat]
        rinv = [-m for m in mpow]
        for _ in range(3):
            mb = [m.astype(BF16) for m in mpow]
            mpow = [_dot(m, m) for m in mb]
            rinv = [r + m + _dot(r.astype(BF16), m.astype(BF16)) for r, m in zip(rinv, mpow)]
        s = 16
        while s < C:
            level = (merge >= s) & (merge < 2 * s)
            off = [jnp.where(level, l, 0.0) for l in lmat]
            y = [c + _dot(r.astype(BF16), c.astype(BF16)) for r, c in zip(rinv, off)]
            rinv = [r - yy - _dot(yy.astype(BF16), r.astype(BF16)) for r, yy in zip(rinv, y)]
            s *= 2

        sol = [x + _dot(r.astype(BF16), x.astype(BF16)) for r, x in zip(rinv, rhs)]
        idx = range(len(group))
        a2 = [_dot(jnp.concatenate([sol[i][:, GDN_DV:].astype(BF16), q_dec[i]], axis=0),
                   state_sc[group[i]].astype(BF16)) for i in idx]
        v_new = [(sol[i][:, :GDN_DV] - a2[i][:C]).astype(BF16) for i in idx]
        out = [a2[i][C:] + _dot(a_qk[i], v_new[i]) for i in idx]
        for i in idx:
            state_sc[group[i]] = state_sc[group[i]] * c_decay[i] + _dot(k_dec_t[i], v_new[i])
        for i in idx:
            h = group[i]
            cols = slice(h * GDN_DV, (h + 1) * GDN_DV)
            o = out[i]
            on = o * lax.rsqrt(jnp.mean(o * o, axis=-1, keepdims=True) + RMS_EPS) * ng
            o_ref[:, cols] = (on * _silu(z_ref[:, cols])).astype(o_ref.dtype)

    for first in range(0, H, GDN_HEAD_GROUP):
        run(list(range(first, first + GDN_HEAD_GROUP)))
    tail_sc[0] = xq_ref[C - 8:C, :]
    tail_sc[1] = xk_ref[C - 8:C, :]
    tail_sc[2] = xv_ref[C - 8:C, :]


def _gdn(proj, conv_w, alog128, dtb128, norm_g, B, S):
    T = proj.shape[0]
    R = GDN_C
    nr = S // R
    W = GDN_HEADS * GDN_DK
    row = lambda c: (lambda b, r: (b * nr + r, c))
    full = lambda shape: pl.BlockSpec(shape, lambda b, r: (0, 0))
    return pl.pallas_call(
        _gdn_chunk_kernel,
        out_shape=jax.ShapeDtypeStruct((T, GDN_HEADS * GDN_DV), BF16),
        grid=(B, nr),
        in_specs=[pl.BlockSpec((R, W), row(COL_GQ // W)),
                  pl.BlockSpec((R, W), row(COL_GK // W)),
                  pl.BlockSpec((R, W), row(COL_GV // W)),
                  pl.BlockSpec((R, LANES), row(COL_BA // LANES)),
                  pl.BlockSpec((R, W), row(COL_Z // W)),
                  full((CONV_WIDTH, GDN_QKV)), full((1, LANES)), full((1, LANES)), full((1, GDN_DV))],
        out_specs=pl.BlockSpec((R, W), row(0)),
        scratch_shapes=[pltpu.VMEM((GDN_HEADS, GDN_DK, GDN_DV), F32),
                        pltpu.VMEM((3, 8, W), F32)],
        compiler_params=_cparams(("parallel", "arbitrary")),
        name="gdn",
    )(proj, proj, proj, proj, proj, conv_w, alog128, dtb128, norm_g)


def _out_proj_kernel(a_ref, s_ref, g_ref, x_ref, wa_ref, ws_ref, wg_ref, lg_ref, lb_ref, of_ref, ob_ref):
    rows = 128
    for r in range(0, x_ref.shape[0], rows):
        sl = slice(r, r + rows)
        y = _dot(a_ref[sl, :], wa_ref[...]) + _dot(s_ref[sl, :], ws_ref[...]) + _dot(g_ref[sl, :], wg_ref[...])
        y = _layer_norm_rows(DN_ALPHA * x_ref[sl, :] + y, lg_ref[...], lb_ref[...])
        of_ref[sl, :] = y
        ob_ref[sl, :] = y.astype(BF16)


def _out_proj_ln(mla, swa, gdn, x, w_out, l, ln_g, ln_b):
    T = x.shape[0]
    tm = 512
    n_mla, n_swa, n_gdn = mla.shape[1], swa.shape[1], gdn.shape[1]
    assert n_mla % n_swa == 0 and (n_mla + n_swa) % n_gdn == 0
    row = lambda w: pl.BlockSpec((tm, w), lambda i: (i, 0))
    full = lambda shape: pl.BlockSpec(shape, lambda i: (0, 0))
    return pl.pallas_call(
        _out_proj_kernel,
        out_shape=(jax.ShapeDtypeStruct((T, D_MODEL), F32), jax.ShapeDtypeStruct((T, D_MODEL), BF16)),
        grid=(T // tm,),
        in_specs=[row(n_mla), row(n_swa), row(n_gdn), row(D_MODEL),
                  pl.BlockSpec((None, n_mla, D_MODEL), lambda i: (l, 0, 0)),
                  pl.BlockSpec((None, n_swa, D_MODEL), lambda i: (l, n_mla // n_swa, 0)),
                  pl.BlockSpec((None, n_gdn, D_MODEL), lambda i: (l, (n_mla + n_swa) // n_gdn, 0)),
                  full((1, D_MODEL)), full((1, D_MODEL))],
        out_specs=(row(D_MODEL), row(D_MODEL)),
        compiler_params=_cparams(("parallel",)),
        name="out_proj_ln",
    )(mla, swa, gdn, x, w_out, w_out, w_out, ln_g, ln_b)


def _ffn_kernel(xb_ref, xf_ref, wg_ref, wu_ref, wd_ref, lg_ref, lb_ref, of_ref, ob_ref, acc_sc):
    j = pl.program_id(1)

    @pl.when(j == 0)
    def _():
        acc_sc[...] = jnp.zeros_like(acc_sc)

    xb = xb_ref[...]
    h = _silu(_dot(xb, wg_ref[...])) * _dot(xb, wu_ref[...])
    acc_sc[...] += _dot(h.astype(BF16), wd_ref[...])

    @pl.when(j == pl.num_programs(1) - 1)
    def _():
        y = _layer_norm_rows(DN_ALPHA * xf_ref[...] + acc_sc[...], lg_ref[...], lb_ref[...])
        of_ref[...] = y
        ob_ref[...] = y.astype(BF16)


def _ffn_ln(xb, xf, wg, wu, wd, m, ln_g, ln_b):
    T = xb.shape[0]
    tm, tf = 512, 512
    return pl.pallas_call(
        _ffn_kernel,
        out_shape=(jax.ShapeDtypeStruct((T, D_MODEL), F32), jax.ShapeDtypeStruct((T, D_MODEL), BF16)),
        grid=(T // tm, D_FF // tf),
        in_specs=[pl.BlockSpec((tm, D_MODEL), lambda i, j: (i, 0)),
                  pl.BlockSpec((tm, D_MODEL), lambda i, j: (i, 0)),
                  pl.BlockSpec((None, D_MODEL, tf), lambda i, j: (m, 0, j)),
                  pl.BlockSpec((None, D_MODEL, tf), lambda i, j: (m, 0, j)),
                  pl.BlockSpec((None, tf, D_MODEL), lambda i, j: (m, j, 0)),
                  pl.BlockSpec((1, D_MODEL), lambda i, j: (0, 0)),
                  pl.BlockSpec((1, D_MODEL), lambda i, j: (0, 0))],
        out_specs=(pl.BlockSpec((tm, D_MODEL), lambda i, j: (i, 0)),
                   pl.BlockSpec((tm, D_MODEL), lambda i, j: (i, 0))),
        scratch_shapes=[pltpu.VMEM((tm, D_MODEL), F32)],
        compiler_params=_cparams(("parallel", "arbitrary")),
        name="ffn_ln",
    )(xb, xf, wg, wu, wd, ln_g, ln_b)


def _router_kernel(x_ref, w_ref, o_ref):
    x = x_ref[...]
    w = w_ref[...]
    xh = x.astype(BF16)
    wh = w.astype(BF16)
    xl = (x - xh.astype(F32)).astype(BF16)
    wl = (w - wh.astype(F32)).astype(BF16)
    logits = _dot(xh, wh) + (_dot(xl, wh) + _dot(xh, wl))
    lane = lax.broadcasted_iota(jnp.int32, logits.shape, 1).astype(F32)
    logits = jnp.where(lane < N_EXPERTS, logits, -jnp.inf)
    m1 = jnp.max(logits, axis=1, keepdims=True)
    i1 = jnp.min(jnp.where(logits == m1, lane, float(LANES)), axis=1, keepdims=True)
    rest = jnp.where(lane == i1, -jnp.inf, logits)
    m2 = jnp.max(rest, axis=1, keepdims=True)
    i2 = jnp.min(jnp.where(rest == m2, lane, float(LANES)), axis=1, keepdims=True)
    e = jnp.exp(m2 - m1)
    g1 = 1.0 / (1.0 + e)
    g2 = e / (1.0 + e)
    out = jnp.where(lane == 0, i1,
                    jnp.where(lane == 1, i2,
                              jnp.where(lane == 2, g1, jnp.where(lane == 3, g2, 0.0))))
    o_ref[...] = out


def _router(xf, w128):
    T = xf.shape[0]
    tm = 512
    return pl.pallas_call(
        _router_kernel,
        out_shape=jax.ShapeDtypeStruct((T, LANES), F32),
        grid=(T // tm,),
        in_specs=[pl.BlockSpec((tm, D_MODEL), lambda i: (i, 0)),
                  pl.BlockSpec((D_MODEL, LANES), lambda i: (0, 0))],
        out_specs=pl.BlockSpec((tm, LANES), lambda i: (i, 0)),
        compiler_params=_cparams(("parallel",)),
        name="moe_router",
    )(xf, w128)


MOE_TM = 512
MOE_TF = 1024
MOE_NJ = D_FF // MOE_TF
MOE_RPS = -(-MOE_TM // MOE_NJ)
MOE_MOVED = MOE_RPS * MOE_NJ
MOE_TMX = -(-MOE_MOVED // 8) * 8


def _moe_kernel(te_ref, nu_ref, tok_ref, dst_ref, x_hbm, wg_ref, wu_ref, wd_ref, o_hbm,
                xg_sc, xb_sc, acc_sc, gsem, ssem, *, n_tokens):
    i = pl.program_id(0)
    j = pl.program_id(1)
    n_used = nu_ref[0]
    active = i < n_used

    def gather_row(tile, s, q):
        pltpu.make_async_copy(x_hbm.at[tok_ref[tile * MOE_TMX + q]], xg_sc.at[s, q], gsem.at[s]).start()

    def scatter_row(seg, s, q):
        pltpu.make_async_copy(acc_sc.at[s, q], o_hbm.at[dst_ref[seg * MOE_TMX + q]], ssem.at[s]).start()

    def gather_wait(s):
        pltpu.make_async_copy(x_hbm.at[pl.ds(0, MOE_TM)], xg_sc.at[s, pl.ds(0, MOE_TM)], gsem.at[s]).wait()
        for q in range(MOE_TM, MOE_MOVED):
            pltpu.make_async_copy(x_hbm.at[0], xg_sc.at[s, q], gsem.at[s]).wait()

    def scatter_wait(s):
        pltpu.make_async_copy(acc_sc.at[s, pl.ds(0, MOE_TM)], o_hbm.at[pl.ds(0, MOE_TM)], ssem.at[s]).wait()
        for q in range(MOE_TM, MOE_MOVED):
            pltpu.make_async_copy(acc_sc.at[s, q], o_hbm.at[0], ssem.at[s]).wait()

    def run(slot):
        other = 1 - slot

        @pl.when(j == 0)
        def _():
            @pl.when(i == 0)
            def _():
                def issue(q, carry):
                    gather_row(0, 0, q)
                    return carry
                lax.fori_loop(0, MOE_MOVED, issue, 0, unroll=2)
                acc_sc[...] = jnp.zeros_like(acc_sc)
                for s in range(2):
                    spill = o_hbm.at[pl.ds(TOP_K * n_tokens + s * MOE_TMX, MOE_TMX)]
                    cp = pltpu.make_async_copy(acc_sc.at[0], spill, ssem.at[0])
                    cp.start()
                    cp.wait()

            @pl.when(i > 0)
            def _():
                scatter_wait(slot)
                acc_sc[slot, pl.ds(0, MOE_TM), :] = jnp.zeros((MOE_TM, D_MODEL), F32)

            gather_wait(slot)
            xb_sc[...] = xg_sc[slot, pl.ds(0, MOE_TM), :].astype(BF16)

        nxt = jnp.minimum(i + 1, n_used - 1)
        for r in range(MOE_RPS):
            scatter_row(i, other, j * MOE_RPS + r)
            gather_row(nxt, other, j * MOE_RPS + r)
        xb = xb_sc[...]
        h = _silu(_dot(xb, wg_ref[...])) * _dot(xb, wu_ref[...])
        acc_sc[slot, pl.ds(0, MOE_TM), :] += _dot(h.astype(BF16), wd_ref[...])

        @pl.when((j == MOE_NJ - 1) & (i == n_used - 1))
        def _():
            gather_wait(other)
            scatter_wait(other)

            def issue(q, carry):
                scatter_row(i + 1, slot, q)
                return carry
            lax.fori_loop(0, MOE_MOVED, issue, 0, unroll=2)
            scatter_wait(slot)

    for sl in range(2):
        pl.when(active & (lax.rem(i, 2) == sl))(functools.partial(run, sl))


def _moe_grouped(tile_expert, n_used, row_token, row_dest, xf, wg, wu, wd, m):
    n_tiles = tile_expert.shape[0]
    n_tokens = xf.shape[0]

    def wcol(i, j, te, nu, tok, dst):
        jj = jnp.where(i < nu[0], j, MOE_NJ - 1)
        return (m, te[i], 0, jj)

    def wrow(i, j, te, nu, tok, dst):
        jj = jnp.where(i < nu[0], j, MOE_NJ - 1)
        return (m, te[i], jj, 0)

    grid_spec = pltpu.PrefetchScalarGridSpec(
        num_scalar_prefetch=4,
        grid=(n_tiles, MOE_NJ),
        in_specs=[pl.BlockSpec(memory_space=pl.ANY),
                  pl.BlockSpec((None, None, D_MODEL, MOE_TF), wcol),
                  pl.BlockSpec((None, None, D_MODEL, MOE_TF), wcol),
                  pl.BlockSpec((None, None, MOE_TF, D_MODEL), wrow)],
        out_specs=pl.BlockSpec(memory_space=pl.ANY),
        scratch_shapes=[pltpu.VMEM((2, MOE_TMX, D_MODEL), F32),
                        pltpu.VMEM((MOE_TM, D_MODEL), BF16),
                        pltpu.VMEM((2, MOE_TMX, D_MODEL), F32),
                        pltpu.SemaphoreType.DMA((2,)),
                        pltpu.SemaphoreType.DMA((2,))],
    )
    return pl.pallas_call(
        functools.partial(_moe_kernel, n_tokens=n_tokens),
        out_shape=jax.ShapeDtypeStruct((TOP_K * n_tokens + 2 * MOE_TMX, D_MODEL), F32),
        grid_spec=grid_spec,
        compiler_params=_cparams(("arbitrary", "arbitrary")),
        name="moe_grouped",
    )(tile_expert, n_used, row_token, row_dest, xf, wg, wu, wd)


def _moe_combine_kernel(o0_ref, o1_ref, rt_ref, x_ref, lg_ref, lb_ref, of_ref, ob_ref):
    rt = rt_ref[...]
    f = rt[:, 2:3] * o0_ref[...] + rt[:, 3:4] * o1_ref[...]
    y = _layer_norm_rows(DN_ALPHA * x_ref[...] + f, lg_ref[...], lb_ref[...])
    of_ref[...] = y
    ob_ref[...] = y.astype(BF16)


def _moe_combine_ln(o_rows, route, xf, ln_g, ln_b):
    T = xf.shape[0]
    tm = 512
    nt = T // tm
    return pl.pallas_call(
        _moe_combine_kernel,
        out_shape=(jax.ShapeDtypeStruct((T, D_MODEL), F32), jax.ShapeDtypeStruct((T, D_MODEL), BF16)),
        grid=(nt,),
        in_specs=[pl.BlockSpec((tm, D_MODEL), lambda i: (i, 0)),
                  pl.BlockSpec((tm, D_MODEL), lambda i: (i + nt, 0)),
                  pl.BlockSpec((tm, LANES), lambda i: (i, 0)),
                  pl.BlockSpec((tm, D_MODEL), lambda i: (i, 0)),
                  pl.BlockSpec((1, D_MODEL), lambda i: (0, 0)),
                  pl.BlockSpec((1, D_MODEL), lambda i: (0, 0))],
        out_specs=(pl.BlockSpec((tm, D_MODEL), lambda i: (i, 0)),
                   pl.BlockSpec((tm, D_MODEL), lambda i: (i, 0))),
        compiler_params=_cparams(("parallel",)),
        name="moe_combine_ln",
    )(o_rows, o_rows, route, xf, ln_g, ln_b)


def _moe_plan(route, T):
    n_pairs = TOP_K * T
    ids = route[:, :TOP_K].astype(jnp.int32)
    e_flat = ids.reshape(-1)
    pair = jnp.arange(n_pairs, dtype=jnp.int32)
    order = jnp.sort(e_flat * n_pairs + pair) % n_pairs
    onehot = (e_flat[:, None] == jnp.arange(N_EXPERTS, dtype=jnp.int32)[None, :]).astype(jnp.int32)
    counts = jnp.sum(onehot, axis=0)
    first = jnp.cumsum(counts) - counts
    tiles_per = (counts + MOE_TM - 1) // MOE_TM
    tile_end = jnp.cumsum(tiles_per)
    tile_start = tile_end - tiles_per
    n_used = tile_end[-1]
    n_tiles = n_pairs // MOE_TM + N_EXPERTS
    tiles = jnp.arange(n_tiles, dtype=jnp.int32)
    tidx = jnp.minimum(tiles, n_used - 1)
    tile_expert = jnp.sum((tidx[:, None] >= tile_end[None, :]).astype(jnp.int32), axis=1)
    tile_expert = jnp.minimum(tile_expert, N_EXPERTS - 1).astype(jnp.int32)
    r_in = jnp.arange(MOE_TMX, dtype=jnp.int32)[None, :]
    k = (tiles - tile_start[tile_expert])[:, None] * MOE_TM + r_in
    valid = (r_in < MOE_TM) & (k < counts[tile_expert][:, None]) & (tiles < n_used)[:, None]
    src = jnp.clip(first[tile_expert][:, None] + k, 0, n_pairs - 1)
    p = order[src]
    spill = n_pairs + (tiles % 2)[:, None] * MOE_TMX + r_in
    row_token = jnp.where(valid, p // TOP_K, 0).reshape(-1)
    row_dest = jnp.where(valid, (p % TOP_K) * T + p // TOP_K, spill)
    lead = n_pairs + MOE_TMX + r_in
    row_dest = jnp.concatenate([lead, row_dest], axis=0).reshape(-1)
    return tile_expert, n_used.reshape(1).astype(jnp.int32), row_token, row_dest


def _prep_w_in(w_in):
    L = w_in.shape[0]
    z = lambda n: jnp.zeros((L, D_MODEL, n), w_in.dtype)
    o_swa = MLA_IN
    o_gdn = MLA_IN + SWA_IN
    hd = SWA_HEAD_DIM
    kr = w_in[:, :, MLA_Q_RANK + MLA_KV_RANK:MLA_IN]
    half = MLA_ROPE // 2
    kr_sw = jnp.concatenate([kr[:, :, half:], kr[:, :, :half]], axis=-1)
    ba = w_in[:, :, o_gdn + GDN_QKV:o_gdn + GDN_QKV + 2 * GDN_HEADS]
    segs = [
        w_in[:, :, :MLA_Q_RANK],
        w_in[:, :, o_swa:o_swa + SWA_HEADS * hd],
        w_in[:, :, o_gdn + GDN_QKV + 2 * GDN_HEADS:],
        w_in[:, :, o_gdn:o_gdn + GDN_QKV],
        w_in[:, :, MLA_Q_RANK:MLA_Q_RANK + MLA_KV_RANK],
        w_in[:, :, o_swa + SWA_HEADS * hd:o_swa + (SWA_HEADS + SWA_KV_HEADS) * hd],
        w_in[:, :, o_swa + (SWA_HEADS + SWA_KV_HEADS) * hd:o_swa + SWA_IN],
        kr, kr_sw, z(LANES - 2 * MLA_ROPE),
        ba, z(LANES - 2 * GDN_HEADS),
        z(D_PROJ - COL_BA - LANES),
    ]
    return jnp.concatenate([seg.astype(BF16) for seg in segs], axis=-1)


def _prep_mla_weights(w_uq, w_ukv):
    L = w_uq.shape[0]
    half = MLA_ROPE // 2
    wq = w_uq.reshape(L, MLA_Q_RANK, MLA_HEADS, MLA_NOPE + MLA_ROPE)
    nope, rope = wq[..., :MLA_NOPE], wq[..., MLA_NOPE:]
    rope_sw = jnp.concatenate([rope[..., half:], rope[..., :half]], axis=-1)
    zpad = jnp.zeros(wq.shape[:3] + (MLA_HP - MLA_NOPE - MLA_ROPE,), wq.dtype)
    wq_pad = jnp.concatenate([nope, rope, zpad], axis=-1).reshape(L, MLA_Q_RANK, MLA_HEADS * MLA_HP)
    wq_sw = jnp.concatenate([jnp.zeros_like(nope), rope_sw, zpad], axis=-1).reshape(L, MLA_Q_RANK, MLA_HEADS * MLA_HP)
    wkv = w_ukv.reshape(L, MLA_KV_RANK, MLA_HEADS, MLA_NOPE + MLA_V)
    knope, vv = wkv[..., :MLA_NOPE], wkv[..., MLA_NOPE:]
    wk_pad = jnp.concatenate([knope, jnp.zeros(knope.shape[:3] + (MLA_HP - MLA_NOPE,), wkv.dtype)], axis=-1)
    wk_pad = wk_pad.reshape(L, MLA_KV_RANK, MLA_HEADS * MLA_HP)
    wv = vv.reshape(L, MLA_KV_RANK, MLA_HEADS * MLA_V)
    return wq_pad.astype(BF16), wq_sw.astype(BF16), wk_pad.astype(BF16), wv.astype(BF16)


def _rope_tables(S):
    half = MLA_ROPE // 2
    pos = jnp.arange(S, dtype=jnp.int32)
    inv_freq = ROPE_BASE ** (-jnp.arange(half, dtype=F32) / half)
    ang = pos.astype(F32)[:, None] * inv_freq[None, :]
    cos, sin = jnp.cos(ang), jnp.sin(ang)
    cos2 = jnp.concatenate([cos, cos], axis=-1)
    sin2 = jnp.concatenate([-sin, sin], axis=-1)
    scale = (MLA_NOPE + MLA_ROPE) ** -0.5
    tail = jnp.zeros((S, MLA_HP - MLA_NOPE - MLA_ROPE), F32)
    cq_head = jnp.concatenate([jnp.ones((S, MLA_NOPE), F32), cos2, tail], axis=-1) * scale
    sq_head = jnp.concatenate([jnp.zeros((S, MLA_NOPE), F32), sin2, tail], axis=-1) * scale
    cqt = jnp.tile(cq_head, (1, MLA_HEADS))
    sqt = jnp.tile(sq_head, (1, MLA_HEADS))
    ckt = jnp.concatenate([cos2, sin2, jnp.zeros((S, LANES - 2 * MLA_ROPE), F32)], axis=-1)
    j = jnp.arange(LANES)[:, None]
    col = jnp.arange(MLA_HEADS * MLA_HP)[None, :]
    place = ((col % MLA_HP) == (MLA_NOPE + (j % MLA_ROPE))) & (j < 2 * MLA_ROPE)
    return cqt, sqt, ckt, place.astype(BF16)


def _pad_lanes(v, offset):
    out = jnp.zeros((1, LANES), F32)
    return lax.dynamic_update_slice(out, v.astype(F32).reshape(1, -1), (0, offset))


def kernel(x, w_in, mla_q_norm, mla_w_uq, mla_kv_norm, mla_w_ukv, swa_sinks, gdn_conv,
           gdn_a_log, gdn_dt_bias, gdn_norm, w_out, ln1_g, ln1_b, ln2_g, ln2_b,
           ffn_w_gate, ffn_w_up, ffn_w_down, moe_router, moe_w_gate, moe_w_up, moe_w_down):
    B, S, D = x.shape
    T = B * S
    xf = x.reshape(T, D)
    xb = xf.astype(BF16)

    w_in_b = _prep_w_in(w_in)
    wq_pad, wq_sw, wk_pad, wv = _prep_mla_weights(mla_w_uq, mla_w_ukv)
    cqt, sqt, ckt, place = _rope_tables(S)
    w_out_b = w_out.astype(BF16)
    ffn_g, ffn_u, ffn_d = ffn_w_gate.astype(BF16), ffn_w_up.astype(BF16), ffn_w_down.astype(BF16)
    moe_g, moe_u, moe_d = moe_w_gate.astype(BF16), moe_w_up.astype(BF16), moe_w_down.astype(BF16)
    router128 = jnp.pad(moe_router.astype(F32), ((0, 0), (0, 0), (0, LANES - N_EXPERTS)))

    for l in range(DEPTH):
        proj = _in_proj(xb, w_in_b, l)
        q, k, v = _mla_prep(proj, mla_q_norm[l].reshape(1, -1), mla_kv_norm[l].reshape(1, -1),
                            wq_pad, wq_sw, wk_pad, wv, l, place, cqt, sqt, ckt, S)
        h_mla = _mla_attn(q, k, v, B, S)
        h_swa = _swa(proj, swa_sinks[l].astype(F32), B, S)
        h_gdn = _gdn(proj, gdn_conv[l], _pad_lanes(gdn_a_log[l], GDN_HEADS), _pad_lanes(gdn_dt_bias[l], GDN_HEADS),
                     gdn_norm[l].reshape(1, -1), B, S)
        xf, xb = _out_proj_ln(h_mla, h_swa, h_gdn, xf, w_out_b, l,
                              ln1_g[l].reshape(1, -1), ln1_b[l].reshape(1, -1))
        lg, lb = ln2_g[l].reshape(1, -1), ln2_b[l].reshape(1, -1)
        if l % 2 == 0:
            xf, xb = _ffn_ln(xb, xf, ffn_g, ffn_u, ffn_d, l // 2, lg, lb)
        else:
            m = l // 2
            route = _router(xf, router128[m])
            tile_expert, n_used, row_token, row_dest = _moe_plan(route, T)
            o_rows = _moe_grouped(tile_expert, n_used, row_token, row_dest, xf, moe_g, moe_u, moe_d, m)
            xf, xb = _moe_combine_ln(o_rows, route, xf, lg, lb)
    return xf.reshape(B, S, D)
```

```python
import functools
import math

import jax
import jax.numpy as jnp
from jax import lax
from jax.experimental import pallas as pl
from jax.experimental.pallas import tpu as pltpu

F32 = jnp.float32
BF16 = jnp.bfloat16

D_MODEL = 2048
DEPTH = 4
MLA_HEADS = 8
MLA_Q_RANK = 512
MLA_KV_RANK = 256
MLA_NOPE = 64
MLA_ROPE = 32
MLA_V = 64
ROPE_BASE = 10000.0
SWA_HEADS = 8
SWA_KV_HEADS = 2
SWA_GROUP = SWA_HEADS // SWA_KV_HEADS
SWA_HEAD_DIM = 64
WINDOW = 128
GDN_HEADS = 8
GDN_DK = 128
GDN_DV = 128
CONV_WIDTH = 4
GDN_C = 256
GDN_HEAD_GROUP = 4
D_FF = 7168
N_EXPERTS = 8
TOP_K = 2
DN_ALPHA = float((2 * DEPTH) ** 0.25)
LN_EPS = 1e-5
RMS_EPS = 1e-6

MLA_IN = MLA_Q_RANK + MLA_KV_RANK + MLA_ROPE
SWA_IN = (SWA_HEADS + 2 * SWA_KV_HEADS) * SWA_HEAD_DIM
GDN_QKV = GDN_HEADS * (2 * GDN_DK + GDN_DV)

LANES = 128
NEG_BIG = -1e30
VMEM_LIMIT = 56 * 1024 * 1024

COL_CQ = 0
COL_SWAQ = 512
COL_Z = 1024
COL_GQ = 2048
COL_GK = 3072
COL_GV = 4096
COL_CKV = 5120
COL_SWAK = 5376
COL_SWAV = 5504
COL_KR = 5632
COL_BA = 5760
D_PROJ = 6144

MLA_HP = 128


def _cparams(sem, vmem=VMEM_LIMIT):
    return pltpu.CompilerParams(dimension_semantics=sem, vmem_limit_bytes=vmem)


def _layer_norm_rows(y, g, b):
    mu = jnp.mean(y, axis=-1, keepdims=True)
    d = y - mu
    var = jnp.mean(d * d, axis=-1, keepdims=True)
    return d * lax.rsqrt(var + LN_EPS) * g + b


def _sigmoid(x):
    return 1.0 / (1.0 + jnp.exp(-x))


def _silu(x):
    return x * _sigmoid(x)


def _dot(a, b):
    return jnp.dot(a, b, preferred_element_type=F32)


def _dot_nt(a, b):
    return lax.dot_general(a, b, (((1,), (1,)), ((), ())), preferred_element_type=F32)


def _matmul_kernel(x_ref, w_ref, o_ref):
    o_ref[...] = _dot(x_ref[...], w_ref[...])


def _in_proj(xb, w, l):
    T = xb.shape[0]
    tm, tn = 1024, 2048
    return pl.pallas_call(
        _matmul_kernel,
        out_shape=jax.ShapeDtypeStruct((T, D_PROJ), F32),
        grid=(T // tm, D_PROJ // tn),
        in_specs=[pl.BlockSpec((tm, D_MODEL), lambda i, j: (i, 0)),
                  pl.BlockSpec((None, D_MODEL, tn), lambda i, j: (l, 0, j))],
        out_specs=pl.BlockSpec((tm, tn), lambda i, j: (i, j)),
        compiler_params=_cparams(("parallel", "arbitrary")),
        name="in_proj",
    )(xb, w)


def _mla_prep_kernel(cq_ref, ckv_ref, kr_ref, gq_ref, gkv_ref, wq_ref, wqs_ref, wk_ref, wv_ref,
                     e_ref, cqt_ref, sqt_ref, ckt_ref, q_ref, k_ref, v_ref):
    cq = cq_ref[...]
    hq = cq * lax.rsqrt(jnp.mean(cq * cq, axis=-1, keepdims=True) + RMS_EPS) * gq_ref[...]
    hq = hq.astype(BF16)
    q = _dot(hq, wq_ref[...]) * cqt_ref[...] + _dot(hq, wqs_ref[...]) * sqt_ref[...]
    q_ref[...] = q.astype(BF16)
    ckv = ckv_ref[...]
    hk = ckv * lax.rsqrt(jnp.mean(ckv * ckv, axis=-1, keepdims=True) + RMS_EPS) * gkv_ref[...]
    hk = hk.astype(BF16)
    krp = kr_ref[...] * ckt_ref[...]
    krp_hi = krp.astype(BF16)
    krp_lo = (krp - krp_hi.astype(F32)).astype(BF16)
    e = e_ref[...]
    k = _dot(hk, wk_ref[...]) + (_dot(krp_hi, e) + _dot(krp_lo, e))
    for h in range(MLA_HEADS):
        k_ref[h * MLA_HP:(h + 1) * MLA_HP, :] = k[:, h * MLA_HP:(h + 1) * MLA_HP].T.astype(BF16)
    v_ref[...] = _dot(hk, wv_ref[...]).astype(BF16)


def _mla_prep(proj, gq, gkv, wq, wqs, wk, wv, l, e, cqt, sqt, ckt, S):
    T = proj.shape[0]
    tm = 512
    sb = S // tm
    full = lambda shape: pl.BlockSpec(shape, lambda i: (0, 0))
    layer = lambda shape: pl.BlockSpec((None,) + shape, lambda i: (l, 0, 0))
    return pl.pallas_call(
        _mla_prep_kernel,
        out_shape=(jax.ShapeDtypeStruct((T, MLA_HEADS * MLA_HP), BF16),
                   jax.ShapeDtypeStruct((T // S * MLA_HEADS * MLA_HP, S), BF16),
                   jax.ShapeDtypeStruct((T, MLA_HEADS * MLA_V), BF16)),
        grid=(T // tm,),
        in_specs=[pl.BlockSpec((tm, MLA_Q_RANK), lambda i: (i, COL_CQ // MLA_Q_RANK)),
                  pl.BlockSpec((tm, MLA_KV_RANK), lambda i: (i, COL_CKV // MLA_KV_RANK)),
                  pl.BlockSpec((tm, LANES), lambda i: (i, COL_KR // LANES)),
                  full((1, MLA_Q_RANK)), full((1, MLA_KV_RANK)),
                  layer((MLA_Q_RANK, MLA_HEADS * MLA_HP)), layer((MLA_Q_RANK, MLA_HEADS * MLA_HP)),
                  layer((MLA_KV_RANK, MLA_HEADS * MLA_HP)), layer((MLA_KV_RANK, MLA_HEADS * MLA_V)),
                  full((LANES, MLA_HEADS * MLA_HP)),
                  pl.BlockSpec((tm, MLA_HEADS * MLA_HP), lambda i: (i % sb, 0)),
                  pl.BlockSpec((tm, MLA_HEADS * MLA_HP), lambda i: (i % sb, 0)),
                  pl.BlockSpec((tm, LANES), lambda i: (i % sb, 0))],
        out_specs=(pl.BlockSpec((tm, MLA_HEADS * MLA_HP), lambda i: (i, 0)),
                   pl.BlockSpec((MLA_HEADS * MLA_HP, tm), lambda i: (i // sb, i % sb)),
                   pl.BlockSpec((tm, MLA_HEADS * MLA_V), lambda i: (i, 0))),
        compiler_params=_cparams(("parallel",)),
        name="mla_prep",
    )(proj, proj, proj, gq, gkv, wq, wqs, wk, wv, e, cqt, sqt, ckt)


MLA_G = 4


def _mla_attn_kernel(q_ref, k_ref, v_ref, o_ref, acc_sc, m_sc, l_sc, *, tq, tk):
    qi = pl.program_id(2)
    acc_sc[...] = jnp.zeros_like(acc_sc)
    m_sc[...] = jnp.full_like(m_sc, NEG_BIG)
    l_sc[...] = jnp.zeros_like(l_sc)
    ratio = tq // tk

    def step(kj, diag):
        rows = pl.ds(pl.multiple_of(kj * tk, tk), tk)
        for g in range(MLA_G):
            s = _dot(q_ref[:, g * MLA_HP:(g + 1) * MLA_HP], k_ref[g * MLA_HP:(g + 1) * MLA_HP, rows])
            if diag is not None:
                r = lax.broadcasted_iota(jnp.int32, s.shape, 0)
                c = lax.broadcasted_iota(jnp.int32, s.shape, 1)
                s = jnp.where(c + diag * tk <= r, s, NEG_BIG)
            m_prev = m_sc[g]
            m_new = jnp.maximum(m_prev, jnp.max(s, axis=1, keepdims=True))
            alpha = jnp.exp(m_prev - m_new)
            p = jnp.exp(s - m_new[:, :1])
            l_sc[g] = alpha * l_sc[g] + jnp.sum(p, axis=1, keepdims=True)
            pv = _dot(p.astype(BF16), v_ref[rows, (g // 2) * LANES:(g // 2 + 1) * LANES])
            acc_sc[g] = acc_sc[g] * alpha + pv
            m_sc[g] = m_new

    def body(kj, carry):
        step(kj, None)
        return carry

    lax.fori_loop(0, qi * ratio, body, 0)
    for d in range(ratio):
        step(qi * ratio + d, d)
    lane = lax.broadcasted_iota(jnp.int32, (tq, LANES), 1)
    for pr in range(MLA_G // 2):
        o0 = acc_sc[2 * pr] / l_sc[2 * pr]
        o1 = acc_sc[2 * pr + 1] / l_sc[2 * pr + 1]
        o_ref[:, pr * LANES:(pr + 1) * LANES] = jnp.where(lane < MLA_V, o0, o1).astype(o_ref.dtype)


def _mla_attn(q, k, v, B, S):
    T = q.shape[0]
    tq, tk = 512, 512
    nq = S // tq
    groups = MLA_HEADS // MLA_G
    return pl.pallas_call(
        functools.partial(_mla_attn_kernel, tq=tq, tk=tk),
        out_shape=jax.ShapeDtypeStruct((T, MLA_HEADS * MLA_V), BF16),
        grid=(B, groups, nq),
        in_specs=[pl.BlockSpec((tq, MLA_G * MLA_HP), lambda b, h, i: (b * nq + i, h)),
                  pl.BlockSpec((MLA_G * MLA_HP, S), lambda b, h, i: (b * groups + h, 0)),
                  pl.BlockSpec((S, MLA_G * MLA_V), lambda b, h, i: (b, h))],
        out_specs=pl.BlockSpec((tq, MLA_G * MLA_V), lambda b, h, i: (b * nq + i, h)),
        scratch_shapes=[pltpu.VMEM((MLA_G, tq, LANES), F32),
                        pltpu.VMEM((MLA_G, tq, LANES), F32),
                        pltpu.VMEM((MLA_G, tq, LANES), F32)],
        compiler_params=_cparams(("parallel", "parallel", "arbitrary")),
        name="mla_attn",
    )(q, k, v)


def _swa_kernel(sink_ref, q_ref, kc_ref, kp_ref, vc_ref, vp_ref, o_ref):
    n = pl.program_id(1)
    hd = SWA_HEAD_DIM
    qi = lax.broadcasted_iota(jnp.int32, (WINDOW, 2 * WINDOW), 0)
    kj = lax.broadcasted_iota(jnp.int32, (WINDOW, 2 * WINDOW), 1)
    delta = qi + WINDOW - kj
    first_key = jnp.where(n > 0, 0, WINDOW)
    valid = (delta >= 0) & (delta < WINDOW) & (kj >= first_key)
    deltaf = delta.astype(F32)
    kcat, vcat = [], []
    for g in range(SWA_KV_HEADS):
        kcat.append(jnp.concatenate([kp_ref[:, g * hd:(g + 1) * hd], kc_ref[:, g * hd:(g + 1) * hd]],
                                    axis=0).astype(BF16))
        vcat.append(jnp.concatenate([vp_ref[:, g * hd:(g + 1) * hd], vc_ref[:, g * hd:(g + 1) * hd]],
                                    axis=0).astype(BF16))
    heads = range(SWA_HEADS)
    s = [_dot_nt(q_ref[:, h * hd:(h + 1) * hd].astype(BF16), kcat[h // SWA_GROUP]) for h in heads]
    s = [jnp.where(valid, x * (hd ** -0.5) - (2.0 ** (-8.0 * (h + 1) / SWA_HEADS)) * deltaf, NEG_BIG)
         for x, h in zip(s, heads)]
    m = [jnp.maximum(jnp.max(x, axis=1, keepdims=True), sink_ref[h]) for x, h in zip(s, heads)]
    p = [jnp.exp(x - mm) for x, mm in zip(s, m)]
    denom = [jnp.sum(pp, axis=1, keepdims=True) + jnp.exp(sink_ref[h] - mm) for pp, mm, h in zip(p, m, heads)]
    o = [_dot(pp.astype(BF16), vcat[h // SWA_GROUP]) / dd for pp, dd, h in zip(p, denom, heads)]
    for oo, h in zip(o, heads):
        o_ref[:, h * hd:(h + 1) * hd] = oo.astype(o_ref.dtype)


def _swa(proj, sinks, B, S):
    T = proj.shape[0]
    nb = S // WINDOW
    cq = COL_SWAQ // (SWA_HEADS * SWA_HEAD_DIM)
    ck = COL_SWAK // LANES
    cv = COL_SWAV // LANES
    cur = lambda c: (lambda b, n: (b * nb + n, c))
    prev = lambda c: (lambda b, n: (b * nb + jnp.maximum(n - 1, 0), c))
    return pl.pallas_call(
        _swa_kernel,
        out_shape=jax.ShapeDtypeStruct((T, SWA_HEADS * SWA_HEAD_DIM), BF16),
        grid=(B, nb),
        in_specs=[pl.BlockSpec(memory_space=pltpu.SMEM),
                  pl.BlockSpec((WINDOW, SWA_HEADS * SWA_HEAD_DIM), cur(cq)),
                  pl.BlockSpec((WINDOW, LANES), cur(ck)),
                  pl.BlockSpec((WINDOW, LANES), prev(ck)),
                  pl.BlockSpec((WINDOW, LANES), cur(cv)),
                  pl.BlockSpec((WINDOW, LANES), prev(cv))],
        out_specs=pl.BlockSpec((WINDOW, SWA_HEADS * SWA_HEAD_DIM), lambda b, n: (b * nb + n, 0)),
        compiler_params=_cparams(("parallel", "arbitrary")),
        name="swa_attn",
    )(sinks, proj, proj, proj, proj, proj)


def _conv_silu(cur, prev8, w, sub):
    acc = cur * w[CONV_WIDTH - 1:CONV_WIDTH, :]
    for d in range(1, CONV_WIDTH):
        rolled = pltpu.roll(cur, d, axis=0)
        head = jnp.where(sub < d, pltpu.roll(prev8, d, axis=0), rolled[:8])
        shifted = jnp.concatenate([head, rolled[8:]], axis=0)
        acc = acc + shifted * w[CONV_WIDTH - 1 - d:CONV_WIDTH - d, :]
    return _silu(acc)


def _l2_normalize(y):
    return y * lax.rsqrt(jnp.sum(y * y, axis=-1, keepdims=True) + RMS_EPS)


def _gdn_chunk_kernel(xq_ref, xk_ref, xv_ref, ba_ref, z_ref, cw_ref, alog_ref, dtb_ref, ng_ref, o_ref,
                      state_sc, tail_sc):
    @pl.when(pl.program_id(1) == 0)
    def _():
        state_sc[...] = jnp.zeros_like(state_sc)
        tail_sc[...] = jnp.zeros_like(tail_sc)

    C = GDN_C
    H = GDN_HEADS
    W = H * GDN_DK
    ri = lax.broadcasted_iota(jnp.int32, (C, C), 0)
    ci = lax.broadcasted_iota(jnp.int32, (C, C), 1)
    incl = ri >= ci
    strict = ri > ci
    merge = jnp.where(strict, ri ^ ci, 2 * C)
    ng = ng_ref[...]
    sub = lax.broadcasted_iota(jnp.int32, (8, LANES), 0)

    ba = ba_ref[...]
    lane = lax.broadcasted_iota(jnp.int32, (C, LANES), 1)
    pos = lax.broadcasted_iota(jnp.int32, (C, LANES), 0)
    t = ba + dtb_ref[...]
    g = -jnp.exp(alog_ref[...]) * (jnp.maximum(t, 0.0) + jnp.log1p(jnp.exp(-jnp.abs(t))))
    step = 1
    while step < C:
        g = g + jnp.where(pos >= step, pltpu.roll(g, step, axis=0), 0.0)
        step *= 2
    gt = jnp.where(lane < H, _sigmoid(ba), g)
    gtt = gt.T

    def run(group):
        lmat, a_qk, rhs, q_dec, k_dec_t, c_decay = [], [], [], [], [], []
        for h in group:
            cols = slice(h * GDN_DK, (h + 1) * GDN_DK)
            q = _l2_normalize(_conv_silu(xq_ref[:, cols], tail_sc[0, :, cols], cw_ref[:, h * GDN_DK:(h + 1) * GDN_DK],
                                         sub)) * (GDN_DK ** -0.5)
            k = _l2_normalize(_conv_silu(xk_ref[:, cols], tail_sc[1, :, cols],
                                         cw_ref[:, W + h * GDN_DK:W + (h + 1) * GDN_DK], sub))
            v = _conv_silu(xv_ref[:, cols], tail_sc[2, :, cols],
                           cw_ref[:, 2 * W + h * GDN_DV:2 * W + (h + 1) * GDN_DV], sub)
            beta = gt[:, h:h + 1]
            gc = gt[:, H + h:H + h + 1]
            gr = gtt[H + h:H + h + 1, :]
            glast = gc[C - 1:C, :]
            decay = jnp.exp(jnp.where(incl, gc - gr, NEG_BIG))
            eg = jnp.exp(gc)
            kb = k * beta
            a1 = _dot_nt(jnp.concatenate([kb, q], axis=0).astype(BF16), k.astype(BF16))
            lmat.append(a1[:C] * jnp.where(strict, decay, 0.0))
            a_qk.append((a1[C:] * decay).astype(BF16))
            rhs.append(jnp.concatenate([v * beta, kb * eg], axis=1))
            q_dec.append((q * eg).astype(BF16))
            k_dec_t.append((k * jnp.exp(glast - gc)).T.astype(BF16))
            c_decay.append(jnp.exp(glast))

        base = merge < 16
        mpow = [jnp.where(base, l, 0.0) for l in lmat]
        rinv = [-m for m in mpow]
        for _ in range(3):
            mb = [m.astype(BF16) for m in mpow]
            mpow = [_dot(m, m) for m in mb]
            rinv = [r + m + _dot(r.astype(BF16), m.astype(BF16)) for r, m in zip(rinv, mpow)]
        s = 16
        while s < C:
            level = (merge >= s) & (merge < 2 * s)
            off = [jnp.where(level, l, 0.0) for l in lmat]
            y = [c + _dot(r.astype(BF16), c.astype(BF16)) for r, c in zip(rinv, off)]
            rinv = [r - yy - _dot(yy.astype(BF16), r.astype(BF16)) for r, yy in zip(rinv, y)]
            s *= 2

        sol = [x + _dot(r.astype(BF16), x.astype(BF16)) for r, x in zip(rinv, rhs)]
        idx = range(len(group))
        a2 = [_dot(jnp.concatenate([sol[i][:, GDN_DV:].astype(BF16), q_dec[i]], axis=0),
                   state_sc[group[i]].astype(BF16)) for i in idx]
        v_new = [(sol[i][:, :GDN_DV] - a2[i][:C]).astype(BF16) for i in idx]
        out = [a2[i][C:] + _dot(a_qk[i], v_new[i]) for i in idx]
        for i in idx:
            state_sc[group[i]] = state_sc[group[i]] * c_decay[i] + _dot(k_dec_t[i], v_new[i])
        for i in idx:
            h = group[i]
            cols = slice(h * GDN_DV, (h + 1) * GDN_DV)
            o = out[i]
            on = o * lax.rsqrt(jnp.mean(o * o, axis=-1, keepdims=True) + RMS_EPS) * ng
            o_ref[:, cols] = (on * _silu(z_ref[:, cols])).astype(o_ref.dtype)

    for first in range(0, H, GDN_HEAD_GROUP):
        run(list(range(first, first + GDN_HEAD_GROUP)))
    tail_sc[0] = xq_ref[C - 8:C, :]
    tail_sc[1] = xk_ref[C - 8:C, :]
    tail_sc[2] = xv_ref[C - 8:C, :]


def _gdn(proj, conv_w, alog128, dtb128, norm_g, B, S):
    T = proj.shape[0]
    R = GDN_C
    nr = S // R
    W = GDN_HEADS * GDN_DK
    row = lambda c: (lambda b, r: (b * nr + r, c))
    full = lambda shape: pl.BlockSpec(shape, lambda b, r: (0, 0))
    return pl.pallas_call(
        _gdn_chunk_kernel,
        out_shape=jax.ShapeDtypeStruct((T, GDN_HEADS * GDN_DV), BF16),
        grid=(B, nr),
        in_specs=[pl.BlockSpec((R, W), row(COL_GQ // W)),
                  pl.BlockSpec((R, W), row(COL_GK // W)),
                  pl.BlockSpec((R, W), row(COL_GV // W)),
                  pl.BlockSpec((R, LANES), row(COL_BA // LANES)),
                  pl.BlockSpec((R, W), row(COL_Z // W)),
                  full((CONV_WIDTH, GDN_QKV)), full((1, LANES)), full((1, LANES)), full((1, GDN_DV))],
        out_specs=pl.BlockSpec((R, W), row(0)),
        scratch_shapes=[pltpu.VMEM((GDN_HEADS, GDN_DK, GDN_DV), F32),
                        pltpu.VMEM((3, 8, W), F32)],
        compiler_params=_cparams(("parallel", "arbitrary")),
        name="gdn",
    )(proj, proj, proj, proj, proj, conv_w, alog128, dtb128, norm_g)


def _out_proj_kernel(a_ref, s_ref, g_ref, x_ref, wa_ref, ws_ref, wg_ref, lg_ref, lb_ref, of_ref, ob_ref):
    rows = 128
    for r in range(0, x_ref.shape[0], rows):
        sl = slice(r, r + rows)
        y = _dot(a_ref[sl, :], wa_ref[...]) + _dot(s_ref[sl, :], ws_ref[...]) + _dot(g_ref[sl, :], wg_ref[...])
        y = _layer_norm_rows(DN_ALPHA * x_ref[sl, :] + y, lg_ref[...], lb_ref[...])
        of_ref[sl, :] = y
        ob_ref[sl, :] = y.astype(BF16)


def _out_proj_ln(mla, swa, gdn, x, w_out, l, ln_g, ln_b):
    T = x.shape[0]
    tm = 512
    n_mla, n_swa, n_gdn = mla.shape[1], swa.shape[1], gdn.shape[1]
    assert n_mla % n_swa == 0 and (n_mla + n_swa) % n_gdn == 0
    row = lambda w: pl.BlockSpec((tm, w), lambda i: (i, 0))
    full = lambda shape: pl.BlockSpec(shape, lambda i: (0, 0))
    return pl.pallas_call(
        _out_proj_kernel,
        out_shape=(jax.ShapeDtypeStruct((T, D_MODEL), F32), jax.ShapeDtypeStruct((T, D_MODEL), BF16)),
        grid=(T // tm,),
        in_specs=[row(n_mla), row(n_swa), row(n_gdn), row(D_MODEL),
                  pl.BlockSpec((None, n_mla, D_MODEL), lambda i: (l, 0, 0)),
                  pl.BlockSpec((None, n_swa, D_MODEL), lambda i: (l, n_mla // n_swa, 0)),
                  pl.BlockSpec((None, n_gdn, D_MODEL), lambda i: (l, (n_mla + n_swa) // n_gdn, 0)),
                  full((1, D_MODEL)), full((1, D_MODEL))],
        out_specs=(row(D_MODEL), row(D_MODEL)),
        compiler_params=_cparams(("parallel",)),
        name="out_proj_ln",
    )(mla, swa, gdn, x, w_out, w_out, w_out, ln_g, ln_b)


def _ffn_kernel(xb_ref, xf_ref, wg_ref, wu_ref, wd_ref, lg_ref, lb_ref, of_ref, ob_ref, acc_sc):
    j = pl.program_id(1)

    @pl.when(j == 0)
    def _():
        acc_sc[...] = jnp.zeros_like(acc_sc)

    xb = xb_ref[...]
    h = _silu(_dot(xb, wg_ref[...])) * _dot(xb, wu_ref[...])
    acc_sc[...] += _dot(h.astype(BF16), wd_ref[...])

    @pl.when(j == pl.num_programs(1) - 1)
    def _():
        y = _layer_norm_rows(DN_ALPHA * xf_ref[...] + acc_sc[...], lg_ref[...], lb_ref[...])
        of_ref[...] = y
        ob_ref[...] = y.astype(BF16)


def _ffn_ln(xb, xf, wg, wu, wd, m, ln_g, ln_b):
    T = xb.shape[0]
    tm, tf = 512, 512
    return pl.pallas_call(
        _ffn_kernel,
        out_shape=(jax.ShapeDtypeStruct((T, D_MODEL), F32), jax.ShapeDtypeStruct((T, D_MODEL), BF16)),
        grid=(T // tm, D_FF // tf),
        in_specs=[pl.BlockSpec((tm, D_MODEL), lambda i, j: (i, 0)),
                  pl.BlockSpec((tm, D_MODEL), lambda i, j: (i, 0)),
                  pl.BlockSpec((None, D_MODEL, tf), lambda i, j: (m, 0, j)),
                  pl.BlockSpec((None, D_MODEL, tf), lambda i, j: (m, 0, j)),
                  pl.BlockSpec((None, tf, D_MODEL), lambda i, j: (m, j, 0)),
                  pl.BlockSpec((1, D_MODEL), lambda i, j: (0, 0)),
                  pl.BlockSpec((1, D_MODEL), lambda i, j: (0, 0))],
        out_specs=(pl.BlockSpec((tm, D_MODEL), lambda i, j: (i, 0)),
                   pl.BlockSpec((tm, D_MODEL), lambda i, j: (i, 0))),
        scratch_shapes=[pltpu.VMEM((tm, D_MODEL), F32)],
        compiler_params=_cparams(("parallel", "arbitrary")),
        name="ffn_ln",
    )(xb, xf, wg, wu, wd, ln_g, ln_b)


def _router_kernel(x_ref, w_ref, o_ref):
    x = x_ref[...]
    w = w_ref[...]
    xh = x.astype(BF16)
    wh = w.astype(BF16)
    xl = (x - xh.astype(F32)).astype(BF16)
    wl = (w - wh.astype(F32)).astype(BF16)
    logits = _dot(xh, wh) + (_dot(xl, wh) + _dot(xh, wl))
    lane = lax.broadcasted_iota(jnp.int32, logits.shape, 1).astype(F32)
    logits = jnp.where(lane < N_EXPERTS, logits, -jnp.inf)
    m1 = jnp.max(logits, axis=1, keepdims=True)
    i1 = jnp.min(jnp.where(logits == m1, lane, float(LANES)), axis=1, keepdims=True)
    rest = jnp.where(lane == i1, -jnp.inf, logits)
    m2 = jnp.max(rest, axis=1, keepdims=True)
    i2 = jnp.min(jnp.where(rest == m2, lane, float(LANES)), axis=1, keepdims=True)
    e = jnp.exp(m2 - m1)
    g1 = 1.0 / (1.0 + e)
    g2 = e / (1.0 + e)
    out = jnp.where(lane == 0, i1,
                    jnp.where(lane == 1, i2,
                              jnp.where(lane == 2, g1, jnp.where(lane == 3, g2, 0.0))))
    o_ref[...] = out


def _router(xf, w128):
    T = xf.shape[0]
    tm = 512
    return pl.pallas_call(
        _router_kernel,
        out_shape=jax.ShapeDtypeStruct((T, LANES), F32),
        grid=(T // tm,),
        in_specs=[pl.BlockSpec((tm, D_MODEL), lambda i: (i, 0)),
                  pl.BlockSpec((D_MODEL, LANES), lambda i: (0, 0))],
        out_specs=pl.BlockSpec((tm, LANES), lambda i: (i, 0)),
        compiler_params=_cparams(("parallel",)),
        name="moe_router",
    )(xf, w128)


MOE_TM = 512
MOE_TF = 1024
MOE_NJ = D_FF // MOE_TF
MOE_RPS = -(-MOE_TM // MOE_NJ)
MOE_MOVED = MOE_RPS * MOE_NJ
MOE_TMX = -(-MOE_MOVED // 8) * 8


def _moe_kernel(te_ref, nu_ref, tok_ref, dst_ref, x_hbm, wg_ref, wu_ref, wd_ref, o_hbm,
                xg_sc, xb_sc, acc_sc, gsem, ssem, *, n_tokens):
    i = pl.program_id(0)
    j = pl.program_id(1)
    n_used = nu_ref[0]
    active = i < n_used

    def gather_row(tile, s, q):
        pltpu.make_async_copy(x_hbm.at[tok_ref[tile * MOE_TMX + q]], xg_sc.at[s, q], gsem.at[s]).start()

    def scatter_row(seg, s, q):
        pltpu.make_async_copy(acc_sc.at[s, q], o_hbm.at[dst_ref[seg * MOE_TMX + q]], ssem.at[s]).start()

    def gather_wait(s):
        pltpu.make_async_copy(x_hbm.at[pl.ds(0, MOE_TM)], xg_sc.at[s, pl.ds(0, MOE_TM)], gsem.at[s]).wait()
        for q in range(MOE_TM, MOE_MOVED):
            pltpu.make_async_copy(x_hbm.at[0], xg_sc.at[s, q], gsem.at[s]).wait()

    def scatter_wait(s):
        pltpu.make_async_copy(acc_sc.at[s, pl.ds(0, MOE_TM)], o_hbm.at[pl.ds(0, MOE_TM)], ssem.at[s]).wait()
        for q in range(MOE_TM, MOE_MOVED):
            pltpu.make_async_copy(acc_sc.at[s, q], o_hbm.at[0], ssem.at[s]).wait()

    def run(slot):
        other = 1 - slot

        @pl.when(j == 0)
        def _():
            @pl.when(i == 0)
            def _():
                def issue(q, carry):
                    gather_row(0, 0, q)
                    return carry
                lax.fori_loop(0, MOE_MOVED, issue, 0, unroll=2)
                acc_sc[...] = jnp.zeros_like(acc_sc)
                for s in range(2):
                    spill = o_hbm.at[pl.ds(TOP_K * n_tokens + s * MOE_TMX, MOE_TMX)]
                    cp = pltpu.make_async_copy(acc_sc.at[0], spill, ssem.at[0])
                    cp.start()
                    cp.wait()

            @pl.when(i > 0)
            def _():
                scatter_wait(slot)
                acc_sc[slot, pl.ds(0, MOE_TM), :] = jnp.zeros((MOE_TM, D_MODEL), F32)

            gather_wait(slot)
            xb_sc[...] = xg_sc[slot, pl.ds(0, MOE_TM), :].astype(BF16)

        nxt = jnp.minimum(i + 1, n_used - 1)
        for r in range(MOE_RPS):
            scatter_row(i, other, j * MOE_RPS + r)
            gather_row(nxt, other, j * MOE_RPS + r)
        xb = xb_sc[...]
        h = _silu(_dot(xb, wg_ref[...])) * _dot(xb, wu_ref[...])
        acc_sc[slot, pl.ds(0, MOE_TM), :] += _dot(h.astype(BF16), wd_ref[...])

        @pl.when((j == MOE_NJ - 1) & (i == n_used - 1))
        def _():
            gather_wait(other)
            scatter_wait(other)

            def issue(q, carry):
                scatter_row(i + 1, slot, q)
                return carry
            lax.fori_loop(0, MOE_MOVED, issue, 0, unroll=2)
            scatter_wait(slot)

    for sl in range(2):
        pl.when(active & (lax.rem(i, 2) == sl))(functools.partial(run, sl))


def _moe_grouped(tile_expert, n_used, row_token, row_dest, xf, wg, wu, wd, m):
    n_tiles = tile_expert.shape[0]
    n_tokens = xf.shape[0]

    def wcol(i, j, te, nu, tok, dst):
        jj = jnp.where(i < nu[0], j, MOE_NJ - 1)
        return (m, te[i], 0, jj)

    def wrow(i, j, te, nu, tok, dst):
        jj = jnp.where(i < nu[0], j, MOE_NJ - 1)
        return (m, te[i], jj, 0)

    grid_spec = pltpu.PrefetchScalarGridSpec(
        num_scalar_prefetch=4,
        grid=(n_tiles, MOE_NJ),
        in_specs=[pl.BlockSpec(memory_space=pl.ANY),
                  pl.BlockSpec((None, None, D_MODEL, MOE_TF), wcol),
                  pl.BlockSpec((None, None, D_MODEL, MOE_TF), wcol),
                  pl.BlockSpec((None, None, MOE_TF, D_MODEL), wrow)],
        out_specs=pl.BlockSpec(memory_space=pl.ANY),
        scratch_shapes=[pltpu.VMEM((2, MOE_TMX, D_MODEL), F32),
                        pltpu.VMEM((MOE_TM, D_MODEL), BF16),
                        pltpu.VMEM((2, MOE_TMX, D_MODEL), F32),
                        pltpu.SemaphoreType.DMA((2,)),
                        pltpu.SemaphoreType.DMA((2,))],
    )
    return pl.pallas_call(
        functools.partial(_moe_kernel, n_tokens=n_tokens),
        out_shape=jax.ShapeDtypeStruct((TOP_K * n_tokens + 2 * MOE_TMX, D_MODEL), F32),
        grid_spec=grid_spec,
        compiler_params=_cparams(("arbitrary", "arbitrary")),
        name="moe_grouped",
    )(tile_expert, n_used, row_token, row_dest, xf, wg, wu, wd)


def _moe_combine_kernel(o0_ref, o1_ref, rt_ref, x_ref, lg_ref, lb_ref, of_ref, ob_ref):
    rt = rt_ref[...]
    f = rt[:, 2:3] * o0_ref[...] + rt[:, 3:4] * o1_ref[...]
    y = _layer_norm_rows(DN_ALPHA * x_ref[...] + f, lg_ref[...], lb_ref[...])
    of_ref[...] = y
    ob_ref[...] = y.astype(BF16)


def _moe_combine_ln(o_rows, route, xf, ln_g, ln_b):
    T = xf.shape[0]
    tm = 512
    nt = T // tm
    return pl.pallas_call(
        _moe_combine_kernel,
        out_shape=(jax.ShapeDtypeStruct((T, D_MODEL), F32), jax.ShapeDtypeStruct((T, D_MODEL), BF16)),
        grid=(nt,),
        in_specs=[pl.BlockSpec((tm, D_MODEL), lambda i: (i, 0)),
                  pl.BlockSpec((tm, D_MODEL), lambda i: (i + nt, 0)),
                  pl.BlockSpec((tm, LANES), lambda i: (i, 0)),
                  pl.BlockSpec((tm, D_MODEL), lambda i: (i, 0)),
                  pl.BlockSpec((1, D_MODEL), lambda i: (0, 0)),
                  pl.BlockSpec((1, D_MODEL), lambda i: (0, 0))],
        out_specs=(pl.BlockSpec((tm, D_MODEL), lambda i: (i, 0)),
                   pl.BlockSpec((tm, D_MODEL), lambda i: (i, 0))),
        compiler_params=_cparams(("parallel",)),
        name="moe_combine_ln",
    )(o_rows, o_rows, route, xf, ln_g, ln_b)


def _moe_plan(route, T):
    n_pairs = TOP_K * T
    ids = route[:, :TOP_K].astype(jnp.int32)
    e_flat = ids.reshape(-1)
    pair = jnp.arange(n_pairs, dtype=jnp.int32)
    order = jnp.sort(e_flat * n_pairs + pair) % n_pairs
    onehot = (e_flat[:, None] == jnp.arange(N_EXPERTS, dtype=jnp.int32)[None, :]).astype(jnp.int32)
    counts = jnp.sum(onehot, axis=0)
    first = jnp.cumsum(counts) - counts
    tiles_per = (counts + MOE_TM - 1) // MOE_TM
    tile_end = jnp.cumsum(tiles_per)
    tile_start = tile_end - tiles_per
    n_used = tile_end[-1]
    n_tiles = n_pairs // MOE_TM + N_EXPERTS
    tiles = jnp.arange(n_tiles, dtype=jnp.int32)
    tidx = jnp.minimum(tiles, n_used - 1)
    tile_expert = jnp.sum((tidx[:, None] >= tile_end[None, :]).astype(jnp.int32), axis=1)
    tile_expert = jnp.minimum(tile_expert, N_EXPERTS - 1).astype(jnp.int32)
    r_in = jnp.arange(MOE_TMX, dtype=jnp.int32)[None, :]
    k = (tiles - tile_start[tile_expert])[:, None] * MOE_TM + r_in
    valid = (r_in < MOE_TM) & (k < counts[tile_expert][:, None]) & (tiles < n_used)[:, None]
    src = jnp.clip(first[tile_expert][:, None] + k, 0, n_pairs - 1)
    p = order[src]
    spill = n_pairs + (tiles % 2)[:, None] * MOE_TMX + r_in
    row_token = jnp.where(valid, p // TOP_K, 0).reshape(-1)
    row_dest = jnp.where(valid, (p % TOP_K) * T + p // TOP_K, spill)
    lead = n_pairs + MOE_TMX + r_in
    row_dest = jnp.concatenate([lead, row_dest], axis=0).reshape(-1)
    return tile_expert, n_used.reshape(1).astype(jnp.int32), row_token, row_dest


def _prep_w_in(w_in):
    L = w_in.shape[0]
    z = lambda n: jnp.zeros((L, D_MODEL, n), w_in.dtype)
    o_swa = MLA_IN
    o_gdn = MLA_IN + SWA_IN
    hd = SWA_HEAD_DIM
    kr = w_in[:, :, MLA_Q_RANK + MLA_KV_RANK:MLA_IN]
    half = MLA_ROPE // 2
    kr_sw = jnp.concatenate([kr[:, :, half:], kr[:, :, :half]], axis=-1)
    ba = w_in[:, :, o_gdn + GDN_QKV:o_gdn + GDN_QKV + 2 * GDN_HEADS]
    segs = [
        w_in[:, :, :MLA_Q_RANK],
        w_in[:, :, o_swa:o_swa + SWA_HEADS * hd],
        w_in[:, :, o_gdn + GDN_QKV + 2 * GDN_HEADS:],
        w_in[:, :, o_gdn:o_gdn + GDN_QKV],
        w_in[:, :, MLA_Q_RANK:MLA_Q_RANK + MLA_KV_RANK],
        w_in[:, :, o_swa + SWA_HEADS * hd:o_swa + (SWA_HEADS + SWA_KV_HEADS) * hd],
        w_in[:, :, o_swa + (SWA_HEADS + SWA_KV_HEADS) * hd:o_swa + SWA_IN],
        kr, kr_sw, z(LANES - 2 * MLA_ROPE),
        ba, z(LANES - 2 * GDN_HEADS),
        z(D_PROJ - COL_BA - LANES),
    ]
    return jnp.concatenate([seg.astype(BF16) for seg in segs], axis=-1)


def _prep_mla_weights(w_uq, w_ukv):
    L = w_uq.shape[0]
    half = MLA_ROPE // 2
    wq = w_uq.reshape(L, MLA_Q_RANK, MLA_HEADS, MLA_NOPE + MLA_ROPE)
    nope, rope = wq[..., :MLA_NOPE], wq[..., MLA_NOPE:]
    rope_sw = jnp.concatenate([rope[..., half:], rope[..., :half]], axis=-1)
    zpad = jnp.zeros(wq.shape[:3] + (MLA_HP - MLA_NOPE - MLA_ROPE,), wq.dtype)
    wq_pad = jnp.concatenate([nope, rope, zpad], axis=-1).reshape(L, MLA_Q_RANK, MLA_HEADS * MLA_HP)
    wq_sw = jnp.concatenate([jnp.zeros_like(nope), rope_sw, zpad], axis=-1).reshape(L, MLA_Q_RANK, MLA_HEADS * MLA_HP)
    wkv = w_ukv.reshape(L, MLA_KV_RANK, MLA_HEADS, MLA_NOPE + MLA_V)
    knope, vv = wkv[..., :MLA_NOPE], wkv[..., MLA_NOPE:]
    wk_pad = jnp.concatenate([knope, jnp.zeros(knope.shape[:3] + (MLA_HP - MLA_NOPE,), wkv.dtype)], axis=-1)
    wk_pad = wk_pad.reshape(L, MLA_KV_RANK, MLA_HEADS * MLA_HP)
    wv = vv.reshape(L, MLA_KV_RANK, MLA_HEADS * MLA_V)
    return wq_pad.astype(BF16), wq_sw.astype(BF16), wk_pad.astype(BF16), wv.astype(BF16)


def _rope_tables(S):
    half = MLA_ROPE // 2
    pos = jnp.arange(S, dtype=jnp.int32)
    inv_freq = ROPE_BASE ** (-jnp.arange(half, dtype=F32) / half)
    ang = pos.astype(F32)[:, None] * inv_freq[None, :]
    cos, sin = jnp.cos(ang), jnp.sin(ang)
    cos2 = jnp.concatenate([cos, cos], axis=-1)
    sin2 = jnp.concatenate([-sin, sin], axis=-1)
    scale = (MLA_NOPE + MLA_ROPE) ** -0.5
    tail = jnp.zeros((S, MLA_HP - MLA_NOPE - MLA_ROPE), F32)
    cq_head = jnp.concatenate([jnp.ones((S, MLA_NOPE), F32), cos2, tail], axis=-1) * scale
    sq_head = jnp.concatenate([jnp.zeros((S, MLA_NOPE), F32), sin2, tail], axis=-1) * scale
    cqt = jnp.tile(cq_head, (1, MLA_HEADS))
    sqt = jnp.tile(sq_head, (1, MLA_HEADS))
    ckt = jnp.concatenate([cos2, sin2, jnp.zeros((S, LANES - 2 * MLA_ROPE), F32)], axis=-1)
    j = jnp.arange(LANES)[:, None]
    col = jnp.arange(MLA_HEADS * MLA_HP)[None, :]
    place = ((col % MLA_HP) == (MLA_NOPE + (j % MLA_ROPE))) & (j < 2 * MLA_ROPE)
    return cqt, sqt, ckt, place.astype(BF16)


def _pad_lanes(v, offset):
    out = jnp.zeros((1, LANES), F32)
    return lax.dynamic_update_slice(out, v.astype(F32).reshape(1, -1), (0, offset))


def kernel(x, w_in, mla_q_norm, mla_w_uq, mla_kv_norm, mla_w_ukv, swa_sinks, gdn_conv,
           gdn_a_log, gdn_dt_bias, gdn_norm, w_out, ln1_g, ln1_b, ln2_g, ln2_b,
           ffn_w_gate, ffn_w_up, ffn_w_down, moe_router, moe_w_gate, moe_w_up, moe_w_down):
    B, S, D = x.shape
    T = B * S
    xf = x.reshape(T, D)
    xb = xf.astype(BF16)

    w_in_b = _prep_w_in(w_in)
    wq_pad, wq_sw, wk_pad, wv = _prep_mla_weights(mla_w_uq, mla_w_ukv)
    cqt, sqt, ckt, place = _rope_tables(S)
    w_out_b = w_out.astype(BF16)
    ffn_g, ffn_u, ffn_d = ffn_w_gate.astype(BF16), ffn_w_up.astype(BF16), ffn_w_down.astype(BF16)
    moe_g, moe_u, moe_d = moe_w_gate.astype(BF16), moe_w_up.astype(BF16), moe_w_down.astype(BF16)
    router128 = jnp.pad(moe_router.astype(F32), ((0, 0), (0, 0), (0, LANES - N_EXPERTS)))

    for l in range(DEPTH):
        proj = _in_proj(xb, w_in_b, l)
        q, k, v = _mla_prep(proj, mla_q_norm[l].reshape(1, -1), mla_kv_norm[l].reshape(1, -1),
                            wq_pad, wq_sw, wk_pad, wv, l, place, cqt, sqt, ckt, S)
        h_mla = _mla_attn(q, k, v, B, S)
        h_swa = _swa(proj, swa_sinks[l].astype(F32), B, S)
        h_gdn = _gdn(proj, gdn_conv[l], _pad_lanes(gdn_a_log[l], GDN_HEADS), _pad_lanes(gdn_dt_bias[l], GDN_HEADS),
                     gdn_norm[l].reshape(1, -1), B, S)
        xf, xb = _out_proj_ln(h_mla, h_swa, h_gdn, xf, w_out_b, l,
                              ln1_g[l].reshape(1, -1), ln1_b[l].reshape(1, -1))
        lg, lb = ln2_g[l].reshape(1, -1), ln2_b[l].reshape(1, -1)
        if l % 2 == 0:
            xf, xb = _ffn_ln(xb, xf, ffn_g, ffn_u, ffn_d, l // 2, lg, lb)
        else:
            m = l // 2
            route = _router(xf, router128[m])
            tile_expert, n_used, row_token, row_dest = _moe_plan(route, T)
            o_rows = _moe_grouped(tile_expert, n_used, row_token, row_dest, xf, moe_g, moe_u, moe_d, m)
            xf, xb = _moe_combine_ln(o_rows, route, xf, lg, lb)
    return xf.reshape(B, S, D)
```

```python
import functools
import math

import jax
import jax.numpy as jnp
from jax import lax
from jax.experimental import pallas as pl
from jax.experimental.pallas import tpu as pltpu

F32 = jnp.float32
BF16 = jnp.bfloat16

D_MODEL = 2048
DEPTH = 4
MLA_HEADS = 8
MLA_Q_RANK = 512
MLA_KV_RANK = 256
MLA_NOPE = 64
MLA_ROPE = 32
MLA_V = 64
ROPE_BASE = 10000.0
SWA_HEADS = 8
SWA_KV_HEADS = 2
SWA_GROUP = SWA_HEADS // SWA_KV_HEADS
SWA_HEAD_DIM = 64
WINDOW = 128
GDN_HEADS = 8
GDN_DK = 128
GDN_DV = 128
CONV_WIDTH = 4
GDN_C = 256
GDN_HEAD_GROUP = 4
D_FF = 7168
N_EXPERTS = 8
TOP_K = 2
DN_ALPHA = float((2 * DEPTH) ** 0.25)
LN_EPS = 1e-5
RMS_EPS = 1e-6

MLA_IN = MLA_Q_RANK + MLA_KV_RANK + MLA_ROPE
SWA_IN = (SWA_HEADS + 2 * SWA_KV_HEADS) * SWA_HEAD_DIM
GDN_QKV = GDN_HEADS * (2 * GDN_DK + GDN_DV)

LANES = 128
NEG_BIG = -1e30
VMEM_LIMIT = 56 * 1024 * 1024

COL_CQ = 0
COL_SWAQ = 512
COL_Z = 1024
COL_GQ = 2048
COL_GK = 3072
COL_GV = 4096
COL_CKV = 5120
COL_SWAK = 5376
COL_SWAV = 5504
COL_KR = 5632
COL_BA = 5760
D_PROJ = 6144

MLA_HP = 128


def _cparams(sem, vmem=VMEM_LIMIT):
    return pltpu.CompilerParams(dimension_semantics=sem, vmem_limit_bytes=vmem)


def _layer_norm_rows(y, g, b):
    mu = jnp.mean(y, axis=-1, keepdims=True)
    d = y - mu
    var = jnp.mean(d * d, axis=-1, keepdims=True)
    return d * lax.rsqrt(var + LN_EPS) * g + b


def _sigmoid(x):
    return 1.0 / (1.0 + jnp.exp(-x))


def _silu(x):
    return x * _sigmoid(x)


def _dot(a, b):
    return jnp.dot(a, b, preferred_element_type=F32)


def _dot_nt(a, b):
    return lax.dot_general(a, b, (((1,), (1,)), ((), ())), preferred_element_type=F32)


def _matmul_kernel(x_ref, w_ref, o_ref):
    o_ref[...] = _dot(x_ref[...], w_ref[...])


def _in_proj(xb, w, l):
    T = xb.shape[0]
    tm, tn = 1024, 2048
    return pl.pallas_call(
        _matmul_kernel,
        out_shape=jax.ShapeDtypeStruct((T, D_PROJ), F32),
        grid=(T // tm, D_PROJ // tn),
        in_specs=[pl.BlockSpec((tm, D_MODEL), lambda i, j: (i, 0)),
                  pl.BlockSpec((None, D_MODEL, tn), lambda i, j: (l, 0, j))],
        out_specs=pl.BlockSpec((tm, tn), lambda i, j: (i, j)),
        compiler_params=_cparams(("parallel", "arbitrary")),
        name="in_proj",
    )(xb, w)


def _mla_prep_kernel(cq_ref, ckv_ref, kr_ref, gq_ref, gkv_ref, wq_ref, wqs_ref, wk_ref, wv_ref,
                     e_ref, cqt_ref, sqt_ref, ckt_ref, q_ref, k_ref, v_ref):
    cq = cq_ref[...]
    hq = cq * lax.rsqrt(jnp.mean(cq * cq, axis=-1, keepdims=True) + RMS_EPS) * gq_ref[...]
    hq = hq.astype(BF16)
    q = _dot(hq, wq_ref[...]) * cqt_ref[...] + _dot(hq, wqs_ref[...]) * sqt_ref[...]
    q_ref[...] = q.astype(BF16)
    ckv = ckv_ref[...]
    hk = ckv * lax.rsqrt(jnp.mean(ckv * ckv, axis=-1, keepdims=True) + RMS_EPS) * gkv_ref[...]
    hk = hk.astype(BF16)
    krp = kr_ref[...] * ckt_ref[...]
    krp_hi = krp.astype(BF16)
    krp_lo = (krp - krp_hi.astype(F32)).astype(BF16)
    e = e_ref[...]
    k = _dot(hk, wk_ref[...]) + (_dot(krp_hi, e) + _dot(krp_lo, e))
    for h in range(MLA_HEADS):
        k_ref[h * MLA_HP:(h + 1) * MLA_HP, :] = k[:, h * MLA_HP:(h + 1) * MLA_HP].T.astype(BF16)
    v_ref[...] = _dot(hk, wv_ref[...]).astype(BF16)


def _mla_prep(proj, gq, gkv, wq, wqs, wk, wv, l, e, cqt, sqt, ckt, S):
    T = proj.shape[0]
    tm = 512
    sb = S // tm
    full = lambda shape: pl.BlockSpec(shape, lambda i: (0, 0))
    layer = lambda shape: pl.BlockSpec((None,) + shape, lambda i: (l, 0, 0))
    return pl.pallas_call(
        _mla_prep_kernel,
        out_shape=(jax.ShapeDtypeStruct((T, MLA_HEADS * MLA_HP), BF16),
                   jax.ShapeDtypeStruct((T // S * MLA_HEADS * MLA_HP, S), BF16),
                   jax.ShapeDtypeStruct((T, MLA_HEADS * MLA_V), BF16)),
        grid=(T // tm,),
        in_specs=[pl.BlockSpec((tm, MLA_Q_RANK), lambda i: (i, COL_CQ // MLA_Q_RANK)),
                  pl.BlockSpec((tm, MLA_KV_RANK), lambda i: (i, COL_CKV // MLA_KV_RANK)),
                  pl.BlockSpec((tm, LANES), lambda i: (i, COL_KR // LANES)),
                  full((1, MLA_Q_RANK)), full((1, MLA_KV_RANK)),
                  layer((MLA_Q_RANK, MLA_HEADS * MLA_HP)), layer((MLA_Q_RANK, MLA_HEADS * MLA_HP)),
                  layer((MLA_KV_RANK, MLA_HEADS * MLA_HP)), layer((MLA_KV_RANK, MLA_HEADS * MLA_V)),
                  full((LANES, MLA_HEADS * MLA_HP)),
                  pl.BlockSpec((tm, MLA_HEADS * MLA_HP), lambda i: (i % sb, 0)),
                  pl.BlockSpec((tm, MLA_HEADS * MLA_HP), lambda i: (i % sb, 0)),
                  pl.BlockSpec((tm, LANES), lambda i: (i % sb, 0))],
        out_specs=(pl.BlockSpec((tm, MLA_HEADS * MLA_HP), lambda i: (i, 0)),
                   pl.BlockSpec((MLA_HEADS * MLA_HP, tm), lambda i: (i // sb, i % sb)),
                   pl.BlockSpec((tm, MLA_HEADS * MLA_V), lambda i: (i, 0))),
        compiler_params=_cparams(("parallel",)),
        name="mla_prep",
    )(proj, proj, proj, gq, gkv, wq, wqs, wk, wv, e, cqt, sqt, ckt)


MLA_G = 4


def _mla_attn_kernel(q_ref, k_ref, v_ref, o_ref, acc_sc, m_sc, l_sc, *, tq, tk):
    qi = pl.program_id(2)
    acc_sc[...] = jnp.zeros_like(acc_sc)
    m_sc[...] = jnp.full_like(m_sc, NEG_BIG)
    l_sc[...] = jnp.zeros_like(l_sc)
    ratio = tq // tk

    def step(kj, diag):
        rows = pl.ds(pl.multiple_of(kj * tk, tk), tk)
        for g in range(MLA_G):
            s = _dot(q_ref[:, g * MLA_HP:(g + 1) * MLA_HP], k_ref[g * MLA_HP:(g + 1) * MLA_HP, rows])
            if diag is not None:
                r = lax.broadcasted_iota(jnp.int32, s.shape, 0)
                c = lax.broadcasted_iota(jnp.int32, s.shape, 1)
                s = jnp.where(c + diag * tk <= r, s, NEG_BIG)
            m_prev = m_sc[g]
            m_new = jnp.maximum(m_prev, jnp.max(s, axis=1, keepdims=True))
            alpha = jnp.exp(m_prev - m_new)
            p = jnp.exp(s - m_new[:, :1])
            l_sc[g] = alpha * l_sc[g] + jnp.sum(p, axis=1, keepdims=True)
            pv = _dot(p.astype(BF16), v_ref[rows, (g // 2) * LANES:(g // 2 + 1) * LANES])
            acc_sc[g] = acc_sc[g] * alpha + pv
            m_sc[g] = m_new

    def body(kj, carry):
        step(kj, None)
        return carry

    lax.fori_loop(0, qi * ratio, body, 0)
    for d in range(ratio):
        step(qi * ratio + d, d)
    lane = lax.broadcasted_iota(jnp.int32, (tq, LANES), 1)
    for pr in range(MLA_G // 2):
        o0 = acc_sc[2 * pr] / l_sc[2 * pr]
        o1 = acc_sc[2 * pr + 1] / l_sc[2 * pr + 1]
        o_ref[:, pr * LANES:(pr + 1) * LANES] = jnp.where(lane < MLA_V, o0, o1).astype(o_ref.dtype)


def _mla_attn(q, k, v, B, S):
    T = q.shape[0]
    tq, tk = 512, 512
    nq = S // tq
    groups = MLA_HEADS // MLA_G
    return pl.pallas_call(
        functools.partial(_mla_attn_kernel, tq=tq, tk=tk),
        out_shape=jax.ShapeDtypeStruct((T, MLA_HEADS * MLA_V), BF16),
        grid=(B, groups, nq),
        in_specs=[pl.BlockSpec((tq, MLA_G * MLA_HP), lambda b, h, i: (b * nq + i, h)),
                  pl.BlockSpec((MLA_G * MLA_HP, S), lambda b, h, i: (b * groups + h, 0)),
                  pl.BlockSpec((S, MLA_G * MLA_V), lambda b, h, i: (b, h))],
        out_specs=pl.BlockSpec((tq, MLA_G * MLA_V), lambda b, h, i: (b * nq + i, h)),
        scratch_shapes=[pltpu.VMEM((MLA_G, tq, LANES), F32),
                        pltpu.VMEM((MLA_G, tq, LANES), F32),
                        pltpu.VMEM((MLA_G, tq, LANES), F32)],
        compiler_params=_cparams(("parallel", "parallel", "arbitrary")),
        name="mla_attn",
    )(q, k, v)


def _swa_kernel(sink_ref, q_ref, kc_ref, kp_ref, vc_ref, vp_ref, o_ref):
    n = pl.program_id(1)
    hd = SWA_HEAD_DIM
    qi = lax.broadcasted_iota(jnp.int32, (WINDOW, 2 * WINDOW), 0)
    kj = lax.broadcasted_iota(jnp.int32, (WINDOW, 2 * WINDOW), 1)
    delta = qi + WINDOW - kj
    first_key = jnp.where(n > 0, 0, WINDOW)
    valid = (delta >= 0) & (delta < WINDOW) & (kj >= first_key)
    deltaf = delta.astype(F32)
    kcat, vcat = [], []
    for g in range(SWA_KV_HEADS):
        kcat.append(jnp.concatenate([kp_ref[:, g * hd:(g + 1) * hd], kc_ref[:, g * hd:(g + 1) * hd]],
                                    axis=0).astype(BF16))
        vcat.append(jnp.concatenate([vp_ref[:, g * hd:(g + 1) * hd], vc_ref[:, g * hd:(g + 1) * hd]],
                                    axis=0).astype(BF16))
    heads = range(SWA_HEADS)
    s = [_dot_nt(q_ref[:, h * hd:(h + 1) * hd].astype(BF16), kcat[h // SWA_GROUP]) for h in heads]
    s = [jnp.where(valid, x * (hd ** -0.5) - (2.0 ** (-8.0 * (h + 1) / SWA_HEADS)) * deltaf, NEG_BIG)
         for x, h in zip(s, heads)]
    m = [jnp.maximum(jnp.max(x, axis=1, keepdims=True), sink_ref[h]) for x, h in zip(s, heads)]
    p = [jnp.exp(x - mm) for x, mm in zip(s, m)]
    denom = [jnp.sum(pp, axis=1, keepdims=True) + jnp.exp(sink_ref[h] - mm) for pp, mm, h in zip(p, m, heads)]
    o = [_dot(pp.astype(BF16), vcat[h // SWA_GROUP]) / dd for pp, dd, h in zip(p, denom, heads)]
    for oo, h in zip(o, heads):
        o_ref[:, h * hd:(h + 1) * hd] = oo.astype(o_ref.dtype)


def _swa(proj, sinks, B, S):
    T = proj.shape[0]
    nb = S // WINDOW
    cq = COL_SWAQ // (SWA_HEADS * SWA_HEAD_DIM)
    ck = COL_SWAK // LANES
    cv = COL_SWAV // LANES
    cur = lambda c: (lambda b, n: (b * nb + n, c))
    prev = lambda c: (lambda b, n: (b * nb + jnp.maximum(n - 1, 0), c))
    return pl.pallas_call(
        _swa_kernel,
        out_shape=jax.ShapeDtypeStruct((T, SWA_HEADS * SWA_HEAD_DIM), BF16),
        grid=(B, nb),
        in_specs=[pl.BlockSpec(memory_space=pltpu.SMEM),
                  pl.BlockSpec((WINDOW, SWA_HEADS * SWA_HEAD_DIM), cur(cq)),
                  pl.BlockSpec((WINDOW, LANES), cur(ck)),
                  pl.BlockSpec((WINDOW, LANES), prev(ck)),
                  pl.BlockSpec((WINDOW, LANES), cur(cv)),
                  pl.BlockSpec((WINDOW, LANES), prev(cv))],
        out_specs=pl.BlockSpec((WINDOW, SWA_HEADS * SWA_HEAD_DIM), lambda b, n: (b * nb + n, 0)),
        compiler_params=_cparams(("parallel", "arbitrary")),
        name="swa_attn",
    )(sinks, proj, proj, proj, proj, proj)


def _conv_silu(cur, prev8, w, sub):
    acc = cur * w[CONV_WIDTH - 1:CONV_WIDTH, :]
    for d in range(1, CONV_WIDTH):
        rolled = pltpu.roll(cur, d, axis=0)
        head = jnp.where(sub < d, pltpu.roll(prev8, d, axis=0), rolled[:8])
        shifted = jnp.concatenate([head, rolled[8:]], axis=0)
        acc = acc + shifted * w[CONV_WIDTH - 1 - d:CONV_WIDTH - d, :]
    return _silu(acc)


def _l2_normalize(y):
    return y * lax.rsqrt(jnp.sum(y * y, axis=-1, keepdims=True) + RMS_EPS)


def _gdn_chunk_kernel(xq_ref, xk_ref, xv_ref, ba_ref, z_ref, cw_ref, alog_ref, dtb_ref, ng_ref, o_ref,
                      state_sc, tail_sc):
    @pl.when(pl.program_id(1) == 0)
    def _():
        state_sc[...] = jnp.zeros_like(state_sc)
        tail_sc[...] = jnp.zeros_like(tail_sc)

    C = GDN_C
    H = GDN_HEADS
    W = H * GDN_DK
    ri = lax.broadcasted_iota(jnp.int32, (C, C), 0)
    ci = lax.broadcasted_iota(jnp.int32, (C, C), 1)
    incl = ri >= ci
    strict = ri > ci
    merge = jnp.where(strict, ri ^ ci, 2 * C)
    ng = ng_ref[...]
    sub = lax.broadcasted_iota(jnp.int32, (8, LANES), 0)

    ba = ba_ref[...]
    lane = lax.broadcasted_iota(jnp.int32, (C, LANES), 1)
    pos = lax.broadcasted_iota(jnp.int32, (C, LANES), 0)
    t = ba + dtb_ref[...]
    g = -jnp.exp(alog_ref[...]) * (jnp.maximum(t, 0.0) + jnp.log1p(jnp.exp(-jnp.abs(t))))
    step = 1
    while step < C:
        g = g + jnp.where(pos >= step, pltpu.roll(g, step, axis=0), 0.0)
        step *= 2
    gt = jnp.where(lane < H, _sigmoid(ba), g)
    gtt = gt.T

    def run(group):
        lmat, a_qk, rhs, q_dec, k_dec_t, c_decay = [], [], [], [], [], []
        for h in group:
            cols = slice(h * GDN_DK, (h + 1) * GDN_DK)
            q = _l2_normalize(_conv_silu(xq_ref[:, cols], tail_sc[0, :, cols], cw_ref[:, h * GDN_DK:(h + 1) * GDN_DK],
                                         sub)) * (GDN_DK ** -0.5)
            k = _l2_normalize(_conv_silu(xk_ref[:, cols], tail_sc[1, :, cols],
                                         cw_ref[:, W + h * GDN_DK:W + (h + 1) * GDN_DK], sub))
            v = _conv_silu(xv_ref[:, cols], tail_sc[2, :, cols],
                           cw_ref[:, 2 * W + h * GDN_DV:2 * W + (h + 1) * GDN_DV], sub)
            beta = gt[:, h:h + 1]
            gc = gt[:, H + h:H + h + 1]
            gr = gtt[H + h:H + h + 1, :]
            glast = gc[C - 1:C, :]
            decay = jnp.exp(jnp.where(incl, gc - gr, NEG_BIG))
            eg = jnp.exp(gc)
            kb = k * beta
            a1 = _dot_nt(jnp.concatenate([kb, q], axis=0).astype(BF16), k.astype(BF16))
            lmat.append(a1[:C] * jnp.where(strict, decay, 0.0))
            a_qk.append((a1[C:] * decay).astype(BF16))
            rhs.append(jnp.concatenate([v * beta, kb * eg], axis=1))
            q_dec.append((q * eg).astype(BF16))
            k_dec_t.append((k * jnp.exp(glast - gc)).T.astype(BF16))
            c_decay.append(jnp.exp(glast))

        base = merge < 16
        mpow = [jnp.where(base, l, 0.0) for l in lmat]
        rinv = [-m for m in mpow]
        for _ in range(3):
            mb = [m.astype(BF16) for m in mpow]
            mpow = [_dot(m, m) for m in mb]
            rinv = [r + m + _dot(r.astype(BF16), m.astype(BF16)) for r, m in zip(rinv, mpow)]
        s = 16
        while s < C:
            level = (merge >= s) & (merge < 2 * s)
            off = [jnp.where(level, l, 0.0) for l in lmat]
            y = [c + _dot(r.astype(BF16), c.astype(BF16)) for r, c in zip(rinv, off)]
            rinv = [r - yy - _dot(yy.astype(BF16), r.astype(BF16)) for r, yy in zip(rinv, y)]
            s *= 2

        sol = [x + _dot(r.astype(BF16), x.astype(BF16)) for r, x in zip(rinv, rhs)]
        idx = range(len(group))
        a2 = [_dot(jnp.concatenate([sol[i][:, GDN_DV:].astype(BF16), q_dec[i]], axis=0),
                   state_sc[group[i]].astype(BF16)) for i in idx]
        v_new = [(sol[i][:, :GDN_DV] - a2[i][:C]).astype(BF16) for i in idx]
        out = [a2[i][C:] + _dot(a_qk[i], v_new[i]) for i in idx]
        for i in idx:
            state_sc[group[i]] = state_sc[group[i]] * c_decay[i] + _dot(k_dec_t[i], v_new[i])
        for i in idx:
            h = group[i]
            cols = slice(h * GDN_DV, (h + 1) * GDN_DV)
            o = out[i]
            on = o * lax.rsqrt(jnp.mean(o * o, axis=-1, keepdims=True) + RMS_EPS) * ng
            o_ref[:, cols] = (on * _silu(z_ref[:, cols])).astype(o_ref.dtype)

    for first in range(0, H, GDN_HEAD_GROUP):
        run(list(range(first, first + GDN_HEAD_GROUP)))
    tail_sc[0] = xq_ref[C - 8:C, :]
    tail_sc[1] = xk_ref[C - 8:C, :]
    tail_sc[2] = xv_ref[C - 8:C, :]


def _gdn(proj, conv_w, alog128, dtb128, norm_g, B, S):
    T = proj.shape[0]
    R = GDN_C
    nr = S // R
    W = GDN_HEADS * GDN_DK
    row = lambda c: (lambda b, r: (b * nr + r, c))
    full = lambda shape: pl.BlockSpec(shape, lambda b, r: (0, 0))
    return pl.pallas_call(
        _gdn_chunk_kernel,
        out_shape=jax.ShapeDtypeStruct((T, GDN_HEADS * GDN_DV), BF16),
        grid=(B, nr),
        in_specs=[pl.BlockSpec((R, W), row(COL_GQ // W)),
                  pl.BlockSpec((R, W), row(COL_GK // W)),
                  pl.BlockSpec((R, W), row(COL_GV // W)),
                  pl.BlockSpec((R, LANES), row(COL_BA // LANES)),
                  pl.BlockSpec((R, W), row(COL_Z // W)),
                  full((CONV_WIDTH, GDN_QKV)), full((1, LANES)), full((1, LANES)), full((1, GDN_DV))],
        out_specs=pl.BlockSpec((R, W), row(0)),
        scratch_shapes=[pltpu.VMEM((GDN_HEADS, GDN_DK, GDN_DV), F32),
                        pltpu.VMEM((3, 8, W), F32)],
        compiler_params=_cparams(("parallel", "arbitrary")),
        name="gdn",
    )(proj, proj, proj, proj, proj, conv_w, alog128, dtb128, norm_g)


def _out_proj_kernel(a_ref, s_ref, g_ref, x_ref, wa_ref, ws_ref, wg_ref, lg_ref, lb_ref, of_ref, ob_ref):
    rows = 128
    for r in range(0, x_ref.shape[0], rows):
        sl = slice(r, r + rows)
        y = _dot(a_ref[sl, :], wa_ref[...]) + _dot(s_ref[sl, :], ws_ref[...]) + _dot(g_ref[sl, :], wg_ref[...])
        y = _layer_norm_rows(DN_ALPHA * x_ref[sl, :] + y, lg_ref[...], lb_ref[...])
        of_ref[sl, :] = y
        ob_ref[sl, :] = y.astype(BF16)


def _out_proj_ln(mla, swa, gdn, x, w_out, l, ln_g, ln_b):
    T = x.shape[0]
    tm = 512
    n_mla, n_swa, n_gdn = mla.shape[1], swa.shape[1], gdn.shape[1]
    assert n_mla % n_swa == 0 and (n_mla + n_swa) % n_gdn == 0
    row = lambda w: pl.BlockSpec((tm, w), lambda i: (i, 0))
    full = lambda shape: pl.BlockSpec(shape, lambda i: (0, 0))
    return pl.pallas_call(
        _out_proj_kernel,
        out_shape=(jax.ShapeDtypeStruct((T, D_MODEL), F32), jax.ShapeDtypeStruct((T, D_MODEL), BF16)),
        grid=(T // tm,),
        in_specs=[row(n_mla), row(n_swa), row(n_gdn), row(D_MODEL),
                  pl.BlockSpec((None, n_mla, D_MODEL), lambda i: (l, 0, 0)),
                  pl.BlockSpec((None, n_swa, D_MODEL), lambda i: (l, n_mla // n_swa, 0)),
                  pl.BlockSpec((None, n_gdn, D_MODEL), lambda i: (l, (n_mla + n_swa) // n_gdn, 0)),
                  full((1, D_MODEL)), full((1, D_MODEL))],
        out_specs=(row(D_MODEL), row(D_MODEL)),
        compiler_params=_cparams(("parallel",)),
        name="out_proj_ln",
    )(mla, swa, gdn, x, w_out, w_out, w_out, ln_g, ln_b)


def _ffn_kernel(xb_ref, xf_ref, wg_ref, wu_ref, wd_ref, lg_ref, lb_ref, of_ref, ob_ref, acc_sc):
    j = pl.program_id(1)

    @pl.when(j == 0)
    def _():
        acc_sc[...] = jnp.zeros_like(acc_sc)

    xb = xb_ref[...]
    h = _silu(_dot(xb, wg_ref[...])) * _dot(xb, wu_ref[...])
    acc_sc[...] += _dot(h.astype(BF16), wd_ref[...])

    @pl.when(j == pl.num_programs(1) - 1)
    def _():
        y = _layer_norm_rows(DN_ALPHA * xf_ref[...] + acc_sc[...], lg_ref[...], lb_ref[...])
        of_ref[...] = y
        ob_ref[...] = y.astype(BF16)


def _ffn_ln(xb, xf, wg, wu, wd, m, ln_g, ln_b):
    T = xb.shape[0]
    tm, tf = 512, 512
    return pl.pallas_call(
        _ffn_kernel,
        out_shape=(jax.ShapeDtypeStruct((T, D_MODEL), F32), jax.ShapeDtypeStruct((T, D_MODEL), BF16)),
        grid=(T // tm, D_FF // tf),
        in_specs=[pl.BlockSpec((tm, D_MODEL), lambda i, j: (i, 0)),
                  pl.BlockSpec((tm, D_MODEL), lambda i, j: (i, 0)),
                  pl.BlockSpec((None, D_MODEL, tf), lambda i, j: (m, 0, j)),
                  pl.BlockSpec((None, D_MODEL, tf), lambda i, j: (m, 0, j)),
                  pl.BlockSpec((None, tf, D_MODEL), lambda i, j: (m, j, 0)),
                  pl.BlockSpec((1, D_MODEL), lambda i, j: (0, 0)),
                  pl.BlockSpec((1, D_MODEL), lambda i, j: (0, 0))],
        out_specs=(pl.BlockSpec((tm, D_MODEL), lambda i, j: (i, 0)),
                   pl.BlockSpec((tm, D_MODEL), lambda i, j: (i, 0))),
        scratch_shapes=[pltpu.VMEM((tm, D_MODEL), F32)],
        compiler_params=_cparams(("parallel", "arbitrary")),
        name="ffn_ln",
    )(xb, xf, wg, wu, wd, ln_g, ln_b)


def _router_kernel(x_ref, w_ref, o_ref):
    x = x_ref[...]
    w = w_ref[...]
    xh = x.astype(BF16)
    wh = w.astype(BF16)
    xl = (x - xh.astype(F32)).astype(BF16)
    wl = (w - wh.astype(F32)).astype(BF16)
    logits = _dot(xh, wh) + (_dot(xl, wh) + _dot(xh, wl))
    lane = lax.broadcasted_iota(jnp.int32, logits.shape, 1).astype(F32)
    logits = jnp.where(lane < N_EXPERTS, logits, -jnp.inf)
    m1 = jnp.max(logits, axis=1, keepdims=True)
    i1 = jnp.min(jnp.where(logits == m1, lane, float(LANES)), axis=1, keepdims=True)
    rest = jnp.where(lane == i1, -jnp.inf, logits)
    m2 = jnp.max(rest, axis=1, keepdims=True)
    i2 = jnp.min(jnp.where(rest == m2, lane, float(LANES)), axis=1, keepdims=True)
    e = jnp.exp(m2 - m1)
    g1 = 1.0 / (1.0 + e)
    g2 = e / (1.0 + e)
    out = jnp.where(lane == 0, i1,
                    jnp.where(lane == 1, i2,
                              jnp.where(lane == 2, g1, jnp.where(lane == 3, g2, 0.0))))
    o_ref[...] = out


def _router(xf, w128):
    T = xf.shape[0]
    tm = 512
    return pl.pallas_call(
        _router_kernel,
        out_shape=jax.ShapeDtypeStruct((T, LANES), F32),
        grid=(T // tm,),
        in_specs=[pl.BlockSpec((tm, D_MODEL), lambda i: (i, 0)),
                  pl.BlockSpec((D_MODEL, LANES), lambda i: (0, 0))],
        out_specs=pl.BlockSpec((tm, LANES), lambda i: (i, 0)),
        compiler_params=_cparams(("parallel",)),
        name="moe_router",
    )(xf, w128)


MOE_TM = 512
MOE_TF = 1024
MOE_NJ = D_FF // MOE_TF
MOE_RPS = -(-MOE_TM // MOE_NJ)
MOE_MOVED = MOE_RPS * MOE_NJ
MOE_TMX = -(-MOE_MOVED // 8) * 8


def _moe_kernel(te_ref, nu_ref, tok_ref, dst_ref, x_hbm, wg_ref, wu_ref, wd_ref, o_hbm,
                xg_sc, xb_sc, acc_sc, gsem, ssem, *, n_tokens):
    i = pl.program_id(0)
    j = pl.program_id(1)
    n_used = nu_ref[0]
    active = i < n_used

    def gather_row(tile, s, q):
        pltpu.make_async_copy(x_hbm.at[tok_ref[tile * MOE_TMX + q]], xg_sc.at[s, q], gsem.at[s]).start()

    def scatter_row(seg, s, q, priority=0):
        pltpu.make_async_copy(acc_sc.at[s, q], o_hbm.at[dst_ref[seg * MOE_TMX + q]],
                              ssem.at[s]).start(priority=priority)

    def gather_wait(s):
        pltpu.make_async_copy(x_hbm.at[pl.ds(0, MOE_TM)], xg_sc.at[s, pl.ds(0, MOE_TM)], gsem.at[s]).wait()
        for q in range(MOE_TM, MOE_MOVED):
            pltpu.make_async_copy(x_hbm.at[0], xg_sc.at[s, q], gsem.at[s]).wait()

    def scatter_wait(s):
        pltpu.make_async_copy(acc_sc.at[s, pl.ds(0, MOE_TM)], o_hbm.at[pl.ds(0, MOE_TM)], ssem.at[s]).wait()
        for q in range(MOE_TM, MOE_MOVED):
            pltpu.make_async_copy(acc_sc.at[s, q], o_hbm.at[0], ssem.at[s]).wait()

    def run(slot):
        other = 1 - slot

        @pl.when(j == 0)
        def _():
            @pl.when(i == 0)
            def _():
                def issue(q, carry):
                    gather_row(0, 0, q)
                    return carry
                lax.fori_loop(0, MOE_MOVED, issue, 0, unroll=2)
                acc_sc[...] = jnp.zeros_like(acc_sc)
                for s in range(2):
                    spill = o_hbm.at[pl.ds(TOP_K * n_tokens + s * MOE_TMX, MOE_TMX)]
                    cp = pltpu.make_async_copy(acc_sc.at[0], spill, ssem.at[0])
                    cp.start()
                    cp.wait()

            @pl.when(i > 0)
            def _():
                scatter_wait(slot)
                acc_sc[slot, pl.ds(0, MOE_TM), :] = jnp.zeros((MOE_TM, D_MODEL), F32)

            gather_wait(slot)
            xb_sc[...] = xg_sc[slot, pl.ds(0, MOE_TM), :].astype(BF16)

        nxt = jnp.minimum(i + 1, n_used - 1)
        for r in range(MOE_RPS):
            scatter_row(i, other, j * MOE_RPS + r, priority=r % 2)
            gather_row(nxt, other, j * MOE_RPS + r)
        xb = xb_sc[...]
        h = _silu(_dot(xb, wg_ref[...])) * _dot(xb, wu_ref[...])
        acc_sc[slot, pl.ds(0, MOE_TM), :] += _dot(h.astype(BF16), wd_ref[...])

        @pl.when((j == MOE_NJ - 1) & (i == n_used - 1))
        def _():
            gather_wait(other)
            scatter_wait(other)

            def issue(q, carry):
                scatter_row(i + 1, slot, q)
                return carry
            lax.fori_loop(0, MOE_MOVED, issue, 0, unroll=2)
            scatter_wait(slot)

    for sl in range(2):
        pl.when(active & (lax.rem(i, 2) == sl))(functools.partial(run, sl))


def _moe_grouped(tile_expert, n_used, row_token, row_dest, xf, wg, wu, wd, m):
    n_tiles = tile_expert.shape[0]
    n_tokens = xf.shape[0]

    def wcol(i, j, te, nu, tok, dst):
        jj = jnp.where(i < nu[0], j, MOE_NJ - 1)
        return (m, te[i], 0, jj)

    def wrow(i, j, te, nu, tok, dst):
        jj = jnp.where(i < nu[0], j, MOE_NJ - 1)
        return (m, te[i], jj, 0)

    grid_spec = pltpu.PrefetchScalarGridSpec(
        num_scalar_prefetch=4,
        grid=(n_tiles, MOE_NJ),
        in_specs=[pl.BlockSpec(memory_space=pl.ANY),
                  pl.BlockSpec((None, None, D_MODEL, MOE_TF), wcol),
                  pl.BlockSpec((None, None, D_MODEL, MOE_TF), wcol),
                  pl.BlockSpec((None, None, MOE_TF, D_MODEL), wrow)],
        out_specs=pl.BlockSpec(memory_space=pl.ANY),
        scratch_shapes=[pltpu.VMEM((2, MOE_TMX, D_MODEL), F32),
                        pltpu.VMEM((MOE_TM, D_MODEL), BF16),
                        pltpu.VMEM((2, MOE_TMX, D_MODEL), F32),
                        pltpu.SemaphoreType.DMA((2,)),
                        pltpu.SemaphoreType.DMA((2,))],
    )
    return pl.pallas_call(
        functools.partial(_moe_kernel, n_tokens=n_tokens),
        out_shape=jax.ShapeDtypeStruct((TOP_K * n_tokens + 2 * MOE_TMX, D_MODEL), F32),
        grid_spec=grid_spec,
        compiler_params=_cparams(("arbitrary", "arbitrary")),
        name="moe_grouped",
    )(tile_expert, n_used, row_token, row_dest, xf, wg, wu, wd)


def _moe_combine_kernel(o0_ref, o1_ref, rt_ref, x_ref, lg_ref, lb_ref, of_ref, ob_ref):
    rt = rt_ref[...]
    f = rt[:, 2:3] * o0_ref[...] + rt[:, 3:4] * o1_ref[...]
    y = _layer_norm_rows(DN_ALPHA * x_ref[...] + f, lg_ref[...], lb_ref[...])
    of_ref[...] = y
    ob_ref[...] = y.astype(BF16)


def _moe_combine_ln(o_rows, route, xf, ln_g, ln_b):
    T = xf.shape[0]
    tm = 512
    nt = T // tm
    return pl.pallas_call(
        _moe_combine_kernel,
        out_shape=(jax.ShapeDtypeStruct((T, D_MODEL), F32), jax.ShapeDtypeStruct((T, D_MODEL), BF16)),
        grid=(nt,),
        in_specs=[pl.BlockSpec((tm, D_MODEL), lambda i: (i, 0)),
                  pl.BlockSpec((tm, D_MODEL), lambda i: (i + nt, 0)),
                  pl.BlockSpec((tm, LANES), lambda i: (i, 0)),
                  pl.BlockSpec((tm, D_MODEL), lambda i: (i, 0)),
                  pl.BlockSpec((1, D_MODEL), lambda i: (0, 0)),
                  pl.BlockSpec((1, D_MODEL), lambda i: (0, 0))],
        out_specs=(pl.BlockSpec((tm, D_MODEL), lambda i: (i, 0)),
                   pl.BlockSpec((tm, D_MODEL), lambda i: (i, 0))),
        compiler_params=_cparams(("parallel",)),
        name="moe_combine_ln",
    )(o_rows, o_rows, route, xf, ln_g, ln_b)


def _moe_plan(route, T):
    n_pairs = TOP_K * T
    ids = route[:, :TOP_K].astype(jnp.int32)
    e_flat = ids.reshape(-1)
    pair = jnp.arange(n_pairs, dtype=jnp.int32)
    order = jnp.sort(e_flat * n_pairs + pair) % n_pairs
    onehot = (e_flat[:, None] == jnp.arange(N_EXPERTS, dtype=jnp.int32)[None, :]).astype(jnp.int32)
    counts = jnp.sum(onehot, axis=0)
    first = jnp.cumsum(counts) - counts
    tiles_per = (counts + MOE_TM - 1) // MOE_TM
    tile_end = jnp.cumsum(tiles_per)
    tile_start = tile_end - tiles_per
    n_used = tile_end[-1]
    n_tiles = n_pairs // MOE_TM + N_EXPERTS
    tiles = jnp.arange(n_tiles, dtype=jnp.int32)
    tidx = jnp.minimum(tiles, n_used - 1)
    tile_expert = jnp.sum((tidx[:, None] >= tile_end[None, :]).astype(jnp.int32), axis=1)
    tile_expert = jnp.minimum(tile_expert, N_EXPERTS - 1).astype(jnp.int32)
    r_in = jnp.arange(MOE_TMX, dtype=jnp.int32)[None, :]
    k = (tiles - tile_start[tile_expert])[:, None] * MOE_TM + r_in
    valid = (r_in < MOE_TM) & (k < counts[tile_expert][:, None]) & (tiles < n_used)[:, None]
    src = jnp.clip(first[tile_expert][:, None] + k, 0, n_pairs - 1)
    p = order[src]
    spill = n_pairs + (tiles % 2)[:, None] * MOE_TMX + r_in
    row_token = jnp.where(valid, p // TOP_K, 0).reshape(-1)
    row_dest = jnp.where(valid, (p % TOP_K) * T + p // TOP_K, spill)
    lead = n_pairs + MOE_TMX + r_in
    row_dest = jnp.concatenate([lead, row_dest], axis=0).reshape(-1)
    return tile_expert, n_used.reshape(1).astype(jnp.int32), row_token, row_dest


def _prep_w_in(w_in):
    L = w_in.shape[0]
    z = lambda n: jnp.zeros((L, D_MODEL, n), w_in.dtype)
    o_swa = MLA_IN
    o_gdn = MLA_IN + SWA_IN
    hd = SWA_HEAD_DIM
    kr = w_in[:, :, MLA_Q_RANK + MLA_KV_RANK:MLA_IN]
    half = MLA_ROPE // 2
    kr_sw = jnp.concatenate([kr[:, :, half:], kr[:, :, :half]], axis=-1)
    ba = w_in[:, :, o_gdn + GDN_QKV:o_gdn + GDN_QKV + 2 * GDN_HEADS]
    segs = [
        w_in[:, :, :MLA_Q_RANK],
        w_in[:, :, o_swa:o_swa + SWA_HEADS * hd],
        w_in[:, :, o_gdn + GDN_QKV + 2 * GDN_HEADS:],
        w_in[:, :, o_gdn:o_gdn + GDN_QKV],
        w_in[:, :, MLA_Q_RANK:MLA_Q_RANK + MLA_KV_RANK],
        w_in[:, :, o_swa + SWA_HEADS * hd:o_swa + (SWA_HEADS + SWA_KV_HEADS) * hd],
        w_in[:, :, o_swa + (SWA_HEADS + SWA_KV_HEADS) * hd:o_swa + SWA_IN],
        kr, kr_sw, z(LANES - 2 * MLA_ROPE),
        ba, z(LANES - 2 * GDN_HEADS),
        z(D_PROJ - COL_BA - LANES),
    ]
    return jnp.concatenate([seg.astype(BF16) for seg in segs], axis=-1)


def _prep_mla_weights(w_uq, w_ukv):
    L = w_uq.shape[0]
    half = MLA_ROPE // 2
    wq = w_uq.reshape(L, MLA_Q_RANK, MLA_HEADS, MLA_NOPE + MLA_ROPE)
    nope, rope = wq[..., :MLA_NOPE], wq[..., MLA_NOPE:]
    rope_sw = jnp.concatenate([rope[..., half:], rope[..., :half]], axis=-1)
    zpad = jnp.zeros(wq.shape[:3] + (MLA_HP - MLA_NOPE - MLA_ROPE,), wq.dtype)
    wq_pad = jnp.concatenate([nope, rope, zpad], axis=-1).reshape(L, MLA_Q_RANK, MLA_HEADS * MLA_HP)
    wq_sw = jnp.concatenate([jnp.zeros_like(nope), rope_sw, zpad], axis=-1).reshape(L, MLA_Q_RANK, MLA_HEADS * MLA_HP)
    wkv = w_ukv.reshape(L, MLA_KV_RANK, MLA_HEADS, MLA_NOPE + MLA_V)
    knope, vv = wkv[..., :MLA_NOPE], wkv[..., MLA_NOPE:]
    wk_pad = jnp.concatenate([knope, jnp.zeros(knope.shape[:3] + (MLA_HP - MLA_NOPE,), wkv.dtype)], axis=-1)
    wk_pad = wk_pad.reshape(L, MLA_KV_RANK, MLA_HEADS * MLA_HP)
    wv = vv.reshape(L, MLA_KV_RANK, MLA_HEADS * MLA_V)
    return wq_pad.astype(BF16), wq_sw.astype(BF16), wk_pad.astype(BF16), wv.astype(BF16)


def _rope_tables(S):
    half = MLA_ROPE // 2
    pos = jnp.arange(S, dtype=jnp.int32)
    inv_freq = ROPE_BASE ** (-jnp.arange(half, dtype=F32) / half)
    ang = pos.astype(F32)[:, None] * inv_freq[None, :]
    cos, sin = jnp.cos(ang), jnp.sin(ang)
    cos2 = jnp.concatenate([cos, cos], axis=-1)
    sin2 = jnp.concatenate([-sin, sin], axis=-1)
    scale = (MLA_NOPE + MLA_ROPE) ** -0.5
    tail = jnp.zeros((S, MLA_HP - MLA_NOPE - MLA_ROPE), F32)
    cq_head = jnp.concatenate([jnp.ones((S, MLA_NOPE), F32), cos2, tail], axis=-1) * scale
    sq_head = jnp.concatenate([jnp.zeros((S, MLA_NOPE), F32), sin2, tail], axis=-1) * scale
    cqt = jnp.tile(cq_head, (1, MLA_HEADS))
    sqt = jnp.tile(sq_head, (1, MLA_HEADS))
    ckt = jnp.concatenate([cos2, sin2, jnp.zeros((S, LANES - 2 * MLA_ROPE), F32)], axis=-1)
    j = jnp.arange(LANES)[:, None]
    col = jnp.arange(MLA_HEADS * MLA_HP)[None, :]
    place = ((col % MLA_HP) == (MLA_NOPE + (j % MLA_ROPE))) & (j < 2 * MLA_ROPE)
    return cqt, sqt, ckt, place.astype(BF16)


def _pad_lanes(v, offset):
    out = jnp.zeros((1, LANES), F32)
    return lax.dynamic_update_slice(out, v.astype(F32).reshape(1, -1), (0, offset))


def kernel(x, w_in, mla_q_norm, mla_w_uq, mla_kv_norm, mla_w_ukv, swa_sinks, gdn_conv,
           gdn_a_log, gdn_dt_bias, gdn_norm, w_out, ln1_g, ln1_b, ln2_g, ln2_b,
           ffn_w_gate, ffn_w_up, ffn_w_down, moe_router, moe_w_gate, moe_w_up, moe_w_down):
    B, S, D = x.shape
    T = B * S
    xf = x.reshape(T, D)
    xb = xf.astype(BF16)

    w_in_b = _prep_w_in(w_in)
    wq_pad, wq_sw, wk_pad, wv = _prep_mla_weights(mla_w_uq, mla_w_ukv)
    cqt, sqt, ckt, place = _rope_tables(S)
    w_out_b = w_out.astype(BF16)
    ffn_g, ffn_u, ffn_d = ffn_w_gate.astype(BF16), ffn_w_up.astype(BF16), ffn_w_down.astype(BF16)
    moe_g, moe_u, moe_d = moe_w_gate.astype(BF16), moe_w_up.astype(BF16), moe_w_down.astype(BF16)
    router128 = jnp.pad(moe_router.astype(F32), ((0, 0), (0, 0), (0, LANES - N_EXPERTS)))

    for l in range(DEPTH):
        proj = _in_proj(xb, w_in_b, l)
        q, k, v = _mla_prep(proj, mla_q_norm[l].reshape(1, -1), mla_kv_norm[l].reshape(1, -1),
                            wq_pad, wq_sw, wk_pad, wv, l, place, cqt, sqt, ckt, S)
        h_mla = _mla_attn(q, k, v, B, S)
        h_swa = _swa(proj, swa_sinks[l].astype(F32), B, S)
        h_gdn = _gdn(proj, gdn_conv[l], _pad_lanes(gdn_a_log[l], GDN_HEADS), _pad_lanes(gdn_dt_bias[l], GDN_HEADS),
                     gdn_norm[l].reshape(1, -1), B, S)
        xf, xb = _out_proj_ln(h_mla, h_swa, h_gdn, xf, w_out_b, l,
                              ln1_g[l].reshape(1, -1), ln1_b[l].reshape(1, -1))
        lg, lb = ln2_g[l].reshape(1, -1), ln2_b[l].reshape(1, -1)
        if l % 2 == 0:
            xf, xb = _ffn_ln(xb, xf, ffn_g, ffn_u, ffn_d, l // 2, lg, lb)
        else:
            m = l // 2
            route = _router(xf, router128[m])
            tile_expert, n_used, row_token, row_dest = _moe_plan(route, T)
            o_rows = _moe_grouped(tile_expert, n_used, row_token, row_dest, xf, moe_g, moe_u, moe_d, m)
            xf, xb = _moe_combine_ln(o_rows, route, xf, lg, lb)
    return xf.reshape(B, S, D)
```
